```python
import math
import jax, jax.numpy as jnp
from jax import lax
import numpy as np

D_MODEL = 4096
BATCH = 4
SEQ = 2048
DEPTH = 4
DEC_BATCH = 8
DEC_SEQ = 1
PAST_LEN = 8192
PAGE_SIZE = 128

N_A_LAYERS = DEPTH // 2
N_B_LAYERS = DEPTH - N_A_LAYERS
D_RNN = D_MODEL
LRU_BLOCKS = 16
LRU_BS = D_RNN // LRU_BLOCKS
CONV_W = 4
LRU_C = 8.0
N_HEADS = 32
HEAD_DIM = D_MODEL // N_HEADS
N_KV = 4
Q_PER_KV = N_HEADS // N_KV
CMP_BLK = 32
CMP_STRIDE = 16
CMP_HID = 2 * HEAD_DIM
SLC_BLK = 64
TOP_N = 16
WINDOW = 512
N_BUCKETS = 32
MAX_DIST = 128
GLOBAL_QBLK = 32
WIN_QBLK = 128
NEG = -1e30
BIG = 1e30
EPS = 1e-6

kernel_name = "yoco_rglru_nsa_decoder_step"


def rms_norm(x, g):
    xf = x.astype(jnp.float32)
    y = xf * lax.rsqrt(jnp.mean(xf * xf, axis=-1, keepdims=True) + EPS)
    return (y * g.astype(jnp.float32)).astype(x.dtype)


def rel_bucket(d):
    d = jnp.maximum(d, 0)
    n_exact = N_BUCKETS // 2
    df = jnp.maximum(d, 1).astype(jnp.float32)
    large = n_exact + (jnp.log(df / n_exact) / math.log(MAX_DIST / n_exact)
                       * (N_BUCKETS - n_exact)).astype(jnp.int32)
    large = jnp.minimum(large, N_BUCKETS - 1)
    return jnp.where(d < n_exact, d, large)


def rglru_layer(x, pos, h0, conv0, norm_g, w_in, conv_w, conv_b, w_rg, b_rg, w_ig, b_ig, lam, w_out):
    B, T, _ = x.shape
    u = rms_norm(x, norm_g) @ w_in
    xb, gate = u[..., :D_RNN], u[..., D_RNN:]
    xpad = jnp.concatenate([conv0.astype(xb.dtype), xb], axis=1)
    xc = conv_b
    for k in range(CONV_W):
        xc = xc + conv_w[k] * xpad[:, k:k + T]
    xblk = xc.reshape(B, T, LRU_BLOCKS, LRU_BS)
    r = jax.nn.sigmoid((jnp.einsum('btnd,nde->btne', xblk, w_rg).reshape(B, T, D_RNN) + b_rg).astype(jnp.float32))
    i = jax.nn.sigmoid((jnp.einsum('btnd,nde->btne', xblk, w_ig).reshape(B, T, D_RNN) + b_ig).astype(jnp.float32))
    log_a = -LRU_C * r * jax.nn.softplus(-lam.astype(jnp.float32))
    a = jnp.exp(log_a)
    mult = jnp.where((pos == 0)[None, :, None], 1.0, jnp.sqrt(-jnp.expm1(2.0 * log_a)))
    b_in = mult * i * xc.astype(jnp.float32)

    def step(h, ab):
        h = ab[0] * h + ab[1]
        return h, h

    h_last, hs = lax.scan(step, h0.astype(jnp.float32), (a.transpose(1, 0, 2), b_in.transpose(1, 0, 2)))
    hs = hs.transpose(1, 0, 2).astype(x.dtype)
    y = (hs * jax.nn.silu(gate)) @ w_out
    return x + y, h_last.astype(h0.dtype), xpad[:, T:].astype(conv0.dtype)


def shared_kv_rows(x, kv_norm, w_kv):
    B, T, _ = x.shape
    kv = (rms_norm(x, kv_norm) @ w_kv).reshape(B, T, 3, 2, N_KV, HEAD_DIM)
    return kv[:, :, 0], kv[:, :, 1], kv[:, :, 2]


def global_context(cmp_rows, slc_rows, k_norm, cmp_pos, w_cmp1, w_cmp2):
    L = cmp_rows.shape[1]
    n_cmp = (L - CMP_BLK) // CMP_STRIDE + 1
    n_slc = -(-L // SLC_BLK)
    idx = jnp.arange(n_cmp)[:, None] * CMP_STRIDE + jnp.arange(CMP_BLK)[None, :]
    blk = cmp_rows[:, idx] + cmp_pos.transpose(1, 0, 2)[None, None, :, :, None, :]
    hid = jax.nn.silu(jnp.einsum('bnlsgd,sldh->bsgnh', blk, w_cmp1))
    comp = jnp.einsum('bsgnh,shd->bsgnd', hid, w_cmp2)
    kc = rms_norm(comp[:, 0], k_norm[0])
    vc = comp[:, 1]

    def to_blocks(v):
        B = v.shape[0]
        v = jnp.pad(v, ((0, 0), (0, n_slc * SLC_BLK - L), (0, 0), (0, 0)))
        return v.reshape(B, n_slc, SLC_BLK, N_KV, HEAD_DIM).transpose(0, 3, 1, 2, 4)

    ks = to_blocks(rms_norm(slc_rows[:, :, 0], k_norm[1]))
    vs = to_blocks(slc_rows[:, :, 1])
    cs = jnp.arange(n_cmp)[:, None] * CMP_STRIDE
    ss = jnp.arange(n_slc)[None, :] * SLC_BLK
    ov = jnp.minimum(cs + CMP_BLK, ss + SLC_BLK) - jnp.maximum(cs, ss)
    m_imp = jnp.maximum(ov, 0).astype(jnp.float32) / CMP_BLK
    return kc, vc, ks, vs, m_imp


def nsa_global(q, q_pos, kc, vc, ks, vs, tbl, m_imp):
    B, G = q.shape[:2]
    n_cmp = kc.shape[2]
    n_slc = ks.shape[2]
    n_sel = min(TOP_N, n_slc)
    d_c = q_pos[:, None] - (jnp.arange(n_cmp) * CMP_STRIDE + (CMP_BLK - 1))[None, :]
    c_ok = d_c >= 0
    bias_c = tbl[:, rel_bucket(d_c)].transpose(0, 3, 1, 2)
    lc = jnp.einsum('bgrqd,bgnd->bgrqn', q, kc).astype(jnp.float32) + bias_c
    pc = jax.nn.softmax(jnp.where(c_ok, lc, NEG), axis=-1) * c_ok
    o_cmp = jnp.einsum('bgrqn,bgnd->bgrqd', pc.astype(vc.dtype), vc)
    imp = jnp.einsum('bgrqn,nj->bgqj', pc, m_imp)
    cur = q_pos // SLC_BLK
    j = jnp.arange(n_slc)[None, :]
    forced = (j == 0) | (j == cur[:, None]) | (j == cur[:, None] - 1)
    allowed = j <= cur[:, None]
    score = jnp.where(forced, BIG, jnp.where(allowed, imp, NEG))
    top_val, top_idx = lax.top_k(score, n_sel)
    blk_ok = top_val > 0.5 * NEG
    bi = jnp.arange(B)[:, None, None, None]
    gi = jnp.arange(G)[None, :, None, None]
    ksel = ks[bi, gi, top_idx]
    vsel = vs[bi, gi, top_idx]
    kpos = top_idx[..., None] * SLC_BLK + jnp.arange(SLC_BLK)
    d_s = q_pos[None, None, :, None, None] - kpos
    s_ok = blk_ok[..., None] & (d_s >= 0)
    bias_s = tbl[gi[..., None], rel_bucket(d_s)].transpose(0, 1, 5, 2, 3, 4)
    ls = jnp.einsum('bgrqd,bgqnkd->bgrqnk', q, ksel).astype(jnp.float32) + bias_s
    ls = jnp.where(s_ok[:, :, None], ls, NEG).reshape(B, G, q.shape[2], q.shape[3], n_sel * SLC_BLK)
    ps = jax.nn.softmax(ls, axis=-1).reshape(B, G, q.shape[2], q.shape[3], n_sel, SLC_BLK)
    o_slc = jnp.einsum('bgrqnk,bgqnkd->bgrqd', ps.astype(vsel.dtype), vsel)
    return o_cmp, o_slc


def window_attn(q, q_pos, kw, vw, k_pos, tbl):
    d = q_pos[:, None] - k_pos[None, :]
    ok = (d >= 0) & (d <= WINDOW) & (k_pos[None, :] >= 0)
    bias = tbl[:, rel_bucket(d)].transpose(0, 3, 1, 2)
    lw = jnp.einsum('bgrqd,bgkd->bgrqk', q, kw).astype(jnp.float32) + bias
    pw = jax.nn.softmax(jnp.where(ok, lw, NEG), axis=-1)
    return jnp.einsum('bgrqk,bgkd->bgrqd', pw.astype(vw.dtype), vw)


def prompt_attend(q, ctx, win_rows, k_norm, tbl):
    kc, vc, ks, vs, m_imp = ctx
    B, G, R, T, HD = q.shape
    pos = jnp.arange(T, dtype=jnp.int32)
    nq = T // GLOBAL_QBLK
    qg = q.reshape(B, G, R, nq, GLOBAL_QBLK, HD).transpose(3, 0, 1, 2, 4, 5)
    o_cmp, o_slc = lax.map(lambda a: nsa_global(a[0], a[1], kc, vc, ks, vs, tbl, m_imp),
                           (qg, pos.reshape(nq, GLOBAL_QBLK)))
    o_cmp = o_cmp.transpose(1, 2, 3, 0, 4, 5).reshape(B, G, R, T, HD)
    o_slc = o_slc.transpose(1, 2, 3, 0, 4, 5).reshape(B, G, R, T, HD)
    pad = ((0, 0), (0, 0), (WINDOW, 0), (0, 0))
    kw_pad = jnp.pad(rms_norm(win_rows[:, :, 0], k_norm[2]).transpose(0, 2, 1, 3), pad)
    vw_pad = jnp.pad(win_rows[:, :, 1].transpose(0, 2, 1, 3), pad)
    nw = T // WIN_QBLK
    qw = q.reshape(B, G, R, nw, WIN_QBLK, HD).transpose(3, 0, 1, 2, 4, 5)

    def win_block(a):
        qb, c = a
        start = c * WIN_QBLK
        kb = lax.dynamic_slice_in_dim(kw_pad, start, WINDOW + WIN_QBLK, axis=2)
        vb = lax.dynamic_slice_in_dim(vw_pad, start, WINDOW + WIN_QBLK, axis=2)
        k_pos = start - WINDOW + jnp.arange(WINDOW + WIN_QBLK, dtype=jnp.int32)
        q_pos = start + jnp.arange(WIN_QBLK, dtype=jnp.int32)
        return window_attn(qb, q_pos, kb, vb, k_pos, tbl)

    o_win = lax.map(win_block, (qw, jnp.arange(nw, dtype=jnp.int32)))
    o_win = o_win.transpose(1, 2, 3, 0, 4, 5).reshape(B, G, R, T, HD)
    return o_cmp, o_slc, o_win


def sample_attend(q, q_pos, ctx, full_win, win_pos, k_norm, tbl):
    kc, vc, ks, vs, m_imp = ctx
    o_cmp, o_slc = nsa_global(q, q_pos, kc, vc, ks, vs, tbl, m_imp)
    kw = rms_norm(full_win[:, :, 0], k_norm[2]).transpose(0, 2, 1, 3)
    vw = full_win[:, :, 1].transpose(0, 2, 1, 3)
    o_win = window_attn(q, q_pos, kw, vw, win_pos, tbl)
    return o_cmp, o_slc, o_win


def nsa_layer(x, attend, norm_g, w_in, gate_bias, q_norm, w_out):
    B, T, _ = x.shape
    HQ = N_HEADS * HEAD_DIM
    u = rms_norm(x, norm_g) @ w_in
    q = rms_norm(u[..., :HQ].reshape(B, T, N_KV, Q_PER_KV, HEAD_DIM), q_norm) * HEAD_DIM ** -0.5
    q = q.transpose(0, 2, 3, 1, 4)
    gate = u[..., HQ:2 * HQ]
    bg = jax.nn.sigmoid((u[..., 2 * HQ:] + gate_bias).astype(jnp.float32))
    bg = bg.reshape(B, T, N_KV, Q_PER_KV, 3).transpose(0, 2, 3, 1, 4)
    o_cmp, o_slc, o_win = attend(q)
    o = bg[..., 0:1] * o_cmp + bg[..., 1:2] * o_slc + bg[..., 2:3] * o_win
    o = o.transpose(0, 3, 1, 2, 4).reshape(B, T, HQ).astype(x.dtype)
    return x + (o * jax.nn.silu(gate)) @ w_out


def setup_inputs(seed: int = 0) -> dict:
    key = jax.random.key(seed)
    ks = jax.random.split(key, 32)
    f = jnp.float32
    n_pages = PAST_LEN // PAGE_SIZE
    n_used = DEC_BATCH * n_pages
    n_pool = (5 * n_used + 3) // 4
    wb = min(WINDOW, PAST_LEN)
    A, Bn = N_A_LAYERS, N_B_LAYERS

    def nrm(k, shape, scale):
        return jax.random.normal(k, shape, f) * scale

    a0 = jax.random.uniform(ks[16], (A, D_RNN), f, 0.9, 0.999)
    s = a0 ** (1.0 / LRU_C)
    return {
        "x_prompt": nrm(ks[0], (BATCH, SEQ, D_MODEL), 1.0),
        "x_sample": nrm(ks[1], (DEC_BATCH, DEC_SEQ, D_MODEL), 1.0),
        "cache_cmp_kv": nrm(ks[2], (n_pool, PAGE_SIZE, 2, N_KV, HEAD_DIM), 1.0),
        "cache_slc_kv": nrm(ks[3], (n_pool, PAGE_SIZE, 2, N_KV, HEAD_DIM), 1.0),
        "state_win_kv": nrm(ks[4], (DEC_BATCH, wb, 2, N_KV, HEAD_DIM), 1.0),
        "state_lru_h": nrm(ks[5], (A, DEC_BATCH, D_RNN), 0.5),
        "state_conv": nrm(ks[6], (A, DEC_BATCH, CONV_W - 1, D_RNN), 1.0),
        "page_table": jax.random.permutation(ks[7], n_pool)[:n_used].reshape(DEC_BATCH, n_pages).astype(jnp.int32),
        "a_norm": 1.0 + nrm(ks[8], (A, D_MODEL), 0.02),
        "a_w_in": nrm(ks[9], (A, D_MODEL, 2 * D_RNN), D_MODEL ** -0.5),
        "a_conv_w": nrm(ks[10], (A, CONV_W, D_RNN), CONV_W ** -0.5),
        "a_conv_b": nrm(ks[11], (A, D_RNN), 0.02),
        "a_w_rg": nrm(ks[12], (A, LRU_BLOCKS, LRU_BS, LRU_BS), LRU_BS ** -0.5),
        "a_b_rg": nrm(ks[13], (A, D_RNN), 0.02),
        "a_w_ig": nrm(ks[14], (A, LRU_BLOCKS, LRU_BS, LRU_BS), LRU_BS ** -0.5),
        "a_b_ig": nrm(ks[15], (A, D_RNN), 0.02),
        "a_lambda": jnp.log(s) - jnp.log1p(-s),
        "a_w_out": nrm(ks[17], (A, D_RNN, D_MODEL), D_RNN ** -0.5),
        "kv_norm": 1.0 + nrm(ks[18], (D_MODEL,), 0.02),
        "w_kv": nrm(ks[19], (D_MODEL, 6 * N_KV * HEAD_DIM), D_MODEL ** -0.5),
        "k_norm": 1.0 + nrm(ks[20], (3, HEAD_DIM), 0.02),
        "cmp_pos": nrm(ks[21], (2, CMP_BLK, HEAD_DIM), 0.1),
        "w_cmp1": nrm(ks[22], (2, CMP_BLK, HEAD_DIM, CMP_HID), (CMP_BLK * HEAD_DIM) ** -0.5),
        "w_cmp2": nrm(ks[23], (2, CMP_HID, HEAD_DIM), CMP_HID ** -0.5),
        "rel_table": nrm(ks[24], (N_BUCKETS, N_HEADS), 0.5),
        "b_norm": 1.0 + nrm(ks[25], (Bn, D_MODEL), 0.02),
        "b_w_in": nrm(ks[26], (Bn, D_MODEL, 2 * N_HEADS * HEAD_DIM + 3 * N_HEADS), D_MODEL ** -0.5),
        "b_gate_bias": nrm(ks[27], (Bn, 3 * N_HEADS), 0.1),
        "b_q_norm": 1.0 + nrm(ks[28], (Bn, HEAD_DIM), 0.02),
        "b_w_out": nrm(ks[29], (Bn, N_HEADS * HEAD_DIM, D_MODEL), (N_HEADS * HEAD_DIM) ** -0.5),
    }


def reference(x_prompt, x_sample, cache_cmp_kv, cache_slc_kv, state_win_kv, state_lru_h,
              state_conv, page_table, a_norm, a_w_in, a_conv_w, a_conv_b, a_w_rg, a_b_rg,
              a_w_ig, a_b_ig, a_lambda, a_w_out, kv_norm, w_kv, k_norm, cmp_pos, w_cmp1,
              w_cmp2, rel_table, b_norm, b_w_in, b_gate_bias, b_q_norm, b_w_out):
    tbl = rel_table.astype(jnp.float32).reshape(N_BUCKETS, N_KV, Q_PER_KV).transpose(1, 0, 2)
    n_pages = PAST_LEN // PAGE_SIZE
    wb = min(WINDOW, PAST_LEN)
    pos_p = jnp.arange(SEQ, dtype=jnp.int32)
    pos_s = PAST_LEN + jnp.arange(DEC_SEQ, dtype=jnp.int32)
    xp, xs = x_prompt, x_sample
    h0_p = jnp.zeros((BATCH, D_RNN), x_prompt.dtype)
    conv0_p = jnp.zeros((BATCH, CONV_W - 1, D_RNN), x_prompt.dtype)
    p_h, p_c, s_h, s_c = [], [], [], []
    for layer in range(DEPTH):
        if layer < N_A_LAYERS:
            wa = (a_norm[layer], a_w_in[layer], a_conv_w[layer], a_conv_b[layer], a_w_rg[layer],
                  a_b_rg[layer], a_w_ig[layer], a_b_ig[layer], a_lambda[layer], a_w_out[layer])
            xp, hp, cp = rglru_layer(xp, pos_p, h0_p, conv0_p, *wa)
            xs, hs, cs = rglru_layer(xs, pos_s, state_lru_h[layer], state_conv[layer], *wa)
            p_h.append(hp)
            p_c.append(cp)
            s_h.append(hs)
            s_c.append(cs)
            continue
        if layer == N_A_LAYERS:
            p_cmp_kv, p_slc_kv, p_win_rows = shared_kv_rows(xp, kv_norm, w_kv)
            s_cmp_kv, s_slc_kv, s_win_rows = shared_kv_rows(xs, kv_norm, w_kv)
            ctx_p = global_context(p_cmp_kv, p_slc_kv, k_norm, cmp_pos, w_cmp1, w_cmp2)
            past_cmp = cache_cmp_kv[page_table].reshape(DEC_BATCH, n_pages * PAGE_SIZE, 2, N_KV, HEAD_DIM)
            past_slc = cache_slc_kv[page_table].reshape(DEC_BATCH, n_pages * PAGE_SIZE, 2, N_KV, HEAD_DIM)
            ctx_s = global_context(jnp.concatenate([past_cmp, s_cmp_kv], axis=1),
                                   jnp.concatenate([past_slc, s_slc_kv], axis=1),
                                   k_norm, cmp_pos, w_cmp1, w_cmp2)
            full_win = jnp.concatenate([state_win_kv, s_win_rows], axis=1)
            win_pos = PAST_LEN - wb + jnp.arange(wb + DEC_SEQ, dtype=jnp.int32)
            attend_p = lambda q: prompt_attend(q, ctx_p, p_win_rows, k_norm, tbl)
            attend_s = lambda q: sample_attend(q, pos_s, ctx_s, full_win, win_pos, k_norm, tbl)
        li = layer - N_A_LAYERS
        wb_l = (b_norm[li], b_w_in[li], b_gate_bias[li], b_q_norm[li], b_w_out[li])
        xp = nsa_layer(xp, attend_p, *wb_l)
        xs = nsa_layer(xs, attend_s, *wb_l)
    p_win_kv = p_win_rows[:, SEQ - min(WINDOW, SEQ):]
    s_win_kv = full_win[:, full_win.shape[1] - min(WINDOW, PAST_LEN + DEC_SEQ):]
    p_lru_h = jnp.stack(p_h)
    p_conv = jnp.stack(p_c)
    s_lru_h = jnp.stack(s_h)
    s_conv = jnp.stack(s_c)
    return (xp, xs, p_cmp_kv, p_slc_kv, p_win_kv, p_lru_h, p_conv,
            s_cmp_kv, s_slc_kv, s_win_kv, s_lru_h, s_conv)
```

```python
import functools
import math

import numpy as np
import jax
import jax.numpy as jnp
from jax import lax
from jax.experimental import pallas as pl
from jax.experimental.pallas import tpu as pltpu

F32 = jnp.float32
BF16 = jnp.bfloat16

EPS = 1e-6
NEG = -1e30
BIG = 1e30
LOWEST = -3e38
LRU_C = 8.0
CONV_W = 4
CMP_BLK = 32
CMP_STRIDE = 16
SLC_BLK = 64
TOP_N = 16
WINDOW = 512
N_BUCKETS = 32
MAX_DIST = 128
PAGE_SIZE = 128
QB = 128
LANES = 128
SAMPLE_ROWS = 16
VMEM_LIMIT = 48 * 1024 * 1024


def _cparams(*sem):
    return pltpu.CompilerParams(dimension_semantics=sem, vmem_limit_bytes=VMEM_LIMIT)


def _dot(a, b):
    return jnp.dot(a, b, preferred_element_type=F32)


def _dot_nt(a, b):
    return lax.dot_general(a, b, (((1,), (1,)), ((), ())), preferred_element_type=F32)


def _rms(x, g):
    return x * lax.rsqrt(jnp.mean(x * x, axis=-1, keepdims=True) + EPS) * g


def _sigmoid(x):
    return jax.nn.sigmoid(x)


def _silu(x):
    return x * jax.nn.sigmoid(x)


def _norm_kernel(x_ref, g_ref, o_ref):
    o_ref[...] = _rms(x_ref[...], g_ref[...]).astype(o_ref.dtype)


def _norm_cast(x, g):
    m, d = x.shape
    tm = min(m, 512)
    return pl.pallas_call(
        _norm_kernel,
        grid=(m // tm,),
        in_specs=[pl.BlockSpec((tm, d), lambda i: (i, 0)),
                  pl.BlockSpec((1, d), lambda i: (0, 0))],
        out_specs=pl.BlockSpec((tm, d), lambda i: (i, 0)),
        out_shape=jax.ShapeDtypeStruct((m, d), BF16),
        compiler_params=_cparams("parallel"),
        name="norm_cast",
    )(x, g.reshape(1, d))


def _mm_kernel(a_ref, w_ref, o_ref):
    o_ref[...] = _dot(a_ref[...], w_ref[...]).reshape(o_ref.shape)


def _mm_res_kernel(a_ref, w_ref, r_ref, o_ref):
    o_ref[...] = r_ref[...] + _dot(a_ref[...], w_ref[...])


def _matmul(a, w, res=None, tm=1024, tn=512, col_blocked=False):
    m, k = a.shape
    n = w.shape[1]
    tm = min(m, tm)
    tn = min(n, tn)
    grid = (m // tm, n // tn)
    in_specs = [pl.BlockSpec((tm, k), lambda i, j: (i, 0)),
                pl.BlockSpec((k, tn), lambda i, j: (0, j))]
    args = [a, w]
    if col_blocked:
        out_spec = pl.BlockSpec((1, tm, tn), lambda i, j: (j, i, 0))
        out_shape = jax.ShapeDtypeStruct((n // tn, m, tn), F32)
    else:
        out_spec = pl.BlockSpec((tm, tn), lambda i, j: (i, j))
        out_shape = jax.ShapeDtypeStruct((m, n), F32)
    body = _mm_kernel
    if res is not None:
        in_specs.append(pl.BlockSpec((tm, tn), lambda i, j: (i, j)))
        args.append(res)
        body = _mm_res_kernel
    return pl.pallas_call(
        body, grid=grid, in_specs=in_specs, out_specs=out_spec, out_shape=out_shape,
        compiler_params=_cparams("parallel", "arbitrary"),
        name="matmul",
    )(*args)


def _lru_coeffs(xc, wrg_ref, wig_ref, brg, big, lam, pos0_row):
    rows, cb = xc.shape
    bs = wrg_ref.shape[-1]
    xcb = xc.astype(BF16)
    r_parts, i_parts = [], []
    for n in range(cb // bs):
        xs = xcb[:, n * bs:(n + 1) * bs]
        r_parts.append(_dot(xs, wrg_ref[n]))
        i_parts.append(_dot(xs, wig_ref[n]))
    r = _sigmoid(jnp.concatenate(r_parts, axis=1) + brg)
    i = _sigmoid(jnp.concatenate(i_parts, axis=1) + big)
    nl = -lam
    softplus = jnp.maximum(nl, 0.0) + jnp.log1p(jnp.exp(-jnp.abs(nl)))
    log_a = -LRU_C * r * softplus
    a = jnp.exp(log_a)
    mult = jnp.sqrt(-jnp.tanh(log_a) * (a * a + 1.0))
    if pos0_row is not None:
        row = lax.broadcasted_iota(jnp.int32, (rows, cb), 0)
        mult = jnp.where(row == pos0_row, 1.0, mult)
    return a, mult * i * xc


def _lru_prompt_kernel(xb_ref, gate_ref, cw_ref, cb_ref, wrg_ref, brg_ref, wig_ref, big_ref,
                       lam_ref, h0_ref, c0_ref, g_ref, hl_ref, ct_ref, xpad_scr, h_scr):
    tc = pl.program_id(2)
    tt, cb = xb_ref.shape

    @pl.when(tc == 0)
    def _():
        h_scr[0:1, :] = h0_ref[0]
        xpad_scr[5:8, :] = c0_ref[0]

    xb = xb_ref[...]
    xpad_scr[8:8 + tt, :] = xb
    w = cw_ref[...]
    xc = cb_ref[...]
    for k in range(CONV_W):
        xc = xc + w[k:k + 1] * xpad_scr[5 + k:5 + k + tt, :]
    xpad_scr[5:8, :] = xpad_scr[5 + tt:8 + tt, :]

    a, b_in = _lru_coeffs(xc, wrg_ref, wig_ref, brg_ref[...], big_ref[...], lam_ref[...],
                           jnp.where(tc == 0, 0, -1))

    row = lax.broadcasted_iota(jnp.int32, (tt, cb), 0)
    d = 1
    while d < tt:
        a_sh = pltpu.roll(a, d, axis=0)
        b_sh = pltpu.roll(b_in, d, axis=0)
        keep = row >= d
        b_in = jnp.where(keep, a * b_sh + b_in, b_in)
        a = jnp.where(keep, a * a_sh, a)
        d *= 2
    h = a * h_scr[0:1, :] + b_in
    h_scr[0:1, :] = h[tt - 1:tt]

    g_ref[...] = (h * _silu(gate_ref[...])).astype(g_ref.dtype)

    @pl.when(tc == pl.num_programs(2) - 1)
    def _():
        hl_ref[0] = h[tt - 1:tt]
        ct_ref[0] = xb[tt - (CONV_W - 1):tt]


def _lru_prompt(u, batch, seq, cw, cb_, wrg, brg, wig, big, lam, h0, c0):
    m, c2 = u.shape
    c = c2 // 2
    cb = min(c, 1024)
    tt = min(seq, 256)
    nt = seq // tt
    ncb = c // cb
    bs = wrg.shape[-1]
    nb = cb // bs
    vec = lambda: pl.BlockSpec((1, cb), lambda b, j, t: (0, j))
    return pl.pallas_call(
        _lru_prompt_kernel,
        grid=(batch, ncb, nt),
        in_specs=[
            pl.BlockSpec((tt, cb), lambda b, j, t: (b * nt + t, j)),
            pl.BlockSpec((tt, cb), lambda b, j, t: (b * nt + t, ncb + j)),
            pl.BlockSpec((CONV_W, cb), lambda b, j, t: (0, j)),
            vec(),
            pl.BlockSpec((nb, bs, bs), lambda b, j, t: (j, 0, 0)),
            vec(),
            pl.BlockSpec((nb, bs, bs), lambda b, j, t: (j, 0, 0)),
            vec(),
            vec(),
            pl.BlockSpec((1, 1, cb), lambda b, j, t: (b, 0, j)),
            pl.BlockSpec((1, CONV_W - 1, cb), lambda b, j, t: (b, 0, j)),
        ],
        out_specs=[
            pl.BlockSpec((tt, cb), lambda b, j, t: (b * nt + t, j)),
            pl.BlockSpec((1, 1, cb), lambda b, j, t: (b, 0, j)),
            pl.BlockSpec((1, CONV_W - 1, cb), lambda b, j, t: (b, 0, j)),
        ],
        out_shape=[
            jax.ShapeDtypeStruct((m, c), BF16),
            jax.ShapeDtypeStruct((batch, 1, c), F32),
            jax.ShapeDtypeStruct((batch, CONV_W - 1, c), F32),
        ],
        scratch_shapes=[pltpu.VMEM((tt + 8, cb), F32), pltpu.VMEM((8, cb), F32)],
        compiler_params=_cparams("parallel", "parallel", "arbitrary"),
        name="lru_prompt",
    )(u, u, cw, cb_.reshape(1, c), wrg, brg.reshape(1, c), wig, big.reshape(1, c),
      lam.reshape(1, c), h0.reshape(batch, 1, c), c0)


def _lru_sample_kernel(xb_ref, gate_ref, cw_ref, cb_ref, wrg_ref, brg_ref, wig_ref, big_ref,
                       lam_ref, h0_ref, c0_ref, g_ref, h_ref):
    w = cw_ref[...]
    xc = cb_ref[...]
    for k in range(CONV_W - 1):
        xc = xc + w[k:k + 1] * c0_ref[k]
    xc = xc + w[CONV_W - 1:CONV_W] * xb_ref[...]
    a, b_in = _lru_coeffs(xc, wrg_ref, wig_ref, brg_ref[...], big_ref[...], lam_ref[...], None)
    h = a * h0_ref[...] + b_in
    h_ref[...] = h
    g_ref[...] = (h * _silu(gate_ref[...])).astype(g_ref.dtype)


def _lru_sample(u, cw, cb_, wrg, brg, wig, big, lam, h0, c0):
    rows, c2 = u.shape
    c = c2 // 2
    cb = min(c, 1024)
    ncb = c // cb
    bs = wrg.shape[-1]
    nb = cb // bs
    vec = lambda: pl.BlockSpec((1, cb), lambda j: (0, j))
    return pl.pallas_call(
        _lru_sample_kernel,
        grid=(ncb,),
        in_specs=[
            pl.BlockSpec((rows, cb), lambda j: (0, j)),
            pl.BlockSpec((rows, cb), lambda j: (0, ncb + j)),
            pl.BlockSpec((CONV_W, cb), lambda j: (0, j)),
            vec(),
            pl.BlockSpec((nb, bs, bs), lambda j: (j, 0, 0)),
            vec(),
            pl.BlockSpec((nb, bs, bs), lambda j: (j, 0, 0)),
            vec(),
            vec(),
            pl.BlockSpec((rows, cb), lambda j: (0, j)),
            pl.BlockSpec((CONV_W - 1, rows, cb), lambda j: (0, 0, j)),
        ],
        out_specs=[pl.BlockSpec((rows, cb), lambda j: (0, j)),
                   pl.BlockSpec((rows, cb), lambda j: (0, j))],
        out_shape=[jax.ShapeDtypeStruct((rows, c), BF16),
                   jax.ShapeDtypeStruct((rows, c), F32)],
        compiler_params=_cparams("parallel"),
        name="lru_sample",
    )(u, u, cw, cb_.reshape(1, c), wrg, brg.reshape(1, c), wig, big.reshape(1, c),
      lam.reshape(1, c), h0, c0)


def _page_gather_kernel(pt_ref, c_ref, o_ref, slab_scr):
    del pt_ref
    n_sg = o_ref.shape[1]
    hd = c_ref.shape[2] // n_sg
    rows = PAGE_SIZE // CMP_STRIDE
    for sg in range(n_sg):
        slab_scr[sg] = c_ref[0, :, sg * hd:(sg + 1) * hd]
        for l in range(CMP_STRIDE):
            o_ref[0, sg, :, l * hd:(l + 1) * hd] = slab_scr[sg, pl.ds(l, rows, stride=CMP_STRIDE), :]


def _page_gather(cache, page_table, n_sg, hd):
    bs, n_pages = page_table.shape
    n_pool = cache.shape[0]
    rows = PAGE_SIZE // CMP_STRIDE
    cache2 = cache.reshape(n_pool, PAGE_SIZE, n_sg * hd)
    return pl.pallas_call(
        _page_gather_kernel,
        grid_spec=pltpu.PrefetchScalarGridSpec(
            num_scalar_prefetch=1,
            grid=(bs, n_pages),
            in_specs=[pl.BlockSpec((1, PAGE_SIZE, n_sg * hd),
                                   lambda b, j, pt: (pt[b * n_pages + j], 0, 0))],
            out_specs=pl.BlockSpec((1, n_sg, rows, CMP_STRIDE * hd), lambda b, j, pt: (b, 0, j, 0)),
            scratch_shapes=[pltpu.VMEM((n_sg, PAGE_SIZE, hd), F32)],
        ),
        out_shape=jax.ShapeDtypeStruct((bs, n_sg, n_pages * rows, CMP_STRIDE * hd), F32),
        compiler_params=_cparams("parallel", "arbitrary"),
        name="page_gather",
    )(page_table.reshape(-1), cache2)


def _compress_kernel(r_ref, pos_ref, w1_ref, w2_ref, kn_ref, o_ref, q_scr, *, n_kv):
    sg = pl.program_id(1)
    rows = r_ref[0, 0]
    nr = rows.shape[0]
    hid = w1_ref.shape[2] // 2
    xa = (rows + pos_ref[0, 0:1]).astype(BF16)
    xb = (rows + pos_ref[0, 1:2]).astype(BF16)
    w1 = w1_ref[0]
    p = _dot(xa, w1[:, :hid])
    q_scr[0:nr, :] = _dot(xb, w1[:, hid:])
    q_scr[nr:nr + 8, :] = jnp.zeros((8, hid), F32)
    pre = p + q_scr[1:nr + 1, :]
    comp = _dot(_silu(pre).astype(BF16), w2_ref[0])
    normed = _rms(comp, kn_ref[...])
    o_ref[0, 0] = jnp.where(sg < n_kv, normed, comp).astype(o_ref.dtype)


def _compress(r16, posab, w1ab, w2, k_norm0, n_kv):
    b, n_sg, nr, kk = r16.shape
    hid2 = w1ab.shape[2]
    hd = w2.shape[2]
    return pl.pallas_call(
        functools.partial(_compress_kernel, n_kv=n_kv),
        grid=(b, n_sg),
        in_specs=[
            pl.BlockSpec((1, 1, nr, kk), lambda i, s: (i, s, 0, 0)),
            pl.BlockSpec((1, 2, kk), lambda i, s: (s // n_kv, 0, 0)),
            pl.BlockSpec((1, kk, hid2), lambda i, s: (s // n_kv, 0, 0)),
            pl.BlockSpec((1, hid2 // 2, hd), lambda i, s: (s // n_kv, 0, 0)),
            pl.BlockSpec((1, hd), lambda i, s: (0, 0)),
        ],
        out_specs=pl.BlockSpec((1, 1, nr, hd), lambda i, s: (i, s, 0, 0)),
        out_shape=jax.ShapeDtypeStruct((b, n_sg, nr, hd), BF16),
        scratch_shapes=[pltpu.VMEM((nr + 8, hid2 // 2), F32)],
        compiler_params=_cparams("parallel", "arbitrary"),
        name="compress",
    )(r16, posab, w1ab, w2, k_norm0.reshape(1, hd))


def _attn_prompt_kernel(uq_ref, ug_ref, bg_ref, gb_ref, ks_ref, vs_ref, kw_ref, vw_ref,
                        kc_ref, vc_ref, wb_ref, bc_ref, mi_ref, ex_ref, qn_ref, kn_ref, og_ref,
                        ks_scr, vs_scr, kw_scr, vw_scr, madd_scr, m_scr, l_scr, acc_scr,
                        *, n_heads_grp, n_slc):
    r_ = n_heads_grp
    c = pl.program_id(2)
    t0 = c * QB
    seq, hd = ks_ref.shape[1], ks_ref.shape[2]
    rq = r_ * QB
    wk = WINDOW + QB
    n_kt = seq // QB

    @pl.when(c == 0)
    def _():
        ks_scr[...] = _rms(ks_ref[0], kn_ref[1:2]).astype(BF16)
        vs_scr[...] = vs_ref[0].astype(BF16)
        kw_scr[0:WINDOW, :] = jnp.zeros((WINDOW, hd), BF16)
        vw_scr[0:WINDOW, :] = jnp.zeros((WINDOW, hd), BF16)
        kw_scr[WINDOW:WINDOW + seq, :] = _rms(kw_ref[0], kn_ref[2:3]).astype(BF16)
        vw_scr[WINDOW:WINDOW + seq, :] = vw_ref[0].astype(BF16)

    uq = uq_ref[...]
    scale = hd ** -0.5
    qb = jnp.concatenate(
        [_rms(uq[:, r * hd:(r + 1) * hd], qn_ref[...]) * scale for r in range(r_)],
        axis=0).astype(BF16)

    n_c = kc_ref.shape[2]
    lc = _dot_nt(qb, kc_ref[0, 0]).reshape(r_, QB, n_c) + bc_ref[...]
    tq = t0 + lax.broadcasted_iota(jnp.int32, (QB, n_c), 0)
    nn = lax.broadcasted_iota(jnp.int32, (QB, n_c), 1)
    c_ok = (tq - (nn * CMP_STRIDE + (CMP_BLK - 1)) >= 0)[None]
    lc = jnp.where(c_ok, lc, NEG)
    e = jnp.exp(lc - jnp.max(lc, axis=-1, keepdims=True))
    pc = jnp.where(c_ok, e / jnp.sum(e, axis=-1, keepdims=True), 0.0)
    pcb = pc.reshape(rq, n_c).astype(BF16)
    o_cmp = _dot(pcb, vc_ref[0, 0])
    imp = jnp.sum(_dot(pcb, mi_ref[...]).reshape(r_, QB, LANES), axis=0)

    jj = lax.broadcasted_iota(jnp.int32, (QB, LANES), 1)
    cur = (t0 + lax.broadcasted_iota(jnp.int32, (QB, LANES), 0)) // SLC_BLK
    forced = (jj == 0) | (jj == cur) | (jj == cur - 1)
    allowed = jj <= cur
    score = jnp.where(forced, BIG, jnp.where(allowed, imp, NEG))
    score = jnp.where(jj < n_slc, score, LOWEST)
    rank = jnp.zeros((QB, LANES), F32)
    for k in range(n_slc):
        sk = score[:, k:k + 1]
        ahead = jnp.where(sk > score, 1.0, jnp.where((sk == score) & (k < jj), 1.0, 0.0))
        rank = rank + ahead
    sel = jnp.where((rank < min(TOP_N, n_slc)) & allowed, 1.0, 0.0).astype(BF16)
    selx = _dot(sel, ex_ref[...])
    for kt in range(n_kt):
        madd_scr[kt] = (selx[:, kt * QB:(kt + 1) * QB] - 1.0) * BIG

    m_scr[...] = jnp.full((rq, LANES), NEG, F32)
    l_scr[...] = jnp.zeros((rq, LANES), F32)
    acc_scr[...] = jnp.zeros((rq, hd), F32)

    def flash_tile(kt, bias):
        k0 = pl.multiple_of(kt * QB, QB)
        s3 = _dot_nt(qb, ks_scr[pl.ds(k0, QB), :]).reshape(r_, QB, QB) + madd_scr[kt][None]
        if bias is not None:
            s3 = s3 + bias
        s = s3.reshape(rq, QB)
        m_prev = m_scr[...]
        m_new = jnp.maximum(m_prev, jnp.max(s, axis=-1, keepdims=True))
        alpha = jnp.exp(m_prev - m_new)
        p = jnp.exp(s - m_new)
        l_scr[...] = alpha * l_scr[...] + jnp.sum(p, axis=-1, keepdims=True)
        acc_scr[...] = alpha * acc_scr[...] + _dot(p.astype(BF16), vs_scr[pl.ds(k0, QB), :])
        m_scr[...] = m_new

    def far_body(kt, carry):
        flash_tile(kt, None)
        return carry

    lax.fori_loop(0, jnp.maximum(c - 1, 0), far_body, 0)

    @pl.when(c >= 1)
    def _():
        flash_tile(c - 1, wb_ref[:, :, WINDOW - QB:WINDOW])

    flash_tile(c, wb_ref[:, :, WINDOW:WINDOW + QB])
    o_slc = acc_scr[...] / l_scr[...]

    w0 = pl.multiple_of(t0, QB)
    kpos_ok = lax.broadcasted_iota(jnp.int32, (1, 1, wk), 2) >= WINDOW - t0
    lw = _dot_nt(qb, kw_scr[pl.ds(w0, wk), :]).reshape(r_, QB, wk) + wb_ref[...]
    lw = lw + jnp.where(kpos_ok, 0.0, NEG)
    ew = jnp.exp(lw - jnp.max(lw, axis=-1, keepdims=True))
    pw = (ew / jnp.sum(ew, axis=-1, keepdims=True)).reshape(rq, wk).astype(BF16)
    o_win = _dot(pw, vw_scr[pl.ds(w0, wk), :])

    bg = _sigmoid(bg_ref[0] + gb_ref[0])
    ug = ug_ref[...]
    for r in range(r_):
        rows = slice(r * QB, (r + 1) * QB)
        o = (bg[:, 3 * r:3 * r + 1] * o_cmp[rows] + bg[:, 3 * r + 1:3 * r + 2] * o_slc[rows]
             + bg[:, 3 * r + 2:3 * r + 3] * o_win[rows])
        og_ref[:, r * hd:(r + 1) * hd] = (o * _silu(ug[:, r * hd:(r + 1) * hd])).astype(og_ref.dtype)


def _attn_prompt(u, bgt, gbias, kv3, comp, wbias, biasc, mimp, expand, q_norm, k_norm,
                 batch, seq, n_kv, n_heads):
    m = u.shape[0]
    hd = k_norm.shape[1]
    r_ = n_heads // n_kv
    hq = n_heads * hd
    nq = seq // QB
    n_c = comp.shape[2]
    n_slc = seq // SLC_BLK
    rq = r_ * QB
    wk = WINDOW + QB
    g3 = lambda f: (lambda b, g, c: f(b, g, c))
    return pl.pallas_call(
        functools.partial(_attn_prompt_kernel, n_heads_grp=r_, n_slc=n_slc),
        grid=(batch, n_kv, nq),
        in_specs=[
            pl.BlockSpec((QB, r_ * hd), lambda b, g, c: (b * nq + c, g)),
            pl.BlockSpec((QB, r_ * hd), lambda b, g, c: (b * nq + c, n_kv + g)),
            pl.BlockSpec((1, QB, 3 * r_), lambda b, g, c: (g, b * nq + c, 0)),
            pl.BlockSpec((1, 1, 3 * r_), lambda b, g, c: (g, 0, 0)),
            pl.BlockSpec((1, seq, hd), lambda b, g, c: (1, b, g)),
            pl.BlockSpec((1, seq, hd), lambda b, g, c: (1, b, n_kv + g)),
            pl.BlockSpec((1, seq, hd), lambda b, g, c: (2, b, g)),
            pl.BlockSpec((1, seq, hd), lambda b, g, c: (2, b, n_kv + g)),
            pl.BlockSpec((1, 1, n_c, hd), lambda b, g, c: (b, g, 0, 0)),
            pl.BlockSpec((1, 1, n_c, hd), lambda b, g, c: (b, n_kv + g, 0, 0)),
            pl.BlockSpec((r_, QB, wk), lambda b, g, c: (g, 0, 0)),
            pl.BlockSpec((r_, QB, n_c), lambda b, g, c: (g, c, 0)),
            pl.BlockSpec((n_c, LANES), lambda b, g, c: (0, 0)),
            pl.BlockSpec((LANES, seq), lambda b, g, c: (0, 0)),
            pl.BlockSpec((1, hd), lambda b, g, c: (0, 0)),
            pl.BlockSpec((3, hd), lambda b, g, c: (0, 0)),
        ],
        out_specs=pl.BlockSpec((QB, r_ * hd), lambda b, g, c: (b * nq + c, g)),
        out_shape=jax.ShapeDtypeStruct((m, hq), BF16),
        scratch_shapes=[
            pltpu.VMEM((seq, hd), BF16), pltpu.VMEM((seq, hd), BF16),
            pltpu.VMEM((seq + WINDOW, hd), BF16), pltpu.VMEM((seq + WINDOW, hd), BF16),
            pltpu.VMEM((seq // QB, QB, QB), F32),
            pltpu.VMEM((rq, LANES), F32), pltpu.VMEM((rq, LANES), F32), pltpu.VMEM((rq, hd), F32),
        ],
        compiler_params=_cparams("parallel", "parallel", "arbitrary"),
        name="attn_prompt",
    )(u, u, bgt, gbias, kv3, kv3, kv3, kv3, comp, comp, wbias, biasc, mimp, expand,
      q_norm.reshape(1, hd), k_norm)


def _attn_sample_cmp_kernel(q_ref, kc_ref, vc_ref, cb_ref, mi_ref, qn_ref, oc_ref, sel_ref,
                            *, n_slc):
    r_, hd = q_ref.shape[1], q_ref.shape[2]
    qb = (_rms(q_ref[0], qn_ref[...]) * hd ** -0.5).astype(BF16)
    lc = _dot_nt(qb, kc_ref[0, 0]) + cb_ref[0]
    e = jnp.exp(lc - jnp.max(lc, axis=-1, keepdims=True))
    pcb = (e / jnp.sum(e, axis=-1, keepdims=True)).astype(BF16)
    oc_ref[0] = _dot(pcb, vc_ref[0, 0])
    nsp = mi_ref.shape[1]
    imp = jnp.sum(_dot(pcb, mi_ref[...]), axis=0, keepdims=True)

    jl = lax.broadcasted_iota(jnp.int32, (1, nsp), 1)
    forced = (jl == 0) | (jl == n_slc - 1) | (jl == n_slc - 2)
    score = jnp.where(jl < n_slc, jnp.where(forced, BIG, imp), LOWEST)
    s_rows = jnp.broadcast_to(score, (nsp, nsp))
    kk = lax.broadcasted_iota(jnp.int32, (nsp, nsp), 0)
    jj = lax.broadcasted_iota(jnp.int32, (nsp, nsp), 1)
    s_col = jnp.sum(jnp.where(kk == jj, s_rows, 0.0), axis=1, keepdims=True)
    ahead = jnp.where(s_col > s_rows, 1.0, jnp.where((s_col == s_rows) & (kk < jj), 1.0, 0.0))
    rank = jnp.sum(ahead, axis=0, keepdims=True)
    lane = lax.broadcasted_iota(jnp.int32, (1, LANES), 1)
    out = jnp.zeros((1, LANES), F32)
    jf = jl.astype(F32)
    for k in range(min(TOP_N, n_slc)):
        idx_k = jnp.sum(jnp.where(rank == k, jf, 0.0), axis=1, keepdims=True)
        out = out + jnp.where(lane == k, idx_k, 0.0)
    sel_ref[0, 0] = jnp.broadcast_to(out, (8, LANES)).astype(jnp.int32)


def _attn_sample_cmp(q3, comp, cbias, mimp, q_norm, n_kv, n_slc):
    bs, n_heads, hd = q3.shape
    r_ = n_heads // n_kv
    nr = comp.shape[2]
    nsp = mimp.shape[1]
    return pl.pallas_call(
        functools.partial(_attn_sample_cmp_kernel, n_slc=n_slc),
        grid=(bs, n_kv),
        in_specs=[
            pl.BlockSpec((1, r_, hd), lambda b, g: (b, g, 0)),
            pl.BlockSpec((1, 1, nr, hd), lambda b, g: (b, g, 0, 0)),
            pl.BlockSpec((1, 1, nr, hd), lambda b, g: (b, n_kv + g, 0, 0)),
            pl.BlockSpec((1, r_, nr), lambda b, g: (g, 0, 0)),
            pl.BlockSpec((nr, nsp), lambda b, g: (0, 0)),
            pl.BlockSpec((1, hd), lambda b, g: (0, 0)),
        ],
        out_specs=[pl.BlockSpec((1, r_, hd), lambda b, g: (b, g, 0)),
                   pl.BlockSpec((1, 1, 8, LANES), lambda b, g: (b, g, 0, 0))],
        out_shape=[jax.ShapeDtypeStruct((bs, n_heads, hd), F32),
                   jax.ShapeDtypeStruct((bs, n_kv, 8, LANES), jnp.int32)],
        compiler_params=_cparams("parallel", "parallel"),
        name="attn_sample_cmp",
    )(q3, comp, comp, cbias, mimp, q_norm.reshape(1, hd))


def _attn_sample_kernel(sel_ref, pt_ref, q_ref, gate_ref, bg_ref, gb_ref, oc_ref, ksn_ref, vsn_ref,
                        kw_ref, vw_ref, kwn_ref, vwn_ref, tb_ref, wb_ref, qn_ref, kn_ref, *rest,
                        n_sel, n_slc, n_kv):
    del pt_ref
    k_refs, v_refs, o_ref = rest[:n_sel], rest[n_sel:2 * n_sel], rest[2 * n_sel]
    b, g = pl.program_id(0), pl.program_id(1)
    r_, hd = q_ref.shape[1], q_ref.shape[2]
    qb = (_rms(q_ref[0], qn_ref[...]) * hd ** -0.5).astype(BF16)

    k_new = ksn_ref[0, 0]
    v_new = vsn_ref[0, 0]
    logits, values = [], []
    for k in range(n_sel):
        idx = sel_ref[(b * n_kv + g) * TOP_N + k]
        is_new = idx == n_slc - 1
        kt = jnp.where(is_new, jnp.broadcast_to(k_new, (SLC_BLK, hd)), k_refs[k][0])
        vt = jnp.where(is_new, jnp.broadcast_to(v_new, (SLC_BLK, hd)), v_refs[k][0])
        logits.append(_dot_nt(qb, _rms(kt, kn_ref[1:2]).astype(BF16)) + tb_ref[0, idx])
        values.append(vt.astype(BF16))
    m = logits[0].max(axis=-1, keepdims=True)
    for s in logits[1:]:
        m = jnp.maximum(m, s.max(axis=-1, keepdims=True))
    es = [jnp.exp(s - m) for s in logits]
    den = es[0].sum(axis=-1, keepdims=True)
    for e in es[1:]:
        den = den + e.sum(axis=-1, keepdims=True)
    o_slc = jnp.zeros((r_, hd), F32)
    for e, v in zip(es, values):
        o_slc = o_slc + _dot((e / den).astype(BF16), v)

    n_w = kw_ref.shape[1]
    lw = _dot_nt(qb, _rms(kw_ref[0], kn_ref[2:3]).astype(BF16)) + wb_ref[0, :, 0:n_w]
    kwn = _rms(kwn_ref[0, 0], kn_ref[2:3]).astype(BF16).astype(F32)
    l_new = jnp.sum(qb.astype(F32) * kwn, axis=-1, keepdims=True) + wb_ref[0, :, n_w:n_w + 1]
    mw = jnp.maximum(lw.max(axis=-1, keepdims=True), l_new)
    ew, e_new = jnp.exp(lw - mw), jnp.exp(l_new - mw)
    dw = ew.sum(axis=-1, keepdims=True) + e_new
    o_win = (_dot((ew / dw).astype(BF16), vw_ref[0].astype(BF16))
             + (e_new / dw).astype(BF16).astype(F32) * vwn_ref[0, 0].astype(BF16).astype(F32))

    bg = _sigmoid(bg_ref[0] + gb_ref[...])
    o = bg[:, 0:1] * oc_ref[0] + bg[:, 1:2] * o_slc + bg[:, 2:3] * o_win
    o_ref[0] = o * _silu(gate_ref[0])


def _attn_sample(sel, page_table, q3, gate3, bg3, gbias2, o_cmp, cache_slc, slc_new, win_state,
                 win_new, tb, wb, q_norm, k_norm, n_kv, n_slc):
    bs, n_heads, hd = q3.shape
    r_ = n_heads // n_kv
    n_pages = page_table.shape[1]
    n_pool = cache_slc.shape[0]
    n_sel = min(TOP_N, n_slc)
    cols = 2 * n_kv * hd
    halves = PAGE_SIZE // SLC_BLK
    cache2 = cache_slc.reshape(n_pool * halves, SLC_BLK, cols)
    n_w = win_state.shape[1]
    win2 = win_state.reshape(bs, n_w, cols)

    def blk_map(k, off):
        def index_map(b, g, sel_ref, pt_ref):
            idx = jnp.minimum(sel_ref[(b * n_kv + g) * TOP_N + k], n_slc - 2)
            page = pt_ref[b * n_pages + idx // halves]
            return (page * halves + idx % halves, 0, off + g)
        return index_map

    head_spec = lambda: pl.BlockSpec((1, r_, hd), lambda b, g, s, p: (b, g, 0))
    in_specs = [
        head_spec(), head_spec(),
        pl.BlockSpec((1, r_, 3), lambda b, g, s, p: (b, g, 0)),
        pl.BlockSpec((r_, 3), lambda b, g, s, p: (g, 0)),
        head_spec(),
        pl.BlockSpec((1, 1, 1, hd), lambda b, g, s, p: (b, g, 0, 0)),
        pl.BlockSpec((1, 1, 1, hd), lambda b, g, s, p: (b, n_kv + g, 0, 0)),
        pl.BlockSpec((1, n_w, hd), lambda b, g, s, p: (b, 0, g)),
        pl.BlockSpec((1, n_w, hd), lambda b, g, s, p: (b, 0, n_kv + g)),
        pl.BlockSpec((1, 1, 1, hd), lambda b, g, s, p: (b, g, 0, 0)),
        pl.BlockSpec((1, 1, 1, hd), lambda b, g, s, p: (b, n_kv + g, 0, 0)),
        pl.BlockSpec((1, n_slc, r_, SLC_BLK), lambda b, g, s, p: (g, 0, 0, 0)),
        pl.BlockSpec((1, r_, wb.shape[2]), lambda b, g, s, p: (g, 0, 0)),
        pl.BlockSpec((1, hd), lambda b, g, s, p: (0, 0)),
        pl.BlockSpec((3, hd), lambda b, g, s, p: (0, 0)),
    ]
    in_specs += [pl.BlockSpec((1, SLC_BLK, hd), blk_map(k, 0)) for k in range(n_sel)]
    in_specs += [pl.BlockSpec((1, SLC_BLK, hd), blk_map(k, n_kv)) for k in range(n_sel)]
    return pl.pallas_call(
        functools.partial(_attn_sample_kernel, n_sel=n_sel, n_slc=n_slc, n_kv=n_kv),
        grid_spec=pltpu.PrefetchScalarGridSpec(
            num_scalar_prefetch=2,
            grid=(bs, n_kv),
            in_specs=in_specs,
            out_specs=pl.BlockSpec((1, r_, hd), lambda b, g, s, p: (b, g, 0)),
        ),
        out_shape=jax.ShapeDtypeStruct((bs, n_heads, hd), F32),
        compiler_params=_cparams("arbitrary", "arbitrary"),
        name="attn_sample",
    )(sel, page_table.reshape(-1), q3, gate3, bg3, gbias2, o_cmp, slc_new, slc_new, win2, win2,
      win_new, win_new, tb, wb, q_norm.reshape(1, hd), k_norm, *([cache2] * (2 * n_sel)))


def _bucket_np(d):
    d = np.maximum(d, 0)
    n_exact = N_BUCKETS // 2
    df = np.maximum(d, 1).astype(np.float64)
    large = n_exact + (np.log(df / n_exact) / math.log(MAX_DIST / n_exact)
                       * (N_BUCKETS - n_exact)).astype(np.int64)
    return np.where(d < n_exact, d, np.minimum(large, N_BUCKETS - 1))


def _dist_bias(rel_table, d, valid, shift):
    fd = rel_table.astype(F32)[_bucket_np(np.arange(MAX_DIST + 1))]
    if shift:
        fd = fd - fd[MAX_DIST:MAX_DIST + 1]
    vals = jnp.moveaxis(fd[np.clip(d, 0, MAX_DIST)], -1, 0)
    return jnp.where(jnp.asarray(valid)[None], vals, NEG)


def _overlap_np(n_cmp, n_slc, rows, cols):
    cs = np.arange(n_cmp)[:, None] * CMP_STRIDE
    ss = np.arange(n_slc)[None, :] * SLC_BLK
    ov = np.minimum(cs + CMP_BLK, ss + SLC_BLK) - np.maximum(cs, ss)
    out = np.zeros((rows, cols), np.float32)
    out[:n_cmp, :n_slc] = np.maximum(ov, 0).astype(np.float32) / CMP_BLK
    return out


def _round_up(x, m):
    return (x + m - 1) // m * m


def kernel(x_prompt, x_sample, cache_cmp_kv, cache_slc_kv, state_win_kv, state_lru_h, state_conv,
           page_table, a_norm, a_w_in, a_conv_w, a_conv_b, a_w_rg, a_b_rg, a_w_ig, a_b_ig, a_lambda,
           a_w_out, kv_norm, w_kv, k_norm, cmp_pos, w_cmp1, w_cmp2, rel_table, b_norm, b_w_in,
           b_gate_bias, b_q_norm, b_w_out):
    batch, seq, d_model = x_prompt.shape
    bs = x_sample.shape[0]
    n_a, n_b = a_norm.shape[0], b_norm.shape[0]
    d_rnn = a_w_in.shape[2] // 2
    n_kv, hd = cache_cmp_kv.shape[3], cache_cmp_kv.shape[4]
    n_heads = rel_table.shape[1]
    r_ = n_heads // n_kv
    hq = n_heads * hd
    n_sg = 2 * n_kv
    cols = n_sg * hd
    n_pages = page_table.shape[1]
    past = n_pages * PAGE_SIZE
    m = batch * seq
    assert x_sample.shape[1] == 1 and seq % QB == 0 and seq >= WINDOW and hd == LANES
    assert seq // CMP_STRIDE == LANES and past >= WINDOW and state_win_kv.shape[1] == WINDOW

    xp = x_prompt.reshape(m, d_model)
    xs = jnp.pad(x_sample.reshape(bs, d_model), ((0, SAMPLE_ROWS - bs), (0, 0)))
    pad_rows = lambda a: jnp.pad(a, ((0, SAMPLE_ROWS - bs), (0, 0)))

    p_h, p_c, s_h, s_c = [], [], [], []
    h0_p = jnp.zeros((batch, d_rnn), F32)
    c0_p = jnp.zeros((batch, CONV_W - 1, d_rnn), F32)
    for l in range(n_a):
        w_in, w_out = a_w_in[l].astype(BF16), a_w_out[l].astype(BF16)
        wrg, wig = a_w_rg[l].astype(BF16), a_w_ig[l].astype(BF16)
        lru_w = (a_conv_w[l], a_conv_b[l], wrg, a_b_rg[l], wig, a_b_ig[l], a_lambda[l])
        u = _matmul(_norm_cast(xp, a_norm[l]), w_in)
        gp, hl, ct = _lru_prompt(u, batch, seq, *lru_w, h0_p, c0_p)
        xp = _matmul(gp, w_out, res=xp)
        p_h.append(hl.reshape(batch, d_rnn))
        p_c.append(ct)

        us = _matmul(_norm_cast(xs, a_norm[l]), w_in)
        c0 = jnp.pad(jnp.swapaxes(state_conv[l], 0, 1), ((0, 0), (0, SAMPLE_ROWS - bs), (0, 0)))
        gs, hs = _lru_sample(us, *lru_w, pad_rows(state_lru_h[l]), c0)
        xs = _matmul(gs, w_out, res=xs)
        s_h.append(hs[:bs])
        s_c.append(jnp.concatenate([state_conv[l][:, 1:], us[:bs, None, :d_rnn]], axis=1))

    w_kv_b = w_kv.astype(BF16)
    kv3 = _matmul(_norm_cast(xp, kv_norm), w_kv_b, tm=512, tn=cols, col_blocked=True)
    kv3s = _matmul(_norm_cast(xs, kv_norm), w_kv_b, tn=cols, col_blocked=True)[:, :bs]
    rows5 = lambda a, n: a.reshape(n, -1, 2, n_kv, hd)
    p_cmp_kv, p_slc_kv = rows5(kv3[0], batch), rows5(kv3[1], batch)
    p_win_kv = rows5(kv3[2], batch)[:, seq - WINDOW:]
    s_cmp_kv, s_slc_kv, s_win_rows = rows5(kv3s[0], bs), rows5(kv3s[1], bs), rows5(kv3s[2], bs)
    s_win_kv = jnp.concatenate([state_win_kv, s_win_rows], axis=1)[:, 1:]

    half = CMP_BLK // 2
    hid = w_cmp1.shape[3]
    w1ab = jnp.concatenate([w_cmp1[:, :half].reshape(2, half * hd, hid),
                            w_cmp1[:, half:].reshape(2, half * hd, hid)], axis=2).astype(BF16)
    posab = jnp.stack([cmp_pos[:, :half].reshape(2, half * hd),
                       cmp_pos[:, half:].reshape(2, half * hd)], axis=1)
    w2b = w_cmp2.astype(BF16)
    r16_p = kv3[0].reshape(batch, seq // CMP_STRIDE, CMP_STRIDE, n_sg, hd)
    r16_p = r16_p.transpose(0, 3, 1, 2, 4).reshape(batch, n_sg, seq // CMP_STRIDE, CMP_STRIDE * hd)
    comp_p = _compress(r16_p, posab, w1ab, w2b, k_norm[0], n_kv)
    comp_s = _compress(_page_gather(cache_cmp_kv, page_table, n_sg, hd), posab, w1ab, w2b,
                       k_norm[0], n_kv)

    n_c = seq // CMP_STRIDE
    n_cmp_p = (seq - CMP_BLK) // CMP_STRIDE + 1
    n_slc_p = seq // SLC_BLK
    tl = np.arange(QB)[:, None]
    kl = np.arange(WINDOW + QB)[None, :]
    dw = tl + WINDOW - kl
    wbias = _dist_bias(rel_table, dw, (dw >= 0) & (dw <= WINDOW), True)
    dc = np.arange(seq)[:, None] - (np.arange(n_c)[None, :] * CMP_STRIDE + CMP_BLK - 1)
    biasc = jnp.where(jnp.asarray(dc >= 0)[None], _dist_bias(rel_table, dc, dc >= 0, True), 0.0)
    mimp_p = jnp.asarray(_overlap_np(n_cmp_p, n_slc_p, n_c, LANES), BF16)
    expand = jnp.asarray((np.arange(LANES)[:, None] == np.arange(seq)[None, :] // SLC_BLK), BF16)

    total = past + 1
    n_cmp_s = (total - CMP_BLK) // CMP_STRIDE + 1
    n_slc_s = -(-total // SLC_BLK)
    nr_s = past // CMP_STRIDE
    assert n_cmp_s == nr_s - 1 and n_slc_s == past // SLC_BLK + 1
    dcs = past - (np.arange(nr_s) * CMP_STRIDE + CMP_BLK - 1)
    cbias_s = _dist_bias(rel_table, dcs, (dcs >= 0) & (np.arange(nr_s) < n_cmp_s), False)
    cbias_s = cbias_s.reshape(n_kv, r_, nr_s)
    mimp_s = jnp.asarray(_overlap_np(n_cmp_s, n_slc_s, nr_s, _round_up(n_slc_s, LANES)), BF16)
    dss = past - np.arange(n_slc_s * SLC_BLK)
    tb_s = _dist_bias(rel_table, dss, dss >= 0, False)
    tb_s = tb_s.reshape(n_kv, r_, n_slc_s, SLC_BLK).transpose(0, 2, 1, 3)
    dws = WINDOW - np.arange(WINDOW + LANES)
    wb_s = _dist_bias(rel_table, dws, dws >= 0, False).reshape(n_kv, r_, WINDOW + LANES)

    slc_new = kv3s[1].reshape(bs, n_sg, 1, hd)
    win_new = kv3s[2].reshape(bs, n_sg, 1, hd)
    for l in range(n_b):
        w_in = b_w_in[l][:, :2 * hq].astype(BF16)
        w_bg = jnp.pad(b_w_in[l][:, 2 * hq:], ((0, 0), (0, LANES - 3 * n_heads))).astype(BF16)
        w_out = b_w_out[l].astype(BF16)
        gbias = b_gate_bias[l]

        xn = _norm_cast(xp, b_norm[l])
        u = _matmul(xn, w_in)
        bgt = _matmul(xn, w_bg)[:, :3 * n_heads].reshape(m, n_kv, 3 * r_).transpose(1, 0, 2)
        og = _attn_prompt(u, bgt, gbias.reshape(n_kv, 1, 3 * r_), kv3, comp_p, wbias, biasc,
                          mimp_p, expand, b_q_norm[l], k_norm, batch, seq, n_kv, n_heads)
        xp = _matmul(og, w_out, res=xp)

        xn = _norm_cast(xs, b_norm[l])
        us = _matmul(xn, w_in)[:bs]
        q3 = us[:, :hq].reshape(bs, n_heads, hd)
        gate3 = us[:, hq:].reshape(bs, n_heads, hd)
        bg3 = _matmul(xn, w_bg)[:bs, :3 * n_heads].reshape(bs, n_heads, 3)
        o_cmp, sel = _attn_sample_cmp(q3, comp_s, cbias_s, mimp_s, b_q_norm[l], n_kv, n_slc_s)
        os_ = _attn_sample(sel[:, :, 0, :TOP_N].reshape(-1), page_table, q3, gate3, bg3,
                           gbias.reshape(n_heads, 3), o_cmp, cache_slc_kv, slc_new, state_win_kv,
                           win_new, tb_s, wb_s, b_q_norm[l], k_norm, n_kv, n_slc_s)
        xs = _matmul(pad_rows(os_.reshape(bs, hq).astype(BF16)), w_out, res=xs)

    return (xp.reshape(batch, seq, d_model), xs[:bs].reshape(bs, 1, d_model),
            p_cmp_kv, p_slc_kv, p_win_kv, jnp.stack(p_h), jnp.stack(p_c),
            s_cmp_kv, s_slc_kv, s_win_kv, jnp.stack(s_h), jnp.stack(s_c))
```

```python
import functools
import math

import numpy as np
import jax
import jax.numpy as jnp
from jax import lax
from jax.experimental import pallas as pl
from jax.experimental.pallas import tpu as pltpu

F32 = jnp.float32
BF16 = jnp.bfloat16

EPS = 1e-6
NEG = -1e30
BIG = 1e30
LOWEST = -3e38
LRU_C = 8.0
CONV_W = 4
CMP_BLK = 32
CMP_STRIDE = 16
SLC_BLK = 64
TOP_N = 16
WINDOW = 512
N_BUCKETS = 32
MAX_DIST = 128
PAGE_SIZE = 128
QB = 128
LANES = 128
SAMPLE_ROWS = 16
VMEM_LIMIT = 48 * 1024 * 1024


def _cparams(*sem):
    return pltpu.CompilerParams(dimension_semantics=sem, vmem_limit_bytes=VMEM_LIMIT)


def _dot(a, b):
    return jnp.dot(a, b, preferred_element_type=F32)


def _dot_nt(a, b):
    return lax.dot_general(a, b, (((1,), (1,)), ((), ())), preferred_element_type=F32)


def _rms(x, g):
    return x * lax.rsqrt(jnp.mean(x * x, axis=-1, keepdims=True) + EPS) * g


def _sigmoid(x):
    return jax.nn.sigmoid(x)


def _silu(x):
    return x * jax.nn.sigmoid(x)


def _norm_kernel(x_ref, g_ref, o_ref):
    o_ref[...] = _rms(x_ref[...], g_ref[...]).astype(o_ref.dtype)


def _norm_cast(x, g):
    m, d = x.shape
    tm = min(m, 512)
    return pl.pallas_call(
        _norm_kernel,
        grid=(m // tm,),
        in_specs=[pl.BlockSpec((tm, d), lambda i: (i, 0)),
                  pl.BlockSpec((1, d), lambda i: (0, 0))],
        out_specs=pl.BlockSpec((tm, d), lambda i: (i, 0)),
        out_shape=jax.ShapeDtypeStruct((m, d), BF16),
        compiler_params=_cparams("parallel"),
        name="norm_cast",
    )(x, g.reshape(1, d))


def _mm_kernel(a_ref, as_ref, w_ref, *rest, has_res):
    if has_res:
        r_ref, rs_ref, o_ref, os_ref, wb_scr = rest
    else:
        o_ref, os_ref, wb_scr = rest

    @pl.when(pl.program_id(1) == 0)
    def _():
        wb_scr[...] = w_ref[...].astype(BF16)
        acc_s = _dot(as_ref[...], wb_scr[...])
        os_ref[...] = rs_ref[...] + acc_s if has_res else acc_s

    acc = _dot(a_ref[...], wb_scr[...])
    o_ref[...] = r_ref[...] + acc if has_res else acc


def _matmul(a, a_s, w, layer=0, n=None, res=None, res_s=None, tm=1024, tn=512):
    m, k = a.shape
    ms = a_s.shape[0]
    n = w.shape[2] if n is None else n
    tm = min(m, tm)
    tn = min(n, tn)
    has_res = res is not None
    in_specs = [pl.BlockSpec((tm, k), lambda j, i: (i, 0)),
                pl.BlockSpec((ms, k), lambda j, i: (0, 0)),
                pl.BlockSpec((None, k, tn), lambda j, i: (layer, 0, j))]
    args = [a, a_s, w]
    if has_res:
        in_specs += [pl.BlockSpec((tm, tn), lambda j, i: (i, j)),
                     pl.BlockSpec((ms, tn), lambda j, i: (0, j))]
        args += [res, res_s]
    return pl.pallas_call(
        functools.partial(_mm_kernel, has_res=has_res),
        grid=(n // tn, m // tm),
        in_specs=in_specs,
        out_specs=[pl.BlockSpec((tm, tn), lambda j, i: (i, j)),
                   pl.BlockSpec((ms, tn), lambda j, i: (0, j))],
        out_shape=[jax.ShapeDtypeStruct((m, n), F32), jax.ShapeDtypeStruct((ms, n), F32)],
        scratch_shapes=[pltpu.VMEM((k, tn), BF16)],
        compiler_params=_cparams("arbitrary", "arbitrary"),
        name="matmul",
    )(*args)


def _lru_coeffs(xc, wrg_ref, wig_ref, brg, big, lam, pos0_row):
    rows, cb = xc.shape
    bs = wrg_ref.shape[-1]
    xcb = xc.astype(BF16)
    r_parts, i_parts = [], []
    for n in range(cb // bs):
        xs = xcb[:, n * bs:(n + 1) * bs]
        r_parts.append(_dot(xs, wrg_ref[n]))
        i_parts.append(_dot(xs, wig_ref[n]))
    r = _sigmoid(jnp.concatenate(r_parts, axis=1) + brg)
    i = _sigmoid(jnp.concatenate(i_parts, axis=1) + big)
    nl = -lam
    softplus = jnp.maximum(nl, 0.0) + jnp.log1p(jnp.exp(-jnp.abs(nl)))
    log_a = -LRU_C * r * softplus
    a = jnp.exp(log_a)
    mult = jnp.sqrt(-jnp.tanh(log_a) * (a * a + 1.0))
    if pos0_row is not None:
        row = lax.broadcasted_iota(jnp.int32, (rows, cb), 0)
        mult = jnp.where(row == pos0_row, 1.0, mult)
    return a, mult * i * xc


def _lru_prompt_kernel(xb_ref, gate_ref, cw_ref, cb_ref, wrg_ref, brg_ref, wig_ref, big_ref,
                       lam_ref, h0_ref, c0_ref, g_ref, hl_ref, ct_ref, xpad_scr, h_scr):
    tc = pl.program_id(2)
    tt, cb = xb_ref.shape

    @pl.when(tc == 0)
    def _():
        h_scr[0:1, :] = h0_ref[0]
        xpad_scr[5:8, :] = c0_ref[0]

    xb = xb_ref[...]
    xpad_scr[8:8 + tt, :] = xb
    w = cw_ref[...]
    xc = cb_ref[...]
    for k in range(CONV_W):
        xc = xc + w[k:k + 1] * xpad_scr[5 + k:5 + k + tt, :]
    xpad_scr[5:8, :] = xpad_scr[5 + tt:8 + tt, :]

    a, b_in = _lru_coeffs(xc, wrg_ref, wig_ref, brg_ref[...], big_ref[...], lam_ref[...],
                           jnp.where(tc == 0, 0, -1))

    row = lax.broadcasted_iota(jnp.int32, (tt, cb), 0)
    d = 1
    while d < tt:
        a_sh = pltpu.roll(a, d, axis=0)
        b_sh = pltpu.roll(b_in, d, axis=0)
        keep = row >= d
        b_in = jnp.where(keep, a * b_sh + b_in, b_in)
        a = jnp.where(keep, a * a_sh, a)
        d *= 2
    h = a * h_scr[0:1, :] + b_in
    h_scr[0:1, :] = h[tt - 1:tt]

    g_ref[...] = (h * _silu(gate_ref[...])).astype(g_ref.dtype)

    @pl.when(tc == pl.num_programs(2) - 1)
    def _():
        hl_ref[0] = h[tt - 1:tt]
        ct_ref[0] = xb[tt - (CONV_W - 1):tt]


def _lru_prompt(u, batch, seq, cw, cb_, wrg, brg, wig, big, lam, h0, c0):
    m, c2 = u.shape
    c = c2 // 2
    cb = min(c, 1024)
    tt = min(seq, 256)
    nt = seq // tt
    ncb = c // cb
    bs = wrg.shape[-1]
    nb = cb // bs
    vec = lambda: pl.BlockSpec((1, cb), lambda b, j, t: (0, j))
    return pl.pallas_call(
        _lru_prompt_kernel,
        grid=(batch, ncb, nt),
        in_specs=[
            pl.BlockSpec((tt, cb), lambda b, j, t: (b * nt + t, j)),
            pl.BlockSpec((tt, cb), lambda b, j, t: (b * nt + t, ncb + j)),
            pl.BlockSpec((CONV_W, cb), lambda b, j, t: (0, j)),
            vec(),
            pl.BlockSpec((nb, bs, bs), lambda b, j, t: (j, 0, 0)),
            vec(),
            pl.BlockSpec((nb, bs, bs), lambda b, j, t: (j, 0, 0)),
            vec(),
            vec(),
            pl.BlockSpec((1, 1, cb), lambda b, j, t: (b, 0, j)),
            pl.BlockSpec((1, CONV_W - 1, cb), lambda b, j, t: (b, 0, j)),
        ],
        out_specs=[
            pl.BlockSpec((tt, cb), lambda b, j, t: (b * nt + t, j)),
            pl.BlockSpec((1, 1, cb), lambda b, j, t: (b, 0, j)),
            pl.BlockSpec((1, CONV_W - 1, cb), lambda b, j, t: (b, 0, j)),
        ],
        out_shape=[
            jax.ShapeDtypeStruct((m, c), BF16),
            jax.ShapeDtypeStruct((batch, 1, c), F32),
            jax.ShapeDtypeStruct((batch, CONV_W - 1, c), F32),
        ],
        scratch_shapes=[pltpu.VMEM((tt + 8, cb), F32), pltpu.VMEM((8, cb), F32)],
        compiler_params=_cparams("parallel", "parallel", "arbitrary"),
        name="lru_prompt",
    )(u, u, cw, cb_.reshape(1, c), wrg, brg.reshape(1, c), wig, big.reshape(1, c),
      lam.reshape(1, c), h0.reshape(batch, 1, c), c0)


def _lru_sample_kernel(xb_ref, gate_ref, cw_ref, cb_ref, wrg_ref, brg_ref, wig_ref, big_ref,
                       lam_ref, h0_ref, c0_ref, g_ref, h_ref):
    w = cw_ref[...]
    xc = cb_ref[...]
    for k in range(CONV_W - 1):
        xc = xc + w[k:k + 1] * c0_ref[k]
    xc = xc + w[CONV_W - 1:CONV_W] * xb_ref[...]
    a, b_in = _lru_coeffs(xc, wrg_ref, wig_ref, brg_ref[...], big_ref[...], lam_ref[...], None)
    h = a * h0_ref[...] + b_in
    h_ref[...] = h
    g_ref[...] = (h * _silu(gate_ref[...])).astype(g_ref.dtype)


def _lru_sample(u, cw, cb_, wrg, brg, wig, big, lam, h0, c0):
    rows, c2 = u.shape
    c = c2 // 2
    cb = min(c, 1024)
    ncb = c // cb
    bs = wrg.shape[-1]
    nb = cb // bs
    vec = lambda: pl.BlockSpec((1, cb), lambda j: (0, j))
    return pl.pallas_call(
        _lru_sample_kernel,
        grid=(ncb,),
        in_specs=[
            pl.BlockSpec((rows, cb), lambda j: (0, j)),
            pl.BlockSpec((rows, cb), lambda j: (0, ncb + j)),
            pl.BlockSpec((CONV_W, cb), lambda j: (0, j)),
            vec(),
            pl.BlockSpec((nb, bs, bs), lambda j: (j, 0, 0)),
            vec(),
            pl.BlockSpec((nb, bs, bs), lambda j: (j, 0, 0)),
            vec(),
            vec(),
            pl.BlockSpec((rows, cb), lambda j: (0, j)),
            pl.BlockSpec((CONV_W - 1, rows, cb), lambda j: (0, 0, j)),
        ],
        out_specs=[pl.BlockSpec((rows, cb), lambda j: (0, j)),
                   pl.BlockSpec((rows, cb), lambda j: (0, j))],
        out_shape=[jax.ShapeDtypeStruct((rows, c), BF16),
                   jax.ShapeDtypeStruct((rows, c), F32)],
        compiler_params=_cparams("parallel"),
        name="lru_sample",
    )(u, u, cw, cb_.reshape(1, c), wrg, brg.reshape(1, c), wig, big.reshape(1, c),
      lam.reshape(1, c), h0, c0)


def _page_gather_kernel(pt_ref, c_ref, o_ref, slab_scr):
    del pt_ref
    n_sg = o_ref.shape[1]
    hd = c_ref.shape[2] // n_sg
    rows = PAGE_SIZE // CMP_STRIDE
    for sg in range(n_sg):
        slab_scr[sg] = c_ref[0, :, sg * hd:(sg + 1) * hd]
        for l in range(CMP_STRIDE):
            o_ref[0, sg, :, l * hd:(l + 1) * hd] = slab_scr[sg, pl.ds(l, rows, stride=CMP_STRIDE), :]


def _page_gather(cache, page_table, n_sg, hd):
    bs, n_pages = page_table.shape
    n_pool = cache.shape[0]
    rows = PAGE_SIZE // CMP_STRIDE
    cache2 = cache.reshape(n_pool, PAGE_SIZE, n_sg * hd)
    return pl.pallas_call(
        _page_gather_kernel,
        grid_spec=pltpu.PrefetchScalarGridSpec(
            num_scalar_prefetch=1,
            grid=(bs, n_pages),
            in_specs=[pl.BlockSpec((1, PAGE_SIZE, n_sg * hd),
                                   lambda b, j, pt: (pt[b * n_pages + j], 0, 0))],
            out_specs=pl.BlockSpec((1, n_sg, rows, CMP_STRIDE * hd), lambda b, j, pt: (b, 0, j, 0)),
            scratch_shapes=[pltpu.VMEM((n_sg, PAGE_SIZE, hd), F32)],
        ),
        out_shape=jax.ShapeDtypeStruct((bs, n_sg, n_pages * rows, CMP_STRIDE * hd), F32),
        compiler_params=_cparams("parallel", "arbitrary"),
        name="page_gather",
    )(page_table.reshape(-1), cache2)


def _compress_kernel(r_ref, pos_ref, w1_ref, w2_ref, kn_ref, o_ref, q_scr, *, n_kv):
    sg = pl.program_id(1)
    rows = r_ref[0, 0]
    nr = rows.shape[0]
    hid = w1_ref.shape[2] // 2
    xa = (rows + pos_ref[0, 0:1]).astype(BF16)
    xb = (rows + pos_ref[0, 1:2]).astype(BF16)
    w1 = w1_ref[0]
    p = _dot(xa, w1[:, :hid])
    q_scr[0:nr, :] = _dot(xb, w1[:, hid:])
    q_scr[nr:nr + 8, :] = jnp.zeros((8, hid), F32)
    pre = p + q_scr[1:nr + 1, :]
    comp = _dot(_silu(pre).astype(BF16), w2_ref[0])
    normed = _rms(comp, kn_ref[...])
    o_ref[0, 0] = jnp.where(sg < n_kv, normed, comp).astype(o_ref.dtype)


def _compress(r16, posab, w1ab, w2, k_norm0, n_kv):
    b, n_sg, nr, kk = r16.shape
    hid2 = w1ab.shape[2]
    hd = w2.shape[2]
    return pl.pallas_call(
        functools.partial(_compress_kernel, n_kv=n_kv),
        grid=(b, n_sg),
        in_specs=[
            pl.BlockSpec((1, 1, nr, kk), lambda i, s: (i, s, 0, 0)),
            pl.BlockSpec((1, 2, kk), lambda i, s: (s // n_kv, 0, 0)),
            pl.BlockSpec((1, kk, hid2), lambda i, s: (s // n_kv, 0, 0)),
            pl.BlockSpec((1, hid2 // 2, hd), lambda i, s: (s // n_kv, 0, 0)),
            pl.BlockSpec((1, hd), lambda i, s: (0, 0)),
        ],
        out_specs=pl.BlockSpec((1, 1, nr, hd), lambda i, s: (i, s, 0, 0)),
        out_shape=jax.ShapeDtypeStruct((b, n_sg, nr, hd), BF16),
        scratch_shapes=[pltpu.VMEM((nr + 8, hid2 // 2), F32)],
        compiler_params=_cparams("parallel", "arbitrary"),
        name="compress",
    )(r16, posab, w1ab, w2, k_norm0.reshape(1, hd))


def _attn_prompt_kernel(uq_ref, ug_ref, bg_ref, gb_ref, ks_ref, vs_ref, kw_ref, vw_ref,
                        kc_ref, vc_ref, wb_ref, tc_ref, mit_ref, ex_ref, qn_ref, kn_ref, og_ref,
                        ks_scr, vs_scr, kw_scr, vw_scr, madd_scr, lg_scr, p_scr, m_scr,
                        *, n_heads_grp, n_slc):
    r_ = n_heads_grp
    c = pl.program_id(2)
    t0 = c * QB
    seq, hd = ks_ref.shape
    rq = r_ * QB
    wk = WINDOW + QB
    n_kt = seq // QB
    n_wt = wk // QB

    @pl.when(c == 0)
    def _():
        ks_scr[...] = _rms(ks_ref[...], kn_ref[1:2]).astype(BF16)
        vs_scr[:, 0:hd] = vs_ref[...].astype(BF16)
        vs_scr[:, hd:2 * hd] = jnp.ones((seq, hd), BF16)
        kw_scr[0:WINDOW, :] = jnp.zeros((WINDOW, hd), BF16)
        vw_scr[0:WINDOW, :] = jnp.zeros((WINDOW, 2 * hd), BF16)
        kw_scr[WINDOW:WINDOW + seq, :] = _rms(kw_ref[...], kn_ref[2:3]).astype(BF16)
        vw_scr[WINDOW:WINDOW + seq, 0:hd] = vw_ref[...].astype(BF16)
        vw_scr[WINDOW:WINDOW + seq, hd:2 * hd] = jnp.ones((seq, hd), BF16)
        p_scr[...] = jnp.zeros(p_scr.shape, BF16)

    uq = uq_ref[...]
    qg = qn_ref[...] * hd ** -0.5
    qb = jnp.concatenate(
        [uq[:, r * hd:(r + 1) * hd]
         * lax.rsqrt(jnp.mean(uq[:, r * hd:(r + 1) * hd] ** 2, axis=-1, keepdims=True) + EPS) * qg
         for r in range(r_)], axis=0).astype(BF16)

    n_c = kc_ref.shape[2]
    bias_c = pltpu.roll(tc_ref[...], (c * (QB // CMP_STRIDE) + QB // CMP_STRIDE) % n_c, axis=2)
    lc = _dot_nt(qb, kc_ref[0, 0]).reshape(r_, QB, n_c) + bias_c
    tq = t0 + lax.broadcasted_iota(jnp.int32, (QB, n_c), 0)
    nn = lax.broadcasted_iota(jnp.int32, (QB, n_c), 1)
    c_ok = (tq - (nn * CMP_STRIDE + (CMP_BLK - 1)) >= 0)[None]
    lc = jnp.where(c_ok, lc, NEG)
    e = jnp.exp(lc - jnp.max(lc, axis=-1, keepdims=True))
    pc = jnp.where(c_ok, e / jnp.sum(e, axis=-1, keepdims=True), 0.0)
    pcb = pc.reshape(rq, n_c).astype(BF16)
    o_cmp = _dot(pcb, vc_ref[0, 0])
    imp_all = _dot_nt(mit_ref[...], pcb)
    imp = imp_all[:, 0:QB]
    for r in range(1, r_):
        imp = imp + imp_all[:, r * QB:(r + 1) * QB]

    jj = lax.broadcasted_iota(jnp.int32, (n_slc, QB), 0)
    cur = (t0 + lax.broadcasted_iota(jnp.int32, (n_slc, QB), 1)) // SLC_BLK
    forced = (jj == 0) | (jj == cur) | (jj == cur - 1)
    allowed = jj <= cur
    score = jnp.where(forced, BIG, jnp.where(allowed, imp, NEG))
    rank = jnp.zeros((n_slc, QB), F32)
    for k in range(n_slc):
        sk = score[k:k + 1, :]
        rank = rank + jnp.where(sk > score, 1.0, jnp.where((sk == score) & (k < jj), 1.0, 0.0))
    sel = jnp.where((rank < min(TOP_N, n_slc)) & allowed, 1.0, 0.0).astype(BF16)
    selx = lax.dot_general(sel, ex_ref[...], (((0,), (0,)), ((), ())),
                           preferred_element_type=F32)
    for kt in range(n_kt):
        madd_scr[kt] = (selx[:, kt * QB:(kt + 1) * QB] - 1.0) * BIG

    m_scr[...] = jnp.full((rq, QB), NEG, F32)

    def logits_tile(kt, bias):
        k0 = pl.multiple_of(kt * QB, QB)
        s3 = _dot_nt(qb, ks_scr[pl.ds(k0, QB), :]).reshape(r_, QB, QB) + madd_scr[kt][None]
        if bias is not None:
            s3 = s3 + bias
        s = s3.reshape(rq, QB)
        lg_scr[kt] = s
        m_scr[...] = jnp.maximum(m_scr[...], s)

    def far_body(kt, carry):
        logits_tile(kt, None)
        return carry

    lax.fori_loop(0, jnp.maximum(c - 1, 0), far_body, 0)

    @pl.when(c >= 1)
    def _():
        logits_tile(c - 1, wb_ref[:, :, WINDOW - QB:WINDOW])

    logits_tile(c, wb_ref[:, :, WINDOW:WINDOW + QB])

    m_scr[...] = jnp.broadcast_to(jnp.max(m_scr[...], axis=-1, keepdims=True), (rq, QB))

    def prob_body(kt, carry):
        k0 = pl.multiple_of(kt * QB, QB)
        p_scr[:, pl.ds(k0, QB)] = jnp.exp(lg_scr[kt] - m_scr[...]).astype(BF16)
        return carry

    lax.fori_loop(0, c + 1, prob_body, 0)
    acc = _dot(p_scr[...], vs_scr[...])
    o_slc = acc[:, 0:hd] / acc[:, hd:2 * hd]

    w0 = pl.multiple_of(t0, QB)
    sw = _dot_nt(qb, kw_scr[pl.ds(w0, wk), :])
    tiles = []
    for j in range(n_wt):
        s = sw[:, j * QB:(j + 1) * QB]
        if j == 0 or j >= n_wt - 2:
            s = s + wb_ref[:, :, j * QB:(j + 1) * QB].reshape(rq, QB)
        if j < n_wt - 1:
            s = s + jnp.where(c + j >= n_wt - 1, 0.0, NEG)
        tiles.append(s)
    mw = tiles[0]
    for s in tiles[1:]:
        mw = jnp.maximum(mw, s)
    mw = jnp.max(mw, axis=-1, keepdims=True)
    pw = jnp.concatenate([jnp.exp(s - mw).astype(BF16) for s in tiles], axis=1)
    accw = _dot(pw, vw_scr[pl.ds(w0, wk), :])
    o_win = accw[:, 0:hd] / accw[:, hd:2 * hd]

    bg = _sigmoid(bg_ref[...] + gb_ref[0])
    ug = ug_ref[...]
    for r in range(r_):
        rows = slice(r * QB, (r + 1) * QB)
        o = (bg[:, 3 * r:3 * r + 1] * o_cmp[rows] + bg[:, 3 * r + 1:3 * r + 2] * o_slc[rows]
             + bg[:, 3 * r + 2:3 * r + 3] * o_win[rows])
        og_ref[:, r * hd:(r + 1) * hd] = (o * _silu(ug[:, r * hd:(r + 1) * hd])).astype(og_ref.dtype)


def _attn_prompt(u, bgs, gbias, kv, comp, wbias, tcfix, mimp_t, expand, q_norm, k_norm,
                 batch, seq, n_kv, n_heads):
    m = u.shape[0]
    hd = k_norm.shape[1]
    r_ = n_heads // n_kv
    hq = n_heads * hd
    nq = seq // QB
    n_c = comp.shape[2]
    n_slc = seq // SLC_BLK
    rq = r_ * QB
    wk = WINDOW + QB
    n_sg = 2 * n_kv
    kv_spec = lambda col0: pl.BlockSpec((seq, hd), lambda b, g, c: (b, col0 + g))
    return pl.pallas_call(
        functools.partial(_attn_prompt_kernel, n_heads_grp=r_, n_slc=n_slc),
        grid=(batch, n_kv, nq),
        in_specs=[
            pl.BlockSpec((QB, r_ * hd), lambda b, g, c: (b * nq + c, g)),
            pl.BlockSpec((QB, r_ * hd), lambda b, g, c: (b * nq + c, n_kv + g)),
            pl.BlockSpec((QB, LANES), lambda b, g, c: (b * nq + c, g)),
            pl.BlockSpec((1, 1, LANES), lambda b, g, c: (g, 0, 0)),
            kv_spec(n_sg), kv_spec(n_sg + n_kv), kv_spec(2 * n_sg), kv_spec(2 * n_sg + n_kv),
            pl.BlockSpec((1, 1, n_c, hd), lambda b, g, c: (b, g, 0, 0)),
            pl.BlockSpec((1, 1, n_c, hd), lambda b, g, c: (b, n_kv + g, 0, 0)),
            pl.BlockSpec((r_, QB, wk), lambda b, g, c: (g, 0, 0)),
            pl.BlockSpec((r_, QB, n_c), lambda b, g, c: (g, 0, 0)),
            pl.BlockSpec((n_slc, n_c), lambda b, g, c: (0, 0)),
            pl.BlockSpec((n_slc, seq), lambda b, g, c: (0, 0)),
            pl.BlockSpec((1, hd), lambda b, g, c: (0, 0)),
            pl.BlockSpec((3, hd), lambda b, g, c: (0, 0)),
        ],
        out_specs=pl.BlockSpec((QB, r_ * hd), lambda b, g, c: (b * nq + c, g)),
        out_shape=jax.ShapeDtypeStruct((m, hq), BF16),
        scratch_shapes=[
            pltpu.VMEM((seq, hd), BF16), pltpu.VMEM((seq, 2 * hd), BF16),
            pltpu.VMEM((seq + WINDOW, hd), BF16), pltpu.VMEM((seq + WINDOW, 2 * hd), BF16),
            pltpu.VMEM((seq // QB, QB, QB), F32),
            pltpu.VMEM((seq // QB, rq, QB), F32),
            pltpu.VMEM((rq, seq), BF16),
            pltpu.VMEM((rq, QB), F32),
        ],
        compiler_params=_cparams("parallel", "parallel", "arbitrary"),
        name="attn_prompt",
    )(u, u, bgs, gbias, kv, kv, kv, kv, comp, comp, wbias, tcfix, mimp_t, expand,
      q_norm.reshape(1, hd), k_norm)


def _attn_sample_cmp_kernel(q_ref, kc_ref, vc_ref, cb_ref, mi_ref, qn_ref, oc_ref, sel_ref,
                            *, n_slc):
    r_, hd = q_ref.shape[1], q_ref.shape[2]
    qb = (_rms(q_ref[0], qn_ref[...]) * hd ** -0.5).astype(BF16)
    lc = _dot_nt(qb, kc_ref[0, 0]) + cb_ref[0]
    e = jnp.exp(lc - jnp.max(lc, axis=-1, keepdims=True))
    pcb = (e / jnp.sum(e, axis=-1, keepdims=True)).astype(BF16)
    oc_ref[0] = _dot(pcb, vc_ref[0, 0])
    nsp = mi_ref.shape[1]
    imp = jnp.sum(_dot(pcb, mi_ref[...]), axis=0, keepdims=True)

    jl = lax.broadcasted_iota(jnp.int32, (1, nsp), 1)
    forced = (jl == 0) | (jl == n_slc - 1) | (jl == n_slc - 2)
    score = jnp.where(jl < n_slc, jnp.where(forced, BIG, imp), LOWEST)
    s_rows = jnp.broadcast_to(score, (nsp, nsp))
    kk = lax.broadcasted_iota(jnp.int32, (nsp, nsp), 0)
    jj = lax.broadcasted_iota(jnp.int32, (nsp, nsp), 1)
    s_col = jnp.sum(jnp.where(kk == jj, s_rows, 0.0), axis=1, keepdims=True)
    ahead = jnp.where(s_col > s_rows, 1.0, jnp.where((s_col == s_rows) & (kk < jj), 1.0, 0.0))
    rank = jnp.sum(ahead, axis=0, keepdims=True)
    lane = lax.broadcasted_iota(jnp.int32, (1, LANES), 1)
    out = jnp.zeros((1, LANES), F32)
    jf = jl.astype(F32)
    for k in range(min(TOP_N, n_slc)):
        idx_k = jnp.sum(jnp.where(rank == k, jf, 0.0), axis=1, keepdims=True)
        out = out + jnp.where(lane == k, idx_k, 0.0)
    sel_ref[0, 0] = jnp.broadcast_to(out, (8, LANES)).astype(jnp.int32)


def _attn_sample_cmp(q3, comp, cbias, mimp, q_norm, n_kv, n_slc):
    bs, n_heads, hd = q3.shape
    r_ = n_heads // n_kv
    nr = comp.shape[2]
    nsp = mimp.shape[1]
    return pl.pallas_call(
        functools.partial(_attn_sample_cmp_kernel, n_slc=n_slc),
        grid=(bs, n_kv),
        in_specs=[
            pl.BlockSpec((1, r_, hd), lambda b, g: (b, g, 0)),
            pl.BlockSpec((1, 1, nr, hd), lambda b, g: (b, g, 0, 0)),
            pl.BlockSpec((1, 1, nr, hd), lambda b, g: (b, n_kv + g, 0, 0)),
            pl.BlockSpec((1, r_, nr), lambda b, g: (g, 0, 0)),
            pl.BlockSpec((nr, nsp), lambda b, g: (0, 0)),
            pl.BlockSpec((1, hd), lambda b, g: (0, 0)),
        ],
        out_specs=[pl.BlockSpec((1, r_, hd), lambda b, g: (b, g, 0)),
                   pl.BlockSpec((1, 1, 8, LANES), lambda b, g: (b, g, 0, 0))],
        out_shape=[jax.ShapeDtypeStruct((bs, n_heads, hd), F32),
                   jax.ShapeDtypeStruct((bs, n_kv, 8, LANES), jnp.int32)],
        compiler_params=_cparams("parallel", "parallel"),
        name="attn_sample_cmp",
    )(q3, comp, comp, cbias, mimp, q_norm.reshape(1, hd))


def _attn_sample_kernel(sel_ref, pt_ref, q_ref, gate_ref, bg_ref, gb_ref, oc_ref, ksn_ref, vsn_ref,
                        kw_ref, vw_ref, kwn_ref, vwn_ref, tb_ref, wb_ref, qn_ref, kn_ref, *rest,
                        n_sel, n_slc, n_kv):
    del pt_ref
    k_refs, v_refs, o_ref = rest[:n_sel], rest[n_sel:2 * n_sel], rest[2 * n_sel]
    b, g = pl.program_id(0), pl.program_id(1)
    r_, hd = q_ref.shape[1], q_ref.shape[2]
    qb = (_rms(q_ref[0], qn_ref[...]) * hd ** -0.5).astype(BF16)

    k_new = ksn_ref[0, 0]
    v_new = vsn_ref[0, 0]
    logits, values = [], []
    for k in range(n_sel):
        idx = sel_ref[(b * n_kv + g) * TOP_N + k]
        is_new = idx == n_slc - 1
        kt = jnp.where(is_new, jnp.broadcast_to(k_new, (SLC_BLK, hd)), k_refs[k][0])
        vt = jnp.where(is_new, jnp.broadcast_to(v_new, (SLC_BLK, hd)), v_refs[k][0])
        logits.append(_dot_nt(qb, _rms(kt, kn_ref[1:2]).astype(BF16)) + tb_ref[0, idx])
        values.append(vt.astype(BF16))
    m = logits[0].max(axis=-1, keepdims=True)
    for s in logits[1:]:
        m = jnp.maximum(m, s.max(axis=-1, keepdims=True))
    es = [jnp.exp(s - m) for s in logits]
    den = es[0].sum(axis=-1, keepdims=True)
    for e in es[1:]:
        den = den + e.sum(axis=-1, keepdims=True)
    o_slc = jnp.zeros((r_, hd), F32)
    for e, v in zip(es, values):
        o_slc = o_slc + _dot((e / den).astype(BF16), v)

    n_w = kw_ref.shape[1]
    lw = _dot_nt(qb, _rms(kw_ref[0], kn_ref[2:3]).astype(BF16)) + wb_ref[0, :, 0:n_w]
    kwn = _rms(kwn_ref[0, 0], kn_ref[2:3]).astype(BF16).astype(F32)
    l_new = jnp.sum(qb.astype(F32) * kwn, axis=-1, keepdims=True) + wb_ref[0, :, n_w:n_w + 1]
    mw = jnp.maximum(lw.max(axis=-1, keepdims=True), l_new)
    ew, e_new = jnp.exp(lw - mw), jnp.exp(l_new - mw)
    dw = ew.sum(axis=-1, keepdims=True) + e_new
    o_win = (_dot((ew / dw).astype(BF16), vw_ref[0].astype(BF16))
             + (e_new / dw).astype(BF16).astype(F32) * vwn_ref[0, 0].astype(BF16).astype(F32))

    bg = _sigmoid(bg_ref[0] + gb_ref[...])
    o = bg[:, 0:1] * oc_ref[0] + bg[:, 1:2] * o_slc + bg[:, 2:3] * o_win
    o_ref[0] = o * _silu(gate_ref[0])


def _attn_sample(sel, page_table, q3, gate3, bg3, gbias2, o_cmp, cache_slc, slc_new, win_state,
                 win_new, tb, wb, q_norm, k_norm, n_kv, n_slc):
    bs, n_heads, hd = q3.shape
    r_ = n_heads // n_kv
    n_pages = page_table.shape[1]
    n_pool = cache_slc.shape[0]
    n_sel = min(TOP_N, n_slc)
    cols = 2 * n_kv * hd
    halves = PAGE_SIZE // SLC_BLK
    cache2 = cache_slc.reshape(n_pool * halves, SLC_BLK, cols)
    n_w = win_state.shape[1]
    win2 = win_state.reshape(bs, n_w, cols)

    def blk_map(k, off):
        def index_map(b, g, sel_ref, pt_ref):
            idx = jnp.minimum(sel_ref[(b * n_kv + g) * TOP_N + k], n_slc - 2)
            page = pt_ref[b * n_pages + idx // halves]
            return (page * halves + idx % halves, 0, off + g)
        return index_map

    head_spec = lambda: pl.BlockSpec((1, r_, hd), lambda b, g, s, p: (b, g, 0))
    in_specs = [
        head_spec(), head_spec(),
        pl.BlockSpec((1, r_, 3), lambda b, g, s, p: (b, g, 0)),
        pl.BlockSpec((r_, 3), lambda b, g, s, p: (g, 0)),
        head_spec(),
        pl.BlockSpec((1, 1, 1, hd), lambda b, g, s, p: (b, g, 0, 0)),
        pl.BlockSpec((1, 1, 1, hd), lambda b, g, s, p: (b, n_kv + g, 0, 0)),
        pl.BlockSpec((1, n_w, hd), lambda b, g, s, p: (b, 0, g)),
        pl.BlockSpec((1, n_w, hd), lambda b, g, s, p: (b, 0, n_kv + g)),
        pl.BlockSpec((1, 1, 1, hd), lambda b, g, s, p: (b, g, 0, 0)),
        pl.BlockSpec((1, 1, 1, hd), lambda b, g, s, p: (b, n_kv + g, 0, 0)),
        pl.BlockSpec((1, n_slc, r_, SLC_BLK), lambda b, g, s, p: (g, 0, 0, 0)),
        pl.BlockSpec((1, r_, wb.shape[2]), lambda b, g, s, p: (g, 0, 0)),
        pl.BlockSpec((1, hd), lambda b, g, s, p: (0, 0)),
        pl.BlockSpec((3, hd), lambda b, g, s, p: (0, 0)),
    ]
    in_specs += [pl.BlockSpec((1, SLC_BLK, hd), blk_map(k, 0)) for k in range(n_sel)]
    in_specs += [pl.BlockSpec((1, SLC_BLK, hd), blk_map(k, n_kv)) for k in range(n_sel)]
    return pl.pallas_call(
        functools.partial(_attn_sample_kernel, n_sel=n_sel, n_slc=n_slc, n_kv=n_kv),
        grid_spec=pltpu.PrefetchScalarGridSpec(
            num_scalar_prefetch=2,
            grid=(bs, n_kv),
            in_specs=in_specs,
            out_specs=pl.BlockSpec((1, r_, hd), lambda b, g, s, p: (b, g, 0)),
        ),
        out_shape=jax.ShapeDtypeStruct((bs, n_heads, hd), F32),
        compiler_params=_cparams("arbitrary", "arbitrary"),
        name="attn_sample",
    )(sel, page_table.reshape(-1), q3, gate3, bg3, gbias2, o_cmp, slc_new, slc_new, win2, win2,
      win_new, win_new, tb, wb, q_norm.reshape(1, hd), k_norm, *([cache2] * (2 * n_sel)))


def _bucket_np(d):
    d = np.maximum(d, 0)
    n_exact = N_BUCKETS // 2
    df = np.maximum(d, 1).astype(np.float64)
    large = n_exact + (np.log(df / n_exact) / math.log(MAX_DIST / n_exact)
                       * (N_BUCKETS - n_exact)).astype(np.int64)
    return np.where(d < n_exact, d, np.minimum(large, N_BUCKETS - 1))


def _dist_bias(rel_table, d, valid, shift):
    fd = rel_table.astype(F32)[_bucket_np(np.arange(MAX_DIST + 1))]
    if shift:
        fd = fd - fd[MAX_DIST:MAX_DIST + 1]
    vals = jnp.moveaxis(fd[np.clip(d, 0, MAX_DIST)], -1, 0)
    return jnp.where(jnp.asarray(valid)[None], vals, NEG)


def _overlap_np(n_cmp, n_slc, rows, cols):
    cs = np.arange(n_cmp)[:, None] * CMP_STRIDE
    ss = np.arange(n_slc)[None, :] * SLC_BLK
    ov = np.minimum(cs + CMP_BLK, ss + SLC_BLK) - np.maximum(cs, ss)
    out = np.zeros((rows, cols), np.float32)
    out[:n_cmp, :n_slc] = np.maximum(ov, 0).astype(np.float32) / CMP_BLK
    return out


def _round_up(x, m):
    return (x + m - 1) // m * m


def kernel(x_prompt, x_sample, cache_cmp_kv, cache_slc_kv, state_win_kv, state_lru_h, state_conv,
           page_table, a_norm, a_w_in, a_conv_w, a_conv_b, a_w_rg, a_b_rg, a_w_ig, a_b_ig, a_lambda,
           a_w_out, kv_norm, w_kv, k_norm, cmp_pos, w_cmp1, w_cmp2, rel_table, b_norm, b_w_in,
           b_gate_bias, b_q_norm, b_w_out):
    batch, seq, d_model = x_prompt.shape
    bs = x_sample.shape[0]
    n_a, n_b = a_norm.shape[0], b_norm.shape[0]
    d_rnn = a_w_in.shape[2] // 2
    n_kv, hd = cache_cmp_kv.shape[3], cache_cmp_kv.shape[4]
    n_heads = rel_table.shape[1]
    r_ = n_heads // n_kv
    hq = n_heads * hd
    n_sg = 2 * n_kv
    cols = n_sg * hd
    n_pages = page_table.shape[1]
    past = n_pages * PAGE_SIZE
    m = batch * seq
    assert x_sample.shape[1] == 1 and seq % QB == 0 and seq >= WINDOW and hd == LANES
    assert seq // CMP_STRIDE == LANES and past >= WINDOW and state_win_kv.shape[1] == WINDOW

    xp = x_prompt.reshape(m, d_model)
    xs = jnp.pad(x_sample.reshape(bs, d_model), ((0, SAMPLE_ROWS - bs), (0, 0)))
    pad_rows = lambda a: jnp.pad(a, ((0, SAMPLE_ROWS - bs), (0, 0)))

    p_h, p_c, s_h, s_c = [], [], [], []
    h0_p = jnp.zeros((batch, d_rnn), F32)
    c0_p = jnp.zeros((batch, CONV_W - 1, d_rnn), F32)
    for l in range(n_a):
        wrg, wig = a_w_rg[l].astype(BF16), a_w_ig[l].astype(BF16)
        lru_w = (a_conv_w[l], a_conv_b[l], wrg, a_b_rg[l], wig, a_b_ig[l], a_lambda[l])
        u, us = _matmul(_norm_cast(xp, a_norm[l]), _norm_cast(xs, a_norm[l]), a_w_in, layer=l)
        gp, hl, ct = _lru_prompt(u, batch, seq, *lru_w, h0_p, c0_p)
        c0 = jnp.pad(jnp.swapaxes(state_conv[l], 0, 1), ((0, 0), (0, SAMPLE_ROWS - bs), (0, 0)))
        gs, hs = _lru_sample(us, *lru_w, pad_rows(state_lru_h[l]), c0)
        xp, xs = _matmul(gp, gs, a_w_out, layer=l, res=xp, res_s=xs)
        p_h.append(hl.reshape(batch, d_rnn))
        p_c.append(ct)
        s_h.append(hs[:bs])
        s_c.append(jnp.concatenate([state_conv[l][:, 1:], us[:bs, None, :d_rnn]], axis=1))

    kv, kvs = _matmul(_norm_cast(xp, kv_norm), _norm_cast(xs, kv_norm), w_kv[None])
    kvs = kvs[:bs]
    rows5 = lambda a, n: a.reshape(n, -1, 2, n_kv, hd)
    p_cmp_kv, p_slc_kv = rows5(kv[:, :cols], batch), rows5(kv[:, cols:2 * cols], batch)
    p_win_kv = rows5(kv[:, 2 * cols:], batch)[:, seq - WINDOW:]
    s_cmp_kv, s_slc_kv = rows5(kvs[:, :cols], bs), rows5(kvs[:, cols:2 * cols], bs)
    s_win_rows = rows5(kvs[:, 2 * cols:], bs)
    s_win_kv = jnp.concatenate([state_win_kv, s_win_rows], axis=1)[:, 1:]

    half = CMP_BLK // 2
    hid = w_cmp1.shape[3]
    w1ab = jnp.concatenate([w_cmp1[:, :half].reshape(2, half * hd, hid),
                            w_cmp1[:, half:].reshape(2, half * hd, hid)], axis=2).astype(BF16)
    posab = jnp.stack([cmp_pos[:, :half].reshape(2, half * hd),
                       cmp_pos[:, half:].reshape(2, half * hd)], axis=1)
    w2b = w_cmp2.astype(BF16)
    r16_p = p_cmp_kv.reshape(batch, seq // CMP_STRIDE, CMP_STRIDE, n_sg, hd)
    r16_p = r16_p.transpose(0, 3, 1, 2, 4).reshape(batch, n_sg, seq // CMP_STRIDE, CMP_STRIDE * hd)
    comp_p = _compress(r16_p, posab, w1ab, w2b, k_norm[0], n_kv)
    comp_s = _compress(_page_gather(cache_cmp_kv, page_table, n_sg, hd), posab, w1ab, w2b,
                       k_norm[0], n_kv)

    n_c = seq // CMP_STRIDE
    n_cmp_p = (seq - CMP_BLK) // CMP_STRIDE + 1
    n_slc_p = seq // SLC_BLK
    wk = WINDOW + QB
    dv = WINDOW - np.arange(wk + 1)
    vrow = _dist_bias(rel_table, dv, dv >= 0, True)
    wbias = jnp.tile(vrow, (1, QB))[:, :QB * wk].reshape(n_heads, QB, wk)
    dc = (np.arange(QB)[:, None] - (np.arange(n_c)[None, :] - (n_c - QB // CMP_STRIDE)) * CMP_STRIDE
          - (CMP_BLK - 1))
    tcfix = jnp.where(jnp.asarray(dc >= 0)[None], _dist_bias(rel_table, dc, dc >= 0, True), 0.0)
    mimp_t = jnp.asarray(_overlap_np(n_cmp_p, n_slc_p, n_c, n_slc_p).T, BF16)
    expand = jnp.asarray((np.arange(n_slc_p)[:, None] == np.arange(seq)[None, :] // SLC_BLK), BF16)

    total = past + 1
    n_cmp_s = (total - CMP_BLK) // CMP_STRIDE + 1
    n_slc_s = -(-total // SLC_BLK)
    nr_s = past // CMP_STRIDE
    assert n_cmp_s == nr_s - 1 and n_slc_s == past // SLC_BLK + 1
    dcs = past - (np.arange(nr_s) * CMP_STRIDE + CMP_BLK - 1)
    cbias_s = _dist_bias(rel_table, dcs, (dcs >= 0) & (np.arange(nr_s) < n_cmp_s), False)
    cbias_s = cbias_s.reshape(n_kv, r_, nr_s)
    mimp_s = jnp.asarray(_overlap_np(n_cmp_s, n_slc_s, nr_s, _round_up(n_slc_s, LANES)), BF16)
    dss = past - np.arange(n_slc_s * SLC_BLK)
    tb_s = _dist_bias(rel_table, dss, dss >= 0, False)
    tb_s = tb_s.reshape(n_kv, r_, n_slc_s, SLC_BLK).transpose(0, 2, 1, 3)
    dws = WINDOW - np.arange(WINDOW + LANES)
    wb_s = _dist_bias(rel_table, dws, dws >= 0, False).reshape(n_kv, r_, WINDOW + LANES)

    slc_new = kvs[:, cols:2 * cols].reshape(bs, n_sg, 1, hd)
    win_new = kvs[:, 2 * cols:].reshape(bs, n_sg, 1, hd)
    slab_pad = ((0, 0), (0, 0), (0, LANES - 3 * r_))
    for l in range(n_b):
        w_bg = jnp.pad(b_w_in[l][:, 2 * hq:].reshape(d_model, n_kv, 3 * r_), slab_pad)
        w_bg = w_bg.reshape(d_model, n_kv * LANES)
        gbias = b_gate_bias[l]
        gb_slab = jnp.pad(gbias.reshape(n_kv, 1, 3 * r_), slab_pad)

        xn, xns = _norm_cast(xp, b_norm[l]), _norm_cast(xs, b_norm[l])
        u, us = _matmul(xn, xns, b_w_in, layer=l, n=2 * hq)
        bgs, bgs_s = _matmul(xn, xns, w_bg[None])
        og = _attn_prompt(u, bgs, gb_slab, kv, comp_p, wbias, tcfix, mimp_t, expand,
                          b_q_norm[l], k_norm, batch, seq, n_kv, n_heads)

        us = us[:bs]
        q3 = us[:, :hq].reshape(bs, n_heads, hd)
        gate3 = us[:, hq:].reshape(bs, n_heads, hd)
        bg3 = bgs_s[:bs].reshape(bs, n_kv, LANES)[:, :, :3 * r_].reshape(bs, n_heads, 3)
        o_cmp, sel = _attn_sample_cmp(q3, comp_s, cbias_s, mimp_s, b_q_norm[l], n_kv, n_slc_s)
        os_ = _attn_sample(sel[:, :, 0, :TOP_N].reshape(-1), page_table, q3, gate3, bg3,
                           gbias.reshape(n_heads, 3), o_cmp, cache_slc_kv, slc_new, state_win_kv,
                           win_new, tb_s, wb_s, b_q_norm[l], k_norm, n_kv, n_slc_s)
        xp, xs = _matmul(og, pad_rows(os_.reshape(bs, hq).astype(BF16)), b_w_out, layer=l,
                         res=xp, res_s=xs)

    return (xp.reshape(batch, seq, d_model), xs[:bs].reshape(bs, 1, d_model),
            p_cmp_kv, p_slc_kv, p_win_kv, jnp.stack(p_h), jnp.stack(p_c),
            s_cmp_kv, s_slc_kv, s_win_kv, jnp.stack(s_h), jnp.stack(s_c))
```

```python
import functools
import math

import numpy as np
import jax
import jax.numpy as jnp
from jax import lax
from jax.experimental import pallas as pl
from jax.experimental.pallas import tpu as pltpu

F32 = jnp.float32
BF16 = jnp.bfloat16

EPS = 1e-6
NEG = -1e30
BIG = 1e30
LOWEST = -3e38
LRU_C = 8.0
CONV_W = 4
CMP_BLK = 32
CMP_STRIDE = 16
SLC_BLK = 64
TOP_N = 16
WINDOW = 512
N_BUCKETS = 32
MAX_DIST = 128
PAGE_SIZE = 128
QB = 128
LANES = 128
SUBLANES = 8
SEG_PAD = 8
SAMPLE_ROWS = 16
VMEM_LIMIT = 48 * 1024 * 1024


def _cparams(*sem):
    return pltpu.CompilerParams(dimension_semantics=sem, vmem_limit_bytes=VMEM_LIMIT)


def _dot(a, b):
    return jnp.dot(a, b, preferred_element_type=F32)


def _dot_nt(a, b):
    return lax.dot_general(a, b, (((1,), (1,)), ((), ())), preferred_element_type=F32)


def _rms(x, g):
    return x * lax.rsqrt(jnp.mean(x * x, axis=-1, keepdims=True) + EPS) * g


def _sigmoid(x):
    return jax.nn.sigmoid(x)


def _silu(x):
    return x * jax.nn.sigmoid(x)


def _norm_kernel(x_ref, g_ref, o_ref):
    o_ref[...] = _rms(x_ref[...], g_ref[...]).astype(o_ref.dtype)


def _norm_cast(x, g):
    m, d = x.shape
    tm = min(m, 512)
    return pl.pallas_call(
        _norm_kernel,
        grid=(m // tm,),
        in_specs=[pl.BlockSpec((tm, d), lambda i: (i, 0)),
                  pl.BlockSpec((1, d), lambda i: (0, 0))],
        out_specs=pl.BlockSpec((tm, d), lambda i: (i, 0)),
        out_shape=jax.ShapeDtypeStruct((m, d), BF16),
        compiler_params=_cparams("parallel"),
        name="norm_cast",
    )(x, g.reshape(1, d))


def _mm_kernel(a_ref, as_ref, w_ref, *rest, has_res, n_rows5):
    if has_res:
        r_ref, rs_ref, o_ref, os_ref = rest[:4]
    else:
        o_ref, os_ref = rest[:2]
    wb_scr = rest[-1]
    rows5_refs = rest[-1 - n_rows5:-1]

    @pl.when(pl.program_id(1) == 0)
    def _():
        wb_scr[...] = w_ref[...].astype(BF16)
        acc_s = _dot(as_ref[...], wb_scr[...])
        os_ref[...] = rs_ref[...] + acc_s if has_res else acc_s

    acc = _dot(a_ref[...], wb_scr[...])
    o_ref[...] = r_ref[...] + acc if has_res else acc

    for br, r5_ref in enumerate(rows5_refs):
        @pl.when(pl.program_id(0) // 2 == br)
        def _(r5_ref=r5_ref):
            n_grp, hd = r5_ref.shape[3], r5_ref.shape[4]
            for g in range(n_grp):
                r5_ref[0, :, 0, g, :] = acc[:, g * hd:(g + 1) * hd]


def _matmul(a, a_s, w, layer=0, n=None, res=None, res_s=None, tm=1024, tn=512, rows5=None):
    m, k = a.shape
    ms = a_s.shape[0]
    n = w.shape[2] if n is None else n
    tm = min(m, tm)
    tn = min(n, tn)
    has_res = res is not None
    in_specs = [pl.BlockSpec((tm, k), lambda j, i: (i, 0)),
                pl.BlockSpec((ms, k), lambda j, i: (0, 0)),
                pl.BlockSpec((None, k, tn), lambda j, i: (layer, 0, j))]
    args = [a, a_s, w]
    if has_res:
        in_specs += [pl.BlockSpec((tm, tn), lambda j, i: (i, j)),
                     pl.BlockSpec((ms, tn), lambda j, i: (0, j))]
        args += [res, res_s]
    out_specs = [pl.BlockSpec((tm, tn), lambda j, i: (i, j)),
                 pl.BlockSpec((ms, tn), lambda j, i: (0, j))]
    out_shape = [jax.ShapeDtypeStruct((m, n), F32), jax.ShapeDtypeStruct((ms, n), F32)]
    n_rows5 = 0
    if rows5 is not None:
        batch, seq, n_kv, hd = rows5
        n_rows5 = n // (2 * tn)
        nt = seq // tm
        assert tn == n_kv * hd and seq % tm == 0 and n % (2 * tn) == 0

        def rows5_map(br):
            def index_map(j, i):
                active, before = j // 2 == br, j < 2 * br
                park = lambda first, last: jnp.where(before, first, last)
                return (jnp.where(active, i // nt, park(0, batch - 1)),
                        jnp.where(active, i % nt, park(0, nt - 1)),
                        jnp.where(active, j % 2, park(0, 1)), 0, 0)
            return index_map

        for br in range(n_rows5):
            out_specs.append(pl.BlockSpec((1, tm, 1, n_kv, hd), rows5_map(br)))
            out_shape.append(jax.ShapeDtypeStruct((batch, seq, 2, n_kv, hd), F32))
    return pl.pallas_call(
        functools.partial(_mm_kernel, has_res=has_res, n_rows5=n_rows5),
        grid=(n // tn, m // tm),
        in_specs=in_specs,
        out_specs=out_specs,
        out_shape=out_shape,
        scratch_shapes=[pltpu.VMEM((k, tn), BF16)],
        compiler_params=_cparams("arbitrary", "arbitrary"),
        name="matmul",
    )(*args)


def _lru_coeffs(xc, wrg_ref, wig_ref, brg, big, lam, pos0_row):
    rows, cb = xc.shape
    bs = wrg_ref.shape[-1]
    xcb = xc.astype(BF16)
    r_parts, i_parts = [], []
    for n in range(cb // bs):
        xs = xcb[:, n * bs:(n + 1) * bs]
        r_parts.append(_dot(xs, wrg_ref[n]))
        i_parts.append(_dot(xs, wig_ref[n]))
    r = _sigmoid(jnp.concatenate(r_parts, axis=1) + brg)
    i = _sigmoid(jnp.concatenate(i_parts, axis=1) + big)
    nl = -lam
    softplus = jnp.maximum(nl, 0.0) + jnp.log1p(jnp.exp(-jnp.abs(nl)))
    log_a = -LRU_C * r * softplus
    a = jnp.exp(log_a)
    mult = jnp.sqrt(-jnp.tanh(log_a) * (a * a + 1.0))
    if pos0_row is not None:
        row = lax.broadcasted_iota(jnp.int32, (rows, cb), 0)
        mult = jnp.where(row == pos0_row, 1.0, mult)
    return a, mult * i * xc


def _lru_prompt_kernel(xb_ref, gate_ref, cw_ref, cb_ref, wrg_ref, brg_ref, wig_ref, big_ref,
                       lam_ref, h0_ref, c0_ref, g_ref, hl_ref, ct_ref, st_scr, un_scr, tail_scr, h_scr):
    tc = pl.program_id(2)
    tt, cb = xb_ref.shape
    nlb = cb // LANES
    seg = tt // SUBLANES
    tail = (CONV_W - 1) * SUBLANES

    @pl.when(tc == 0)
    def _():
        h_scr[...] = jnp.broadcast_to(h0_ref[0], (SUBLANES, cb))
        for i in range(CONV_W - 1):
            tail_scr[i * SUBLANES:(i + 1) * SUBLANES, :] = jnp.broadcast_to(c0_ref[0, i:i + 1, :],
                                                                             (SUBLANES, cb))

    pitch = seg + SEG_PAD

    def to_segments(src_ref, slot):
        cols = []
        for lb in range(nlb):
            for j in range(SUBLANES):
                st_scr[slot, lb, j * pitch:j * pitch + seg, :] = src_ref[j * seg:(j + 1) * seg,
                                                                         lb * LANES:(lb + 1) * LANES]
            cols.append(jnp.concatenate(
                [st_scr[slot, lb, pl.ds(k, SUBLANES, stride=pitch), :] for k in range(seg)], axis=0))
        return jnp.concatenate(cols, axis=1)

    x = to_segments(xb_ref, 0)

    sub = lax.broadcasted_iota(jnp.int32, (SUBLANES, cb), 0)
    heads = []
    for i in range(CONV_W - 1):
        cur = x[tt - tail + i * SUBLANES:tt - tail + (i + 1) * SUBLANES]
        prev = tail_scr[i * SUBLANES:(i + 1) * SUBLANES, :]
        heads.append(pltpu.roll(jnp.where(sub == SUBLANES - 1, prev, cur), 1, axis=0))
    tail_scr[...] = x[tt - tail:tt]
    w = cw_ref[...]
    xc = cb_ref[...]
    for k in range(CONV_W - 1):
        m = CONV_W - 1 - k
        xc = xc + w[k:k + 1] * jnp.concatenate(heads[CONV_W - 1 - m:] + [x[0:tt - m * SUBLANES]], axis=0)
    xc = xc + w[CONV_W - 1:CONV_W] * x

    a, b_in = _lru_coeffs(xc, wrg_ref, wig_ref, brg_ref[...], big_ref[...], lam_ref[...],
                           jnp.where(tc == 0, 0, -1))

    blk = lambda v, k: v[k * SUBLANES:(k + 1) * SUBLANES]
    e, p = blk(b_in, 0), blk(a, 0)
    for k in range(1, seg):
        e = blk(a, k) * e + blk(b_in, k)
        p = blk(a, k) * p
    carry = h_scr[0:1, :]
    carries = [carry]
    for j in range(SUBLANES - 1):
        carry = p[j:j + 1] * carry + e[j:j + 1]
        carries.append(carry)
    h = jnp.concatenate(carries, axis=0)
    hs = []
    for k in range(seg):
        h = blk(a, k) * h + blk(b_in, k)
        hs.append(h)
    h_scr[...] = jnp.broadcast_to(h[SUBLANES - 1:SUBLANES], (SUBLANES, cb))
    g = jnp.concatenate(hs, axis=0) * _silu(to_segments(gate_ref, 1))

    for lb in range(nlb):
        for k in range(seg):
            un_scr[lb, pl.ds(k, SUBLANES, stride=pitch), :] = g[k * SUBLANES:(k + 1) * SUBLANES,
                                                                lb * LANES:(lb + 1) * LANES]
        for j in range(SUBLANES):
            g_ref[j * seg:(j + 1) * seg, lb * LANES:(lb + 1) * LANES] = (
                un_scr[lb, j * pitch:j * pitch + seg, :].astype(g_ref.dtype))

    @pl.when(tc == pl.num_programs(2) - 1)
    def _():
        hl_ref[0] = h[SUBLANES - 1:SUBLANES]
        ct_ref[0] = xb_ref[tt - (CONV_W - 1):tt, :]


def _lru_prompt(u, batch, seq, cw, cb_, wrg, brg, wig, big, lam, h0, c0):
    m, c2 = u.shape
    c = c2 // 2
    cb = min(c, 1024)
    tt = min(seq, 256)
    nt = seq // tt
    ncb = c // cb
    bs = wrg.shape[-1]
    nb = cb // bs
    vec = lambda: pl.BlockSpec((1, cb), lambda b, j, t: (0, j))
    return pl.pallas_call(
        _lru_prompt_kernel,
        grid=(batch, ncb, nt),
        in_specs=[
            pl.BlockSpec((tt, cb), lambda b, j, t: (b * nt + t, j)),
            pl.BlockSpec((tt, cb), lambda b, j, t: (b * nt + t, ncb + j)),
            pl.BlockSpec((CONV_W, cb), lambda b, j, t: (0, j)),
            vec(),
            pl.BlockSpec((nb, bs, bs), lambda b, j, t: (j, 0, 0)),
            vec(),
            pl.BlockSpec((nb, bs, bs), lambda b, j, t: (j, 0, 0)),
            vec(),
            vec(),
            pl.BlockSpec((1, 1, cb), lambda b, j, t: (b, 0, j)),
            pl.BlockSpec((1, CONV_W - 1, cb), lambda b, j, t: (b, 0, j)),
        ],
        out_specs=[
            pl.BlockSpec((tt, cb), lambda b, j, t: (b * nt + t, j)),
            pl.BlockSpec((1, 1, cb), lambda b, j, t: (b, 0, j)),
            pl.BlockSpec((1, CONV_W - 1, cb), lambda b, j, t: (b, 0, j)),
        ],
        out_shape=[
            jax.ShapeDtypeStruct((m, c), BF16),
            jax.ShapeDtypeStruct((batch, 1, c), F32),
            jax.ShapeDtypeStruct((batch, CONV_W - 1, c), F32),
        ],
        scratch_shapes=[pltpu.VMEM((2, cb // LANES, tt + SUBLANES * SEG_PAD, LANES), F32),
                        pltpu.VMEM((cb // LANES, tt + SUBLANES * SEG_PAD, LANES), F32),
                        pltpu.VMEM(((CONV_W - 1) * SUBLANES, cb), F32),
                        pltpu.VMEM((SUBLANES, cb), F32)],
        compiler_params=_cparams("parallel", "parallel", "arbitrary"),
        name="lru_prompt",
    )(u, u, cw, cb_.reshape(1, c), wrg, brg.reshape(1, c), wig, big.reshape(1, c),
      lam.reshape(1, c), h0.reshape(batch, 1, c), c0)


def _lru_sample_kernel(xb_ref, gate_ref, cw_ref, cb_ref, wrg_ref, brg_ref, wig_ref, big_ref,
                       lam_ref, h0_ref, c0_ref, g_ref, h_ref):
    w = cw_ref[...]
    xc = cb_ref[...]
    for k in range(CONV_W - 1):
        xc = xc + w[k:k + 1] * c0_ref[k]
    xc = xc + w[CONV_W - 1:CONV_W] * xb_ref[...]
    a, b_in = _lru_coeffs(xc, wrg_ref, wig_ref, brg_ref[...], big_ref[...], lam_ref[...], None)
    h = a * h0_ref[...] + b_in
    h_ref[...] = h
    g_ref[...] = (h * _silu(gate_ref[...])).astype(g_ref.dtype)


def _lru_sample(u, cw, cb_, wrg, brg, wig, big, lam, h0, c0):
    rows, c2 = u.shape
    c = c2 // 2
    cb = min(c, 1024)
    ncb = c // cb
    bs = wrg.shape[-1]
    nb = cb // bs
    vec = lambda: pl.BlockSpec((1, cb), lambda j: (0, j))
    return pl.pallas_call(
        _lru_sample_kernel,
        grid=(ncb,),
        in_specs=[
            pl.BlockSpec((rows, cb), lambda j: (0, j)),
            pl.BlockSpec((rows, cb), lambda j: (0, ncb + j)),
            pl.BlockSpec((CONV_W, cb), lambda j: (0, j)),
            vec(),
            pl.BlockSpec((nb, bs, bs), lambda j: (j, 0, 0)),
            vec(),
            pl.BlockSpec((nb, bs, bs), lambda j: (j, 0, 0)),
            vec(),
            vec(),
            pl.BlockSpec((rows, cb), lambda j: (0, j)),
            pl.BlockSpec((CONV_W - 1, rows, cb), lambda j: (0, 0, j)),
        ],
        out_specs=[pl.BlockSpec((rows, cb), lambda j: (0, j)),
                   pl.BlockSpec((rows, cb), lambda j: (0, j))],
        out_shape=[jax.ShapeDtypeStruct((rows, c), BF16),
                   jax.ShapeDtypeStruct((rows, c), F32)],
        compiler_params=_cparams("parallel"),
        name="lru_sample",
    )(u, u, cw, cb_.reshape(1, c), wrg, brg.reshape(1, c), wig, big.reshape(1, c),
      lam.reshape(1, c), h0, c0)


GATHER_PAGES = 4


def _page_gather_kernel(pt_ref, *refs):
    del pt_ref
    c_refs, o_ref, slab_scr = refs[:GATHER_PAGES], refs[GATHER_PAGES], refs[GATHER_PAGES + 1]
    n_sg = o_ref.shape[1]
    hd = c_refs[0].shape[2] // n_sg
    rows = PAGE_SIZE // CMP_STRIDE
    for q, c_ref in enumerate(c_refs):
        for sg in range(n_sg):
            slab_scr[q, sg] = c_ref[0, :, sg * hd:(sg + 1) * hd]
            for l in range(CMP_STRIDE):
                o_ref[0, sg, q * rows:(q + 1) * rows, l * hd:(l + 1) * hd] = (
                    slab_scr[q, sg, pl.ds(l, rows, stride=CMP_STRIDE), :])


def _page_gather(pages, page_table, n_sg, hd):
    bs, n_pages = page_table.shape
    rows = PAGE_SIZE // CMP_STRIDE
    assert n_pages % GATHER_PAGES == 0

    def page_map(q):
        return lambda b, j, pt: (pt[b * n_pages + j * GATHER_PAGES + q], 0, 0)

    return pl.pallas_call(
        _page_gather_kernel,
        grid_spec=pltpu.PrefetchScalarGridSpec(
            num_scalar_prefetch=1,
            grid=(bs, n_pages // GATHER_PAGES),
            in_specs=[pl.BlockSpec((1, PAGE_SIZE, n_sg * hd), page_map(q))
                      for q in range(GATHER_PAGES)],
            out_specs=pl.BlockSpec((1, n_sg, GATHER_PAGES * rows, CMP_STRIDE * hd),
                                   lambda b, j, pt: (b, 0, j, 0)),
            scratch_shapes=[pltpu.VMEM((GATHER_PAGES, n_sg, PAGE_SIZE, hd), F32)],
        ),
        out_shape=jax.ShapeDtypeStruct((bs, n_sg, n_pages * rows, CMP_STRIDE * hd), F32),
        compiler_params=_cparams("parallel", "arbitrary"),
        name="page_gather",
    )(page_table.reshape(-1), *([pages] * GATHER_PAGES))


def _compress_kernel(r_ref, pos_ref, w1_ref, w2_ref, kn_ref, o_ref, q_scr, *, n_kv):
    sg = pl.program_id(1)
    rows = r_ref[0, 0]
    nr = rows.shape[0]
    hid = w1_ref.shape[2] // 2
    xa = (rows + pos_ref[0, 0:1]).astype(BF16)
    xb = (rows + pos_ref[0, 1:2]).astype(BF16)
    w1 = w1_ref[0]
    p = _dot(xa, w1[:, :hid])
    q_scr[0:nr, :] = _dot(xb, w1[:, hid:])
    q_scr[nr:nr + 8, :] = jnp.zeros((8, hid), F32)
    pre = p + q_scr[1:nr + 1, :]
    comp = _dot(_silu(pre).astype(BF16), w2_ref[0])
    normed = _rms(comp, kn_ref[...])
    o_ref[0, 0] = jnp.where(sg < n_kv, normed, comp).astype(o_ref.dtype)


def _compress(r16, posab, w1ab, w2, k_norm0, n_kv):
    b, n_sg, nr, kk = r16.shape
    hid2 = w1ab.shape[2]
    hd = w2.shape[2]
    return pl.pallas_call(
        functools.partial(_compress_kernel, n_kv=n_kv),
        grid=(b, n_sg),
        in_specs=[
            pl.BlockSpec((1, 1, nr, kk), lambda i, s: (i, s, 0, 0)),
            pl.BlockSpec((1, 2, kk), lambda i, s: (s // n_kv, 0, 0)),
            pl.BlockSpec((1, kk, hid2), lambda i, s: (s // n_kv, 0, 0)),
            pl.BlockSpec((1, hid2 // 2, hd), lambda i, s: (s // n_kv, 0, 0)),
            pl.BlockSpec((1, hd), lambda i, s: (0, 0)),
        ],
        out_specs=pl.BlockSpec((1, 1, nr, hd), lambda i, s: (i, s, 0, 0)),
        out_shape=jax.ShapeDtypeStruct((b, n_sg, nr, hd), BF16),
        scratch_shapes=[pltpu.VMEM((nr + 8, hid2 // 2), F32)],
        compiler_params=_cparams("parallel", "arbitrary"),
        name="compress",
    )(r16, posab, w1ab, w2, k_norm0.reshape(1, hd))


def _attn_prompt_kernel(uq_ref, ug_ref, bg_ref, gb_ref, ks_ref, vs_ref, kw_ref, vw_ref,
                        kc_ref, vc_ref, wb_ref, tc_ref, mit_ref, ex_ref, qn_ref, kn_ref, og_ref,
                        ks_scr, vs_scr, kw_scr, vw_scr, madd_scr, lg_scr, p_scr, m_scr,
                        *, n_heads_grp, n_slc):
    r_ = n_heads_grp
    c = pl.program_id(2)
    t0 = c * QB
    seq, hd = ks_ref.shape
    rq = r_ * QB
    wk = WINDOW + QB
    n_wt = wk // QB

    @pl.when(c == 0)
    def _():
        ks_scr[...] = _rms(ks_ref[...], kn_ref[1:2]).astype(BF16)
        vs_scr[:, 0:hd] = vs_ref[...].astype(BF16)
        vs_scr[:, hd:2 * hd] = jnp.ones((seq, hd), BF16)
        kw_scr[0:WINDOW, :] = jnp.zeros((WINDOW, hd), BF16)
        vw_scr[0:WINDOW, :] = jnp.zeros((WINDOW, 2 * hd), BF16)
        kw_scr[WINDOW:WINDOW + seq, :] = _rms(kw_ref[...], kn_ref[2:3]).astype(BF16)
        vw_scr[WINDOW:WINDOW + seq, 0:hd] = vw_ref[...].astype(BF16)
        vw_scr[WINDOW:WINDOW + seq, hd:2 * hd] = jnp.ones((seq, hd), BF16)
        p_scr[...] = jnp.zeros(p_scr.shape, BF16)

    uq = uq_ref[...]
    qg = qn_ref[...] * hd ** -0.5
    qb = jnp.concatenate(
        [uq[:, r * hd:(r + 1) * hd]
         * lax.rsqrt(jnp.mean(uq[:, r * hd:(r + 1) * hd] ** 2, axis=-1, keepdims=True) + EPS) * qg
         for r in range(r_)], axis=0).astype(BF16)

    n_c = kc_ref.shape[2]
    bias_c = pltpu.roll(tc_ref[...], (c * (QB // CMP_STRIDE) + QB // CMP_STRIDE) % n_c, axis=2)
    lc = _dot_nt(qb, kc_ref[0, 0]).reshape(r_, QB, n_c) + bias_c
    tq = t0 + lax.broadcasted_iota(jnp.int32, (QB, n_c), 0)
    nn = lax.broadcasted_iota(jnp.int32, (QB, n_c), 1)
    c_ok = (tq - (nn * CMP_STRIDE + (CMP_BLK - 1)) >= 0)[None]
    lc = jnp.where(c_ok, lc, NEG)
    e = jnp.exp(lc - jnp.max(lc, axis=-1, keepdims=True))
    pc = jnp.where(c_ok, e / jnp.sum(e, axis=-1, keepdims=True), 0.0)
    pcb = pc.reshape(rq, n_c).astype(BF16)
    o_cmp = _dot(pcb, vc_ref[0, 0])
    imp_all = _dot_nt(mit_ref[...], pcb)
    imp = imp_all[:, 0:QB]
    for r in range(1, r_):
        imp = imp + imp_all[:, r * QB:(r + 1) * QB]

    jj = lax.broadcasted_iota(jnp.int32, (n_slc, QB), 0)
    cur = (t0 + lax.broadcasted_iota(jnp.int32, (n_slc, QB), 1)) // SLC_BLK
    forced = (jj == 0) | (jj == cur) | (jj == cur - 1)
    allowed = jj <= cur
    score = jnp.where(forced, BIG, jnp.where(allowed, imp, NEG))
    rank = jnp.zeros((n_slc, QB), F32)
    for k in range(n_slc):
        sk = score[k:k + 1, :]
        rank = rank + jnp.where(sk > score, 1.0, jnp.where((sk == score) & (k < jj), 1.0, 0.0))
    sel = jnp.where((rank < min(TOP_N, n_slc)) & allowed, 1.0, 0.0).astype(BF16)
    selx = lax.dot_general(sel, ex_ref[...], (((0,), (0,)), ((), ())),
                           preferred_element_type=F32)
    tk = 2 * QB
    for kp in range(seq // tk):
        madd_scr[kp] = (selx[:, kp * tk:(kp + 1) * tk] - 1.0) * BIG

    m_scr[...] = jnp.full((rq, tk), NEG, F32)
    near1 = wb_ref[:, :, WINDOW - QB:WINDOW]
    near0 = wb_ref[:, :, WINDOW:WINDOW + QB]

    def logits_pair(kp, bias_lo, bias_hi):
        k0 = pl.multiple_of(kp * tk, tk)
        s3 = _dot_nt(qb, ks_scr[pl.ds(k0, tk), :]).reshape(r_, QB, tk) + madd_scr[kp][None]
        if bias_lo is not None or bias_hi is not None:
            lo, hi = s3[:, :, 0:QB], s3[:, :, QB:tk]
            lo = lo if bias_lo is None else lo + bias_lo
            hi = hi if bias_hi is None else hi + bias_hi
            s3 = jnp.concatenate([lo, hi], axis=2)
        s = s3.reshape(rq, tk)
        lg_scr[kp] = s
        m_scr[...] = jnp.maximum(m_scr[...], s)

    def far_body(kp, carry):
        logits_pair(kp, None, None)
        return carry

    last = c // 2
    c_odd = c % 2 == 1
    lax.fori_loop(0, jnp.where(c_odd, last, jnp.maximum(last - 1, 0)), far_body, 0)

    @pl.when(c_odd)
    def _():
        logits_pair(last, near1, near0)

    @pl.when(jnp.logical_not(c_odd) & (c >= 2))
    def _():
        logits_pair(last - 1, None, near1)

    @pl.when(jnp.logical_not(c_odd))
    def _():
        logits_pair(last, near0, None)

    m_scr[...] = jnp.broadcast_to(jnp.max(m_scr[...], axis=-1, keepdims=True), (rq, tk))

    def prob_body(kp, carry):
        k0 = pl.multiple_of(kp * tk, tk)
        p_scr[:, pl.ds(k0, tk)] = jnp.exp(lg_scr[kp] - m_scr[...]).astype(BF16)
        return carry

    lax.fori_loop(0, last + 1, prob_body, 0)
    acc = _dot(p_scr[...], vs_scr[...])
    o_slc = acc[:, 0:hd] / acc[:, hd:2 * hd]

    w0 = pl.multiple_of(t0, QB)
    sw = _dot_nt(qb, kw_scr[pl.ds(w0, wk), :])
    tiles = []
    for j in range(n_wt):
        s = sw[:, j * QB:(j + 1) * QB]
        if j == 0 or j >= n_wt - 2:
            s = s + wb_ref[:, :, j * QB:(j + 1) * QB].reshape(rq, QB)
        if j < n_wt - 1:
            s = s + jnp.where(c + j >= n_wt - 1, 0.0, NEG)
        tiles.append(s)
    mw = tiles[0]
    for s in tiles[1:]:
        mw = jnp.maximum(mw, s)
    mw = jnp.max(mw, axis=-1, keepdims=True)
    pw = jnp.concatenate([jnp.exp(s - mw).astype(BF16) for s in tiles], axis=1)
    accw = _dot(pw, vw_scr[pl.ds(w0, wk), :])
    o_win = accw[:, 0:hd] / accw[:, hd:2 * hd]

    bg = _sigmoid(bg_ref[...] + gb_ref[0])
    ug = ug_ref[...]
    for r in range(r_):
        rows = slice(r * QB, (r + 1) * QB)
        o = (bg[:, 3 * r:3 * r + 1] * o_cmp[rows] + bg[:, 3 * r + 1:3 * r + 2] * o_slc[rows]
             + bg[:, 3 * r + 2:3 * r + 3] * o_win[rows])
        og_ref[:, r * hd:(r + 1) * hd] = (o * _silu(ug[:, r * hd:(r + 1) * hd])).astype(og_ref.dtype)


def _attn_prompt(u, bgs, gbias, kv, comp, wbias, tcfix, mimp_t, expand, q_norm, k_norm,
                 batch, seq, n_kv, n_heads):
    m = u.shape[0]
    hd = k_norm.shape[1]
    r_ = n_heads // n_kv
    hq = n_heads * hd
    nq = seq // QB
    n_c = comp.shape[2]
    n_slc = seq // SLC_BLK
    rq = r_ * QB
    wk = WINDOW + QB
    n_sg = 2 * n_kv
    kv_spec = lambda col0: pl.BlockSpec((seq, hd), lambda b, g, c: (b, col0 + g))
    return pl.pallas_call(
        functools.partial(_attn_prompt_kernel, n_heads_grp=r_, n_slc=n_slc),
        grid=(batch, n_kv, nq),
        in_specs=[
            pl.BlockSpec((QB, r_ * hd), lambda b, g, c: (b * nq + c, g)),
            pl.BlockSpec((QB, r_ * hd), lambda b, g, c: (b * nq + c, n_kv + g)),
            pl.BlockSpec((QB, LANES), lambda b, g, c: (b * nq + c, g)),
            pl.BlockSpec((1, 1, LANES), lambda b, g, c: (g, 0, 0)),
            kv_spec(n_sg), kv_spec(n_sg + n_kv), kv_spec(2 * n_sg), kv_spec(2 * n_sg + n_kv),
            pl.BlockSpec((1, 1, n_c, hd), lambda b, g, c: (b, g, 0, 0)),
            pl.BlockSpec((1, 1, n_c, hd), lambda b, g, c: (b, n_kv + g, 0, 0)),
            pl.BlockSpec((r_, QB, wk), lambda b, g, c: (g, 0, 0)),
            pl.BlockSpec((r_, QB, n_c), lambda b, g, c: (g, 0, 0)),
            pl.BlockSpec((n_slc, n_c), lambda b, g, c: (0, 0)),
            pl.BlockSpec((n_slc, seq), lambda b, g, c: (0, 0)),
            pl.BlockSpec((1, hd), lambda b, g, c: (0, 0)),
            pl.BlockSpec((3, hd), lambda b, g, c: (0, 0)),
        ],
        out_specs=pl.BlockSpec((QB, r_ * hd), lambda b, g, c: (b * nq + c, g)),
        out_shape=jax.ShapeDtypeStruct((m, hq), BF16),
        scratch_shapes=[
            pltpu.VMEM((seq, hd), BF16), pltpu.VMEM((seq, 2 * hd), BF16),
            pltpu.VMEM((seq + WINDOW, hd), BF16), pltpu.VMEM((seq + WINDOW, 2 * hd), BF16),
            pltpu.VMEM((seq // (2 * QB), QB, 2 * QB), F32),
            pltpu.VMEM((seq // (2 * QB), rq, 2 * QB), F32),
            pltpu.VMEM((rq, seq), BF16),
            pltpu.VMEM((rq, 2 * QB), F32),
        ],
        compiler_params=_cparams("parallel", "parallel", "arbitrary"),
        name="attn_prompt",
    )(u, u, bgs, gbias, kv, kv, kv, kv, comp, comp, wbias, tcfix, mimp_t, expand,
      q_norm.reshape(1, hd), k_norm)


def _attn_sample_cmp_kernel(q_ref, kc_ref, vc_ref, cb_ref, mi_ref, qn_ref, oc_ref, sel_ref,
                            *, n_slc):
    r_, hd = q_ref.shape[1], q_ref.shape[2]
    qb = (_rms(q_ref[0], qn_ref[...]) * hd ** -0.5).astype(BF16)
    lc = _dot_nt(qb, kc_ref[0, 0]) + cb_ref[0]
    e = jnp.exp(lc - jnp.max(lc, axis=-1, keepdims=True))
    pcb = (e / jnp.sum(e, axis=-1, keepdims=True)).astype(BF16)
    oc_ref[0] = _dot(pcb, vc_ref[0, 0])
    nsp = mi_ref.shape[1]
    imp = jnp.sum(_dot(pcb, mi_ref[...]), axis=0, keepdims=True)

    jl = lax.broadcasted_iota(jnp.int32, (1, nsp), 1)
    forced = (jl == 0) | (jl == n_slc - 1) | (jl == n_slc - 2)
    score = jnp.where(jl < n_slc, jnp.where(forced, BIG, imp), LOWEST)
    s_rows = jnp.broadcast_to(score, (nsp, nsp))
    kk = lax.broadcasted_iota(jnp.int32, (nsp, nsp), 0)
    jj = lax.broadcasted_iota(jnp.int32, (nsp, nsp), 1)
    s_col = jnp.sum(jnp.where(kk == jj, s_rows, 0.0), axis=1, keepdims=True)
    ahead = jnp.where(s_col > s_rows, 1.0, jnp.where((s_col == s_rows) & (kk < jj), 1.0, 0.0))
    rank = jnp.sum(ahead, axis=0, keepdims=True)
    lane = lax.broadcasted_iota(jnp.int32, (1, LANES), 1)
    out = jnp.zeros((1, LANES), F32)
    jf = jl.astype(F32)
    for k in range(min(TOP_N, n_slc)):
        idx_k = jnp.sum(jnp.where(rank == k, jf, 0.0), axis=1, keepdims=True)
        out = out + jnp.where(lane == k, idx_k, 0.0)
    sel_ref[0, 0] = jnp.broadcast_to(out, (8, LANES)).astype(jnp.int32)


def _attn_sample_cmp(q3, comp, cbias, mimp, q_norm, n_kv, n_slc):
    bs, n_heads, hd = q3.shape
    r_ = n_heads // n_kv
    nr = comp.shape[2]
    nsp = mimp.shape[1]
    return pl.pallas_call(
        functools.partial(_attn_sample_cmp_kernel, n_slc=n_slc),
        grid=(bs, n_kv),
        in_specs=[
            pl.BlockSpec((1, r_, hd), lambda b, g: (b, g, 0)),
            pl.BlockSpec((1, 1, nr, hd), lambda b, g: (b, g, 0, 0)),
            pl.BlockSpec((1, 1, nr, hd), lambda b, g: (b, n_kv + g, 0, 0)),
            pl.BlockSpec((1, r_, nr), lambda b, g: (g, 0, 0)),
            pl.BlockSpec((nr, nsp), lambda b, g: (0, 0)),
            pl.BlockSpec((1, hd), lambda b, g: (0, 0)),
        ],
        out_specs=[pl.BlockSpec((1, r_, hd), lambda b, g: (b, g, 0)),
                   pl.BlockSpec((1, 1, 8, LANES), lambda b, g: (b, g, 0, 0))],
        out_shape=[jax.ShapeDtypeStruct((bs, n_heads, hd), F32),
                   jax.ShapeDtypeStruct((bs, n_kv, 8, LANES), jnp.int32)],
        compiler_params=_cparams("parallel", "parallel"),
        name="attn_sample_cmp",
    )(q3, comp, comp, cbias, mimp, q_norm.reshape(1, hd))


def _attn_sample_kernel(sel_ref, pt_ref, q_ref, gate_ref, bg_ref, gb_ref, oc_ref, ksn_ref, vsn_ref,
                        kw_ref, vw_ref, kwn_ref, vwn_ref, tb_ref, wb_ref, qn_ref, kn_ref, *rest,
                        n_sel, n_slc, n_kv):
    del pt_ref
    k_refs, v_refs, o_ref = rest[:n_sel], rest[n_sel:2 * n_sel], rest[2 * n_sel]
    b, g = pl.program_id(0), pl.program_id(1)
    r_, hd = q_ref.shape[1], q_ref.shape[2]
    qb = (_rms(q_ref[0], qn_ref[...]) * hd ** -0.5).astype(BF16)

    k_new = ksn_ref[0, 0]
    v_new = vsn_ref[0, 0]
    logits, values = [], []
    for k in range(n_sel):
        idx = sel_ref[(b * n_kv + g) * TOP_N + k]
        is_new = idx == n_slc - 1
        kt = jnp.where(is_new, jnp.broadcast_to(k_new, (SLC_BLK, hd)), k_refs[k][0])
        vt = jnp.where(is_new, jnp.broadcast_to(v_new, (SLC_BLK, hd)), v_refs[k][0])
        logits.append(_dot_nt(qb, _rms(kt, kn_ref[1:2]).astype(BF16)) + tb_ref[0, idx])
        values.append(vt.astype(BF16))
    m = logits[0].max(axis=-1, keepdims=True)
    for s in logits[1:]:
        m = jnp.maximum(m, s.max(axis=-1, keepdims=True))
    es = [jnp.exp(s - m) for s in logits]
    den = es[0].sum(axis=-1, keepdims=True)
    for e in es[1:]:
        den = den + e.sum(axis=-1, keepdims=True)
    o_slc = jnp.zeros((r_, hd), F32)
    for e, v in zip(es, values):
        o_slc = o_slc + _dot((e / den).astype(BF16), v)

    n_w = kw_ref.shape[1]
    lw = _dot_nt(qb, _rms(kw_ref[0], kn_ref[2:3]).astype(BF16)) + wb_ref[0, :, 0:n_w]
    kwn = _rms(kwn_ref[0, 0], kn_ref[2:3]).astype(BF16).astype(F32)
    l_new = jnp.sum(qb.astype(F32) * kwn, axis=-1, keepdims=True) + wb_ref[0, :, n_w:n_w + 1]
    mw = jnp.maximum(lw.max(axis=-1, keepdims=True), l_new)
    ew, e_new = jnp.exp(lw - mw), jnp.exp(l_new - mw)
    dw = ew.sum(axis=-1, keepdims=True) + e_new
    o_win = (_dot((ew / dw).astype(BF16), vw_ref[0].astype(BF16))
             + (e_new / dw).astype(BF16).astype(F32) * vwn_ref[0, 0].astype(BF16).astype(F32))

    bg = _sigmoid(bg_ref[0] + gb_ref[...])
    o = bg[:, 0:1] * oc_ref[0] + bg[:, 1:2] * o_slc + bg[:, 2:3] * o_win
    o_ref[0] = o * _silu(gate_ref[0])


def _attn_sample(sel, page_table, q3, gate3, bg3, gbias2, o_cmp, cache_slc, slc_new, win_state,
                 win_new, tb, wb, q_norm, k_norm, n_kv, n_slc):
    bs, n_heads, hd = q3.shape
    r_ = n_heads // n_kv
    n_pages = page_table.shape[1]
    n_pool = cache_slc.shape[0]
    n_sel = min(TOP_N, n_slc)
    cols = 2 * n_kv * hd
    halves = PAGE_SIZE // SLC_BLK
    cache2 = cache_slc.reshape(n_pool * halves, SLC_BLK, cols)
    n_w = win_state.shape[1]
    win2 = win_state.reshape(bs, n_w, cols)

    def blk_map(k, off):
        def index_map(b, g, sel_ref, pt_ref):
            idx = jnp.minimum(sel_ref[(b * n_kv + g) * TOP_N + k], n_slc - 2)
            page = pt_ref[b * n_pages + idx // halves]
            return (page * halves + idx % halves, 0, off + g)
        return index_map

    head_spec = lambda: pl.BlockSpec((1, r_, hd), lambda b, g, s, p: (b, g, 0))
    in_specs = [
        head_spec(), head_spec(),
        pl.BlockSpec((1, r_, 3), lambda b, g, s, p: (b, g, 0)),
        pl.BlockSpec((r_, 3), lambda b, g, s, p: (g, 0)),
        head_spec(),
        pl.BlockSpec((1, 1, 1, hd), lambda b, g, s, p: (b, g, 0, 0)),
        pl.BlockSpec((1, 1, 1, hd), lambda b, g, s, p: (b, n_kv + g, 0, 0)),
        pl.BlockSpec((1, n_w, hd), lambda b, g, s, p: (b, 0, g)),
        pl.BlockSpec((1, n_w, hd), lambda b, g, s, p: (b, 0, n_kv + g)),
        pl.BlockSpec((1, 1, 1, hd), lambda b, g, s, p: (b, g, 0, 0)),
        pl.BlockSpec((1, 1, 1, hd), lambda b, g, s, p: (b, n_kv + g, 0, 0)),
        pl.BlockSpec((1, n_slc, r_, SLC_BLK), lambda b, g, s, p: (g, 0, 0, 0)),
        pl.BlockSpec((1, r_, wb.shape[2]), lambda b, g, s, p: (g, 0, 0)),
        pl.BlockSpec((1, hd), lambda b, g, s, p: (0, 0)),
        pl.BlockSpec((3, hd), lambda b, g, s, p: (0, 0)),
    ]
    in_specs += [pl.BlockSpec((1, SLC_BLK, hd), blk_map(k, 0)) for k in range(n_sel)]
    in_specs += [pl.BlockSpec((1, SLC_BLK, hd), blk_map(k, n_kv)) for k in range(n_sel)]
    return pl.pallas_call(
        functools.partial(_attn_sample_kernel, n_sel=n_sel, n_slc=n_slc, n_kv=n_kv),
        grid_spec=pltpu.PrefetchScalarGridSpec(
            num_scalar_prefetch=2,
            grid=(bs, n_kv),
            in_specs=in_specs,
            out_specs=pl.BlockSpec((1, r_, hd), lambda b, g, s, p: (b, g, 0)),
        ),
        out_shape=jax.ShapeDtypeStruct((bs, n_heads, hd), F32),
        compiler_params=_cparams("arbitrary", "arbitrary"),
        name="attn_sample",
    )(sel, page_table.reshape(-1), q3, gate3, bg3, gbias2, o_cmp, slc_new, slc_new, win2, win2,
      win_new, win_new, tb, wb, q_norm.reshape(1, hd), k_norm, *([cache2] * (2 * n_sel)))


def _bucket_np(d):
    d = np.maximum(d, 0)
    n_exact = N_BUCKETS // 2
    df = np.maximum(d, 1).astype(np.float64)
    large = n_exact + (np.log(df / n_exact) / math.log(MAX_DIST / n_exact)
                       * (N_BUCKETS - n_exact)).astype(np.int64)
    return np.where(d < n_exact, d, np.minimum(large, N_BUCKETS - 1))


def _dist_bias(rel_table, d, valid, shift):
    fd = rel_table.astype(F32)[_bucket_np(np.arange(MAX_DIST + 1))]
    if shift:
        fd = fd - fd[MAX_DIST:MAX_DIST + 1]
    vals = jnp.moveaxis(fd[np.clip(d, 0, MAX_DIST)], -1, 0)
    return jnp.where(jnp.asarray(valid)[None], vals, NEG)


def _overlap_np(n_cmp, n_slc, rows, cols):
    cs = np.arange(n_cmp)[:, None] * CMP_STRIDE
    ss = np.arange(n_slc)[None, :] * SLC_BLK
    ov = np.minimum(cs + CMP_BLK, ss + SLC_BLK) - np.maximum(cs, ss)
    out = np.zeros((rows, cols), np.float32)
    out[:n_cmp, :n_slc] = np.maximum(ov, 0).astype(np.float32) / CMP_BLK
    return out


def _round_up(x, m):
    return (x + m - 1) // m * m


def kernel(x_prompt, x_sample, cache_cmp_kv, cache_slc_kv, state_win_kv, state_lru_h, state_conv,
           page_table, a_norm, a_w_in, a_conv_w, a_conv_b, a_w_rg, a_b_rg, a_w_ig, a_b_ig, a_lambda,
           a_w_out, kv_norm, w_kv, k_norm, cmp_pos, w_cmp1, w_cmp2, rel_table, b_norm, b_w_in,
           b_gate_bias, b_q_norm, b_w_out):
    batch, seq, d_model = x_prompt.shape
    bs = x_sample.shape[0]
    n_a, n_b = a_norm.shape[0], b_norm.shape[0]
    d_rnn = a_w_in.shape[2] // 2
    n_kv, hd = cache_cmp_kv.shape[3], cache_cmp_kv.shape[4]
    n_heads = rel_table.shape[1]
    r_ = n_heads // n_kv
    hq = n_heads * hd
    n_sg = 2 * n_kv
    cols = n_sg * hd
    n_pages = page_table.shape[1]
    past = n_pages * PAGE_SIZE
    m = batch * seq
    assert x_sample.shape[1] == 1 and seq % QB == 0 and seq >= WINDOW and hd == LANES
    assert seq // CMP_STRIDE == LANES and past >= WINDOW and state_win_kv.shape[1] == WINDOW

    xp = x_prompt.reshape(m, d_model)
    xs = jnp.pad(x_sample.reshape(bs, d_model), ((0, SAMPLE_ROWS - bs), (0, 0)))
    pad_rows = lambda a: jnp.pad(a, ((0, SAMPLE_ROWS - bs), (0, 0)))

    p_h, p_c, s_h, s_c = [], [], [], []
    h0_p = jnp.zeros((batch, d_rnn), F32)
    c0_p = jnp.zeros((batch, CONV_W - 1, d_rnn), F32)
    for l in range(n_a):
        wrg, wig = a_w_rg[l].astype(BF16), a_w_ig[l].astype(BF16)
        lru_w = (a_conv_w[l], a_conv_b[l], wrg, a_b_rg[l], wig, a_b_ig[l], a_lambda[l])
        u, us = _matmul(_norm_cast(xp, a_norm[l]), _norm_cast(xs, a_norm[l]), a_w_in, layer=l)
        gp, hl, ct = _lru_prompt(u, batch, seq, *lru_w, h0_p, c0_p)
        c0 = jnp.pad(jnp.swapaxes(state_conv[l], 0, 1), ((0, 0), (0, SAMPLE_ROWS - bs), (0, 0)))
        gs, hs = _lru_sample(us, *lru_w, pad_rows(state_lru_h[l]), c0)
        xp, xs = _matmul(gp, gs, a_w_out, layer=l, res=xp, res_s=xs)
        p_h.append(hl.reshape(batch, d_rnn))
        p_c.append(ct)
        s_h.append(hs[:bs])
        s_c.append(jnp.concatenate([state_conv[l][:, 1:], us[:bs, None, :d_rnn]], axis=1))

    kv, kvs, p_cmp_kv, p_slc_kv, p_win_rows = _matmul(
        _norm_cast(xp, kv_norm), _norm_cast(xs, kv_norm), w_kv[None], tm=512, tn=n_kv * hd,
        rows5=(batch, seq, n_kv, hd))
    kvs = kvs[:bs]
    rows5 = lambda a, n: a.reshape(n, -1, 2, n_kv, hd)
    p_win_kv = p_win_rows[:, seq - WINDOW:]
    s_cmp_kv, s_slc_kv = rows5(kvs[:, :cols], bs), rows5(kvs[:, cols:2 * cols], bs)
    s_win_rows = rows5(kvs[:, 2 * cols:], bs)
    s_win_kv = jnp.concatenate([state_win_kv, s_win_rows], axis=1)[:, 1:]

    half = CMP_BLK // 2
    hid = w_cmp1.shape[3]
    w1ab = jnp.concatenate([w_cmp1[:, :half].reshape(2, half * hd, hid),
                            w_cmp1[:, half:].reshape(2, half * hd, hid)], axis=2).astype(BF16)
    posab = jnp.stack([cmp_pos[:, :half].reshape(2, half * hd),
                       cmp_pos[:, half:].reshape(2, half * hd)], axis=1)
    w2b = w_cmp2.astype(BF16)
    own_pages = jnp.arange(m // PAGE_SIZE, dtype=jnp.int32).reshape(batch, seq // PAGE_SIZE)
    r16_p = _page_gather(kv.reshape(m // PAGE_SIZE, PAGE_SIZE, 3 * cols), own_pages, n_sg, hd)
    comp_p = _compress(r16_p, posab, w1ab, w2b, k_norm[0], n_kv)
    r16_s = _page_gather(cache_cmp_kv.reshape(-1, PAGE_SIZE, cols), page_table, n_sg, hd)
    comp_s = _compress(r16_s, posab, w1ab, w2b, k_norm[0], n_kv)

    n_c = seq // CMP_STRIDE
    n_cmp_p = (seq - CMP_BLK) // CMP_STRIDE + 1
    n_slc_p = seq // SLC_BLK
    wk = WINDOW + QB
    dv = WINDOW - np.arange(wk + 1)
    vrow = _dist_bias(rel_table, dv, dv >= 0, True)
    wbias = jnp.tile(vrow, (1, QB))[:, :QB * wk].reshape(n_heads, QB, wk)
    dc = (np.arange(QB)[:, None] - (np.arange(n_c)[None, :] - (n_c - QB // CMP_STRIDE)) * CMP_STRIDE
          - (CMP_BLK - 1))
    tcfix = jnp.where(jnp.asarray(dc >= 0)[None], _dist_bias(rel_table, dc, dc >= 0, True), 0.0)
    mimp_t = jnp.asarray(_overlap_np(n_cmp_p, n_slc_p, n_c, n_slc_p).T, BF16)
    expand = jnp.asarray((np.arange(n_slc_p)[:, None] == np.arange(seq)[None, :] // SLC_BLK), BF16)

    total = past + 1
    n_cmp_s = (total - CMP_BLK) // CMP_STRIDE + 1
    n_slc_s = -(-total // SLC_BLK)
    nr_s = past // CMP_STRIDE
    assert n_cmp_s == nr_s - 1 and n_slc_s == past // SLC_BLK + 1
    dcs = past - (np.arange(nr_s) * CMP_STRIDE + CMP_BLK - 1)
    cbias_s = _dist_bias(rel_table, dcs, (dcs >= 0) & (np.arange(nr_s) < n_cmp_s), False)
    cbias_s = cbias_s.reshape(n_kv, r_, nr_s)
    mimp_s = jnp.asarray(_overlap_np(n_cmp_s, n_slc_s, nr_s, _round_up(n_slc_s, LANES)), BF16)
    dss = past - np.arange(n_slc_s * SLC_BLK)
    tb_s = _dist_bias(rel_table, dss, dss >= 0, False)
    tb_s = tb_s.reshape(n_kv, r_, n_slc_s, SLC_BLK).transpose(0, 2, 1, 3)
    dws = WINDOW - np.arange(WINDOW + LANES)
    wb_s = _dist_bias(rel_table, dws, dws >= 0, False).reshape(n_kv, r_, WINDOW + LANES)

    slc_new = kvs[:, cols:2 * cols].reshape(bs, n_sg, 1, hd)
    win_new = kvs[:, 2 * cols:].reshape(bs, n_sg, 1, hd)
    slab_pad = ((0, 0), (0, 0), (0, LANES - 3 * r_))
    for l in range(n_b):
        w_bg = jnp.pad(b_w_in[l][:, 2 * hq:].reshape(d_model, n_kv, 3 * r_), slab_pad)
        w_bg = w_bg.reshape(d_model, n_kv * LANES)
        gbias = b_gate_bias[l]
        gb_slab = jnp.pad(gbias.reshape(n_kv, 1, 3 * r_), slab_pad)

        xn, xns = _norm_cast(xp, b_norm[l]), _norm_cast(xs, b_norm[l])
        u, us = _matmul(xn, xns, b_w_in, layer=l, n=2 * hq)
        bgs, bgs_s = _matmul(xn, xns, w_bg[None])
        og = _attn_prompt(u, bgs, gb_slab, kv, comp_p, wbias, tcfix, mimp_t, expand,
                          b_q_norm[l], k_norm, batch, seq, n_kv, n_heads)

        us = us[:bs]
        q3 = us[:, :hq].reshape(bs, n_heads, hd)
        gate3 = us[:, hq:].reshape(bs, n_heads, hd)
        bg3 = bgs_s[:bs].reshape(bs, n_kv, LANES)[:, :, :3 * r_].reshape(bs, n_heads, 3)
        o_cmp, sel = _attn_sample_cmp(q3, comp_s, cbias_s, mimp_s, b_q_norm[l], n_kv, n_slc_s)
        os_ = _attn_sample(sel[:, :, 0, :TOP_N].reshape(-1), page_table, q3, gate3, bg3,
                           gbias.reshape(n_heads, 3), o_cmp, cache_slc_kv, slc_new, state_win_kv,
                           win_new, tb_s, wb_s, b_q_norm[l], k_norm, n_kv, n_slc_s)
        xp, xs = _matmul(og, pad_rows(os_.reshape(bs, hq).astype(BF16)), b_w_out, layer=l,
                         res=xp, res_s=xs)

    return (xp.reshape(batch, seq, d_model), xs[:bs].reshape(bs, 1, d_model),
            p_cmp_kv, p_slc_kv, p_win_kv, jnp.stack(p_h), jnp.stack(p_c),
            s_cmp_kv, s_slc_kv, s_win_kv, jnp.stack(s_h), jnp.stack(s_c))
```

```python
import functools
import math

import numpy as np
import jax
import jax.numpy as jnp
from jax import lax
from jax.experimental import pallas as pl
from jax.experimental.pallas import tpu as pltpu

F32 = jnp.float32
BF16 = jnp.bfloat16

EPS = 1e-6
NEG = -1e30
BIG = 1e30
LOWEST = -3e38
LRU_C = 8.0
CONV_W = 4
CMP_BLK = 32
CMP_STRIDE = 16
SLC_BLK = 64
TOP_N = 16
WINDOW = 512
N_BUCKETS = 32
MAX_DIST = 128
PAGE_SIZE = 128
QB = 128
LANES = 128
SUBLANES = 8
SEG_PAD = 8
SAMPLE_ROWS = 16
VMEM_LIMIT = 48 * 1024 * 1024


def _cparams(*sem):
    return pltpu.CompilerParams(dimension_semantics=sem, vmem_limit_bytes=VMEM_LIMIT)


def _dot(a, b):
    return jnp.dot(a, b, preferred_element_type=F32)


def _dot_nt(a, b):
    return lax.dot_general(a, b, (((1,), (1,)), ((), ())), preferred_element_type=F32)


def _rms(x, g):
    return x * lax.rsqrt(jnp.mean(x * x, axis=-1, keepdims=True) + EPS) * g


def _sigmoid(x):
    return jax.nn.sigmoid(x)


def _silu(x):
    return x * jax.nn.sigmoid(x)


def _norm_kernel(x_ref, g_ref, o_ref):
    o_ref[...] = _rms(x_ref[...], g_ref[...]).astype(o_ref.dtype)


def _norm_cast(x, g):
    m, d = x.shape
    tm = min(m, 512)
    return pl.pallas_call(
        _norm_kernel,
        grid=(m // tm,),
        in_specs=[pl.BlockSpec((tm, d), lambda i: (i, 0)),
                  pl.BlockSpec((1, d), lambda i: (0, 0))],
        out_specs=pl.BlockSpec((tm, d), lambda i: (i, 0)),
        out_shape=jax.ShapeDtypeStruct((m, d), BF16),
        compiler_params=_cparams("parallel"),
        name="norm_cast",
    )(x, g.reshape(1, d))


def _mm_kernel(a_ref, as_ref, w_ref, *rest, has_res, n_rows5):
    if has_res:
        r_ref, rs_ref, o_ref, os_ref = rest[:4]
    else:
        o_ref, os_ref = rest[:2]
    wb_scr = rest[-1]
    rows5_refs = rest[-1 - n_rows5:-1]

    @pl.when(pl.program_id(1) == 0)
    def _():
        wb_scr[...] = w_ref[...].astype(BF16)
        acc_s = _dot(as_ref[...], wb_scr[...])
        os_ref[...] = rs_ref[...] + acc_s if has_res else acc_s

    acc = _dot(a_ref[...], wb_scr[...])
    o_ref[...] = r_ref[...] + acc if has_res else acc

    for br, r5_ref in enumerate(rows5_refs):
        @pl.when(pl.program_id(0) // 2 == br)
        def _(r5_ref=r5_ref):
            n_grp, hd = r5_ref.shape[3], r5_ref.shape[4]
            for g in range(n_grp):
                r5_ref[0, :, 0, g, :] = acc[:, g * hd:(g + 1) * hd]


def _matmul(a, a_s, w, layer=0, n=None, res=None, res_s=None, tm=1024, tn=512, rows5=None):
    m, k = a.shape
    ms = a_s.shape[0]
    n = w.shape[2] if n is None else n
    tm = min(m, tm)
    tn = min(n, tn)
    has_res = res is not None
    in_specs = [pl.BlockSpec((tm, k), lambda j, i: (i, 0)),
                pl.BlockSpec((ms, k), lambda j, i: (0, 0)),
                pl.BlockSpec((None, k, tn), lambda j, i: (layer, 0, j))]
    args = [a, a_s, w]
    if has_res:
        in_specs += [pl.BlockSpec((tm, tn), lambda j, i: (i, j)),
                     pl.BlockSpec((ms, tn), lambda j, i: (0, j))]
        args += [res, res_s]
    out_specs = [pl.BlockSpec((tm, tn), lambda j, i: (i, j)),
                 pl.BlockSpec((ms, tn), lambda j, i: (0, j))]
    out_shape = [jax.ShapeDtypeStruct((m, n), F32), jax.ShapeDtypeStruct((ms, n), F32)]
    n_rows5 = 0
    if rows5 is not None:
        batch, seq, n_kv, hd = rows5
        n_rows5 = n // (2 * tn)
        nt = seq // tm
        assert tn == n_kv * hd and seq % tm == 0 and n % (2 * tn) == 0

        def rows5_map(br):
            def index_map(j, i):
                active, before = j // 2 == br, j < 2 * br
                park = lambda first, last: jnp.where(before, first, last)
                return (jnp.where(active, i // nt, park(0, batch - 1)),
                        jnp.where(active, i % nt, park(0, nt - 1)),
                        jnp.where(active, j % 2, park(0, 1)), 0, 0)
            return index_map

        for br in range(n_rows5):
            out_specs.append(pl.BlockSpec((1, tm, 1, n_kv, hd), rows5_map(br)))
            out_shape.append(jax.ShapeDtypeStruct((batch, seq, 2, n_kv, hd), F32))
    return pl.pallas_call(
        functools.partial(_mm_kernel, has_res=has_res, n_rows5=n_rows5),
        grid=(n // tn, m // tm),
        in_specs=in_specs,
        out_specs=out_specs,
        out_shape=out_shape,
        scratch_shapes=[pltpu.VMEM((k, tn), BF16)],
        compiler_params=_cparams("arbitrary", "arbitrary"),
        name="matmul",
    )(*args)


def _lru_coeffs(xc, wrg_ref, wig_ref, brg, big, lam, pos0_row):
    rows, cb = xc.shape
    bs = wrg_ref.shape[-1]
    xcb = xc.astype(BF16)
    r_parts, i_parts = [], []
    for n in range(cb // bs):
        xs = xcb[:, n * bs:(n + 1) * bs]
        r_parts.append(_dot(xs, wrg_ref[n]))
        i_parts.append(_dot(xs, wig_ref[n]))
    r = _sigmoid(jnp.concatenate(r_parts, axis=1) + brg)
    i = _sigmoid(jnp.concatenate(i_parts, axis=1) + big)
    nl = -lam
    softplus = jnp.maximum(nl, 0.0) + jnp.log1p(jnp.exp(-jnp.abs(nl)))
    log_a = -LRU_C * r * softplus
    a = jnp.exp(log_a)
    mult = jnp.sqrt(-jnp.tanh(log_a) * (a * a + 1.0))
    if pos0_row is not None:
        row = lax.broadcasted_iota(jnp.int32, (rows, cb), 0)
        mult = jnp.where(row == pos0_row, 1.0, mult)
    return a, mult * i * xc


def _lru_prompt_kernel(xb_ref, gate_ref, cw_ref, cb_ref, wrg_ref, brg_ref, wig_ref, big_ref,
                       lam_ref, h0_ref, c0_ref, g_ref, hl_ref, ct_ref, st_scr, un_scr, tail_scr, h_scr):
    tc = pl.program_id(2)
    tt, cb = xb_ref.shape
    nlb = cb // LANES
    seg = tt // SUBLANES
    tail = (CONV_W - 1) * SUBLANES

    @pl.when(tc == 0)
    def _():
        h_scr[...] = jnp.broadcast_to(h0_ref[0], (SUBLANES, cb))
        for i in range(CONV_W - 1):
            tail_scr[i * SUBLANES:(i + 1) * SUBLANES, :] = jnp.broadcast_to(c0_ref[0, i:i + 1, :],
                                                                             (SUBLANES, cb))

    pitch = seg + SEG_PAD

    def to_segments(src_ref, slot):
        cols = []
        for lb in range(nlb):
            for j in range(SUBLANES):
                st_scr[slot, lb, j * pitch:j * pitch + seg, :] = src_ref[j * seg:(j + 1) * seg,
                                                                         lb * LANES:(lb + 1) * LANES]
            cols.append(jnp.concatenate(
                [st_scr[slot, lb, pl.ds(k, SUBLANES, stride=pitch), :] for k in range(seg)], axis=0))
        return jnp.concatenate(cols, axis=1)

    x = to_segments(xb_ref, 0)

    sub = lax.broadcasted_iota(jnp.int32, (SUBLANES, cb), 0)
    heads = []
    for i in range(CONV_W - 1):
        cur = x[tt - tail + i * SUBLANES:tt - tail + (i + 1) * SUBLANES]
        prev = tail_scr[i * SUBLANES:(i + 1) * SUBLANES, :]
        heads.append(pltpu.roll(jnp.where(sub == SUBLANES - 1, prev, cur), 1, axis=0))
    tail_scr[...] = x[tt - tail:tt]
    w = cw_ref[...]
    xc = cb_ref[...]
    for k in range(CONV_W - 1):
        m = CONV_W - 1 - k
        xc = xc + w[k:k + 1] * jnp.concatenate(heads[CONV_W - 1 - m:] + [x[0:tt - m * SUBLANES]], axis=0)
    xc = xc + w[CONV_W - 1:CONV_W] * x

    a, b_in = _lru_coeffs(xc, wrg_ref, wig_ref, brg_ref[...], big_ref[...], lam_ref[...],
                           jnp.where(tc == 0, 0, -1))

    blk = lambda v, k: v[k * SUBLANES:(k + 1) * SUBLANES]
    e, p = blk(b_in, 0), blk(a, 0)
    for k in range(1, seg):
        e = blk(a, k) * e + blk(b_in, k)
        p = blk(a, k) * p
    carry = h_scr[0:1, :]
    carries = [carry]
    for j in range(SUBLANES - 1):
        carry = p[j:j + 1] * carry + e[j:j + 1]
        carries.append(carry)
    h = jnp.concatenate(carries, axis=0)
    hs = []
    for k in range(seg):
        h = blk(a, k) * h + blk(b_in, k)
        hs.append(h)
    h_scr[...] = jnp.broadcast_to(h[SUBLANES - 1:SUBLANES], (SUBLANES, cb))
    g = jnp.concatenate(hs, axis=0) * _silu(to_segments(gate_ref, 1))

    for lb in range(nlb):
        for k in range(seg):
            un_scr[lb, pl.ds(k, SUBLANES, stride=pitch), :] = g[k * SUBLANES:(k + 1) * SUBLANES,
                                                                lb * LANES:(lb + 1) * LANES]
        for j in range(SUBLANES):
            g_ref[j * seg:(j + 1) * seg, lb * LANES:(lb + 1) * LANES] = (
                un_scr[lb, j * pitch:j * pitch + seg, :].astype(g_ref.dtype))

    @pl.when(tc == pl.num_programs(2) - 1)
    def _():
        hl_ref[0] = h[SUBLANES - 1:SUBLANES]
        ct_ref[0] = xb_ref[tt - (CONV_W - 1):tt, :]


def _lru_prompt(u, batch, seq, cw, cb_, wrg, brg, wig, big, lam, h0, c0):
    m, c2 = u.shape
    c = c2 // 2
    cb = min(c, 1024)
    tt = min(seq, 256)
    nt = seq // tt
    ncb = c // cb
    bs = wrg.shape[-1]
    nb = cb // bs
    vec = lambda: pl.BlockSpec((1, cb), lambda b, j, t: (0, j))
    return pl.pallas_call(
        _lru_prompt_kernel,
        grid=(batch, ncb, nt),
        in_specs=[
            pl.BlockSpec((tt, cb), lambda b, j, t: (b * nt + t, j)),
            pl.BlockSpec((tt, cb), lambda b, j, t: (b * nt + t, ncb + j)),
            pl.BlockSpec((CONV_W, cb), lambda b, j, t: (0, j)),
            vec(),
            pl.BlockSpec((nb, bs, bs), lambda b, j, t: (j, 0, 0)),
            vec(),
            pl.BlockSpec((nb, bs, bs), lambda b, j, t: (j, 0, 0)),
            vec(),
            vec(),
            pl.BlockSpec((1, 1, cb), lambda b, j, t: (b, 0, j)),
            pl.BlockSpec((1, CONV_W - 1, cb), lambda b, j, t: (b, 0, j)),
        ],
        out_specs=[
            pl.BlockSpec((tt, cb), lambda b, j, t: (b * nt + t, j)),
            pl.BlockSpec((1, 1, cb), lambda b, j, t: (b, 0, j)),
            pl.BlockSpec((1, CONV_W - 1, cb), lambda b, j, t: (b, 0, j)),
        ],
        out_shape=[
            jax.ShapeDtypeStruct((m, c), BF16),
            jax.ShapeDtypeStruct((batch, 1, c), F32),
            jax.ShapeDtypeStruct((batch, CONV_W - 1, c), F32),
        ],
        scratch_shapes=[pltpu.VMEM((2, cb // LANES, tt + SUBLANES * SEG_PAD, LANES), F32),
                        pltpu.VMEM((cb // LANES, tt + SUBLANES * SEG_PAD, LANES), F32),
                        pltpu.VMEM(((CONV_W - 1) * SUBLANES, cb), F32),
                        pltpu.VMEM((SUBLANES, cb), F32)],
        compiler_params=_cparams("parallel", "parallel", "arbitrary"),
        name="lru_prompt",
    )(u, u, cw, cb_.reshape(1, c), wrg, brg.reshape(1, c), wig, big.reshape(1, c),
      lam.reshape(1, c), h0.reshape(batch, 1, c), c0)


def _lru_sample_kernel(xb_ref, gate_ref, cw_ref, cb_ref, wrg_ref, brg_ref, wig_ref, big_ref,
                       lam_ref, h0_ref, c0_ref, g_ref, h_ref):
    w = cw_ref[...]
    xc = cb_ref[...]
    for k in range(CONV_W - 1):
        xc = xc + w[k:k + 1] * c0_ref[k]
    xc = xc + w[CONV_W - 1:CONV_W] * xb_ref[...]
    a, b_in = _lru_coeffs(xc, wrg_ref, wig_ref, brg_ref[...], big_ref[...], lam_ref[...], None)
    h = a * h0_ref[...] + b_in
    h_ref[...] = h
    g_ref[...] = (h * _silu(gate_ref[...])).astype(g_ref.dtype)


def _lru_sample(u, cw, cb_, wrg, brg, wig, big, lam, h0, c0):
    rows, c2 = u.shape
    c = c2 // 2
    cb = min(c, 1024)
    ncb = c // cb
    bs = wrg.shape[-1]
    nb = cb // bs
    vec = lambda: pl.BlockSpec((1, cb), lambda j: (0, j))
    return pl.pallas_call(
        _lru_sample_kernel,
        grid=(ncb,),
        in_specs=[
            pl.BlockSpec((rows, cb), lambda j: (0, j)),
            pl.BlockSpec((rows, cb), lambda j: (0, ncb + j)),
            pl.BlockSpec((CONV_W, cb), lambda j: (0, j)),
            vec(),
            pl.BlockSpec((nb, bs, bs), lambda j: (j, 0, 0)),
            vec(),
            pl.BlockSpec((nb, bs, bs), lambda j: (j, 0, 0)),
            vec(),
            vec(),
            pl.BlockSpec((rows, cb), lambda j: (0, j)),
            pl.BlockSpec((CONV_W - 1, rows, cb), lambda j: (0, 0, j)),
        ],
        out_specs=[pl.BlockSpec((rows, cb), lambda j: (0, j)),
                   pl.BlockSpec((rows, cb), lambda j: (0, j))],
        out_shape=[jax.ShapeDtypeStruct((rows, c), BF16),
                   jax.ShapeDtypeStruct((rows, c), F32)],
        compiler_params=_cparams("parallel"),
        name="lru_sample",
    )(u, u, cw, cb_.reshape(1, c), wrg, brg.reshape(1, c), wig, big.reshape(1, c),
      lam.reshape(1, c), h0, c0)


GATHER_PAGES = 4


def _page_gather_kernel(pt_ref, *refs):
    del pt_ref
    c_refs, o_ref, slab_scr = refs[:GATHER_PAGES], refs[GATHER_PAGES], refs[GATHER_PAGES + 1]
    n_kv, hd = c_refs[0].shape[3], c_refs[0].shape[4]
    rows = PAGE_SIZE // CMP_STRIDE
    for q, c_ref in enumerate(c_refs):
        for sg in range(2 * n_kv):
            slab_scr[q, sg] = c_ref[0, :, sg // n_kv, sg % n_kv, :]
            for l in range(CMP_STRIDE):
                o_ref[0, sg, q * rows:(q + 1) * rows, l * hd:(l + 1) * hd] = (
                    slab_scr[q, sg, pl.ds(l, rows, stride=CMP_STRIDE), :])


def _page_gather(pages, page_table):
    bs, n_pages = page_table.shape
    n_kv, hd = pages.shape[3], pages.shape[4]
    n_sg = 2 * n_kv
    rows = PAGE_SIZE // CMP_STRIDE
    assert n_pages % GATHER_PAGES == 0

    def page_map(q):
        return lambda b, j, pt: (pt[b * n_pages + j * GATHER_PAGES + q], 0, 0, 0, 0)

    return pl.pallas_call(
        _page_gather_kernel,
        grid_spec=pltpu.PrefetchScalarGridSpec(
            num_scalar_prefetch=1,
            grid=(bs, n_pages // GATHER_PAGES),
            in_specs=[pl.BlockSpec((1, PAGE_SIZE, 2, n_kv, hd), page_map(q))
                      for q in range(GATHER_PAGES)],
            out_specs=pl.BlockSpec((1, n_sg, GATHER_PAGES * rows, CMP_STRIDE * hd),
                                   lambda b, j, pt: (b, 0, j, 0)),
            scratch_shapes=[pltpu.VMEM((GATHER_PAGES, n_sg, PAGE_SIZE, hd), F32)],
        ),
        out_shape=jax.ShapeDtypeStruct((bs, n_sg, n_pages * rows, CMP_STRIDE * hd), F32),
        compiler_params=_cparams("parallel", "arbitrary"),
        name="page_gather",
    )(page_table.reshape(-1), *([pages] * GATHER_PAGES))


def _compress_kernel(r_ref, pos_ref, w1_ref, w2_ref, kn_ref, o_ref, q_scr, *, n_kv):
    sg = pl.program_id(1)
    rows = r_ref[0, 0]
    nr = rows.shape[0]
    hid = w1_ref.shape[2] // 2
    xa = (rows + pos_ref[0, 0:1]).astype(BF16)
    xb = (rows + pos_ref[0, 1:2]).astype(BF16)
    w1 = w1_ref[0]
    p = _dot(xa, w1[:, :hid])
    q_scr[0:nr, :] = _dot(xb, w1[:, hid:])
    q_scr[nr:nr + 8, :] = jnp.zeros((8, hid), F32)
    pre = p + q_scr[1:nr + 1, :]
    comp = _dot(_silu(pre).astype(BF16), w2_ref[0])
    normed = _rms(comp, kn_ref[...])
    o_ref[0, 0] = jnp.where(sg < n_kv, normed, comp).astype(o_ref.dtype)


def _compress(r16, posab, w1ab, w2, k_norm0, n_kv):
    b, n_sg, nr, kk = r16.shape
    hid2 = w1ab.shape[2]
    hd = w2.shape[2]
    return pl.pallas_call(
        functools.partial(_compress_kernel, n_kv=n_kv),
        grid=(b, n_sg),
        in_specs=[
            pl.BlockSpec((1, 1, nr, kk), lambda i, s: (i, s, 0, 0)),
            pl.BlockSpec((1, 2, kk), lambda i, s: (s // n_kv, 0, 0)),
            pl.BlockSpec((1, kk, hid2), lambda i, s: (s // n_kv, 0, 0)),
            pl.BlockSpec((1, hid2 // 2, hd), lambda i, s: (s // n_kv, 0, 0)),
            pl.BlockSpec((1, hd), lambda i, s: (0, 0)),
        ],
        out_specs=pl.BlockSpec((1, 1, nr, hd), lambda i, s: (i, s, 0, 0)),
        out_shape=jax.ShapeDtypeStruct((b, n_sg, nr, hd), BF16),
        scratch_shapes=[pltpu.VMEM((nr + 8, hid2 // 2), F32)],
        compiler_params=_cparams("parallel", "arbitrary"),
        name="compress",
    )(r16, posab, w1ab, w2, k_norm0.reshape(1, hd))


def _attn_prompt_kernel(uq_ref, ug_ref, bg_ref, gb_ref, ks_ref, vs_ref, kw_ref, vw_ref,
                        kc_ref, vc_ref, wb_ref, tc_ref, mit_ref, ex_ref, qn_ref, kn_ref, og_ref,
                        ks_scr, vs_scr, kw_scr, vw_scr, madd_scr, lg_scr, p_scr, m_scr,
                        *, n_heads_grp, n_slc):
    r_ = n_heads_grp
    c = pl.program_id(2)
    t0 = c * QB
    seq, hd = ks_ref.shape
    rq = r_ * QB
    wk = WINDOW + QB
    n_wt = wk // QB

    @pl.when(c == 0)
    def _():
        ks_scr[...] = _rms(ks_ref[...], kn_ref[1:2]).astype(BF16)
        vs_scr[:, 0:hd] = vs_ref[...].astype(BF16)
        vs_scr[:, hd:2 * hd] = jnp.ones((seq, hd), BF16)
        kw_scr[0:WINDOW, :] = jnp.zeros((WINDOW, hd), BF16)
        vw_scr[0:WINDOW, :] = jnp.zeros((WINDOW, 2 * hd), BF16)
        kw_scr[WINDOW:WINDOW + seq, :] = _rms(kw_ref[...], kn_ref[2:3]).astype(BF16)
        vw_scr[WINDOW:WINDOW + seq, 0:hd] = vw_ref[...].astype(BF16)
        vw_scr[WINDOW:WINDOW + seq, hd:2 * hd] = jnp.ones((seq, hd), BF16)
        p_scr[...] = jnp.zeros(p_scr.shape, BF16)

    uq = uq_ref[...]
    qg = qn_ref[...] * hd ** -0.5
    qb = jnp.concatenate(
        [uq[:, r * hd:(r + 1) * hd]
         * lax.rsqrt(jnp.mean(uq[:, r * hd:(r + 1) * hd] ** 2, axis=-1, keepdims=True) + EPS) * qg
         for r in range(r_)], axis=0).astype(BF16)

    w0 = pl.multiple_of(t0, QB)
    sw = _dot_nt(qb, kw_scr[pl.ds(w0, wk), :])
    tiles = []
    for j in range(n_wt):
        s = sw[:, j * QB:(j + 1) * QB]
        if j == 0 or j >= n_wt - 2:
            s = s + wb_ref[:, :, j * QB:(j + 1) * QB].reshape(rq, QB)
        if j < n_wt - 1:
            s = s + jnp.where(c + j >= n_wt - 1, 0.0, NEG)
        tiles.append(s)
    mw = tiles[0]
    for s in tiles[1:]:
        mw = jnp.maximum(mw, s)
    mw = jnp.max(mw, axis=-1, keepdims=True)
    pw = jnp.concatenate([jnp.exp(s - mw).astype(BF16) for s in tiles], axis=1)
    accw = _dot(pw, vw_scr[pl.ds(w0, wk), :])
    o_win = accw[:, 0:hd] / accw[:, hd:2 * hd]

    n_c = kc_ref.shape[2]
    bias_c = pltpu.roll(tc_ref[...], (c * (QB // CMP_STRIDE) + QB // CMP_STRIDE) % n_c, axis=2)
    lc = _dot_nt(qb, kc_ref[0, 0]).reshape(r_, QB, n_c) + bias_c
    tq = t0 + lax.broadcasted_iota(jnp.int32, (QB, n_c), 0)
    nn = lax.broadcasted_iota(jnp.int32, (QB, n_c), 1)
    c_ok = (tq - (nn * CMP_STRIDE + (CMP_BLK - 1)) >= 0)[None]
    lc = jnp.where(c_ok, lc, NEG)
    e = jnp.exp(lc - jnp.max(lc, axis=-1, keepdims=True))
    pc = jnp.where(c_ok, e / jnp.sum(e, axis=-1, keepdims=True), 0.0)
    pcb = pc.reshape(rq, n_c).astype(BF16)
    o_cmp = _dot(pcb, vc_ref[0, 0])
    imp_all = _dot_nt(mit_ref[...], pcb)
    imp = imp_all[:, 0:QB]
    for r in range(1, r_):
        imp = imp + imp_all[:, r * QB:(r + 1) * QB]

    jj = lax.broadcasted_iota(jnp.int32, (n_slc, QB), 0)
    cur = (t0 + lax.broadcasted_iota(jnp.int32, (n_slc, QB), 1)) // SLC_BLK
    forced = (jj == 0) | (jj == cur) | (jj == cur - 1)
    allowed = jj <= cur
    score = jnp.where(forced, BIG, jnp.where(allowed, imp, NEG))
    rank = jnp.zeros((n_slc, QB), F32)
    for k in range(n_slc):
        sk = score[k:k + 1, :]
        rank = rank + jnp.where(sk > score, 1.0, jnp.where((sk == score) & (k < jj), 1.0, 0.0))
    sel = jnp.where((rank < min(TOP_N, n_slc)) & allowed, 1.0, 0.0).astype(BF16)
    selx = lax.dot_general(sel, ex_ref[...], (((0,), (0,)), ((), ())),
                           preferred_element_type=F32)
    tk = 2 * QB
    for kp in range(seq // tk):
        madd_scr[kp] = (selx[:, kp * tk:(kp + 1) * tk] - 1.0) * BIG

    m_scr[...] = jnp.full((rq, tk), NEG, F32)
    near1 = wb_ref[:, :, WINDOW - QB:WINDOW]
    near0 = wb_ref[:, :, WINDOW:WINDOW + QB]

    def logits_pair(kp, bias_lo, bias_hi):
        k0 = pl.multiple_of(kp * tk, tk)
        s3 = _dot_nt(qb, ks_scr[pl.ds(k0, tk), :]).reshape(r_, QB, tk) + madd_scr[kp][None]
        if bias_lo is not None or bias_hi is not None:
            lo, hi = s3[:, :, 0:QB], s3[:, :, QB:tk]
            lo = lo if bias_lo is None else lo + bias_lo
            hi = hi if bias_hi is None else hi + bias_hi
            s3 = jnp.concatenate([lo, hi], axis=2)
        s = s3.reshape(rq, tk)
        lg_scr[kp] = s
        m_scr[...] = jnp.maximum(m_scr[...], s)

    def far_body(kp, carry):
        logits_pair(kp, None, None)
        return carry

    last = c // 2
    c_odd = c % 2 == 1
    lax.fori_loop(0, jnp.where(c_odd, last, jnp.maximum(last - 1, 0)), far_body, 0)

    @pl.when(c_odd)
    def _():
        logits_pair(last, near1, near0)

    @pl.when(jnp.logical_not(c_odd) & (c >= 2))
    def _():
        logits_pair(last - 1, None, near1)

    @pl.when(jnp.logical_not(c_odd))
    def _():
        logits_pair(last, near0, None)

    m_scr[...] = jnp.broadcast_to(jnp.max(m_scr[...], axis=-1, keepdims=True), (rq, tk))

    def prob_body(kp, carry):
        k0 = pl.multiple_of(kp * tk, tk)
        p_scr[:, pl.ds(k0, tk)] = jnp.exp(lg_scr[kp] - m_scr[...]).astype(BF16)
        return carry

    lax.fori_loop(0, last + 1, prob_body, 0)
    acc = _dot(p_scr[...], vs_scr[...])
    o_slc = acc[:, 0:hd] / acc[:, hd:2 * hd]

    bg = _sigmoid(bg_ref[...] + gb_ref[0])
    ug = ug_ref[...]
    for r in range(r_):
        rows = slice(r * QB, (r + 1) * QB)
        o = (bg[:, 3 * r:3 * r + 1] * o_cmp[rows] + bg[:, 3 * r + 1:3 * r + 2] * o_slc[rows]
             + bg[:, 3 * r + 2:3 * r + 3] * o_win[rows])
        og_ref[:, r * hd:(r + 1) * hd] = (o * _silu(ug[:, r * hd:(r + 1) * hd])).astype(og_ref.dtype)


def _attn_prompt(u, bgs, gbias, kv, comp, wbias, tcfix, mimp_t, expand, q_norm, k_norm,
                 batch, seq, n_kv, n_heads):
    m = u.shape[0]
    hd = k_norm.shape[1]
    r_ = n_heads // n_kv
    hq = n_heads * hd
    nq = seq // QB
    n_c = comp.shape[2]
    n_slc = seq // SLC_BLK
    rq = r_ * QB
    wk = WINDOW + QB
    n_sg = 2 * n_kv
    kv_spec = lambda col0: pl.BlockSpec((seq, hd), lambda b, g, c: (b, col0 + g))
    return pl.pallas_call(
        functools.partial(_attn_prompt_kernel, n_heads_grp=r_, n_slc=n_slc),
        grid=(batch, n_kv, nq),
        in_specs=[
            pl.BlockSpec((QB, r_ * hd), lambda b, g, c: (b * nq + c, g)),
            pl.BlockSpec((QB, r_ * hd), lambda b, g, c: (b * nq + c, n_kv + g)),
            pl.BlockSpec((QB, LANES), lambda b, g, c: (b * nq + c, g)),
            pl.BlockSpec((1, 1, LANES), lambda b, g, c: (g, 0, 0)),
            kv_spec(n_sg), kv_spec(n_sg + n_kv), kv_spec(2 * n_sg), kv_spec(2 * n_sg + n_kv),
            pl.BlockSpec((1, 1, n_c, hd), lambda b, g, c: (b, g, 0, 0)),
            pl.BlockSpec((1, 1, n_c, hd), lambda b, g, c: (b, n_kv + g, 0, 0)),
            pl.BlockSpec((r_, QB, wk), lambda b, g, c: (g, 0, 0)),
            pl.BlockSpec((r_, QB, n_c), lambda b, g, c: (g, 0, 0)),
            pl.BlockSpec((n_slc, n_c), lambda b, g, c: (0, 0)),
            pl.BlockSpec((n_slc, seq), lambda b, g, c: (0, 0)),
            pl.BlockSpec((1, hd), lambda b, g, c: (0, 0)),
            pl.BlockSpec((3, hd), lambda b, g, c: (0, 0)),
        ],
        out_specs=pl.BlockSpec((QB, r_ * hd), lambda b, g, c: (b * nq + c, g)),
        out_shape=jax.ShapeDtypeStruct((m, hq), BF16),
        scratch_shapes=[
            pltpu.VMEM((seq, hd), BF16), pltpu.VMEM((seq, 2 * hd), BF16),
            pltpu.VMEM((seq + WINDOW, hd), BF16), pltpu.VMEM((seq + WINDOW, 2 * hd), BF16),
            pltpu.VMEM((seq // (2 * QB), QB, 2 * QB), F32),
            pltpu.VMEM((seq // (2 * QB), rq, 2 * QB), F32),
            pltpu.VMEM((rq, seq), BF16),
            pltpu.VMEM((rq, 2 * QB), F32),
        ],
        compiler_params=_cparams("parallel", "parallel", "arbitrary"),
        name="attn_prompt",
    )(u, u, bgs, gbias, kv, kv, kv, kv, comp, comp, wbias, tcfix, mimp_t, expand,
      q_norm.reshape(1, hd), k_norm)


def _attn_sample_cmp_kernel(q_ref, kc_ref, vc_ref, cb_ref, mi_ref, qn_ref, oc_ref, sel_ref,
                            *, n_slc):
    r_, hd = q_ref.shape[1], q_ref.shape[2]
    qb = (_rms(q_ref[0], qn_ref[...]) * hd ** -0.5).astype(BF16)
    lc = _dot_nt(qb, kc_ref[0, 0]) + cb_ref[0]
    e = jnp.exp(lc - jnp.max(lc, axis=-1, keepdims=True))
    pcb = (e / jnp.sum(e, axis=-1, keepdims=True)).astype(BF16)
    oc_ref[0] = _dot(pcb, vc_ref[0, 0])
    nsp = mi_ref.shape[1]
    imp = jnp.sum(_dot(pcb, mi_ref[...]), axis=0, keepdims=True)

    jl = lax.broadcasted_iota(jnp.int32, (1, nsp), 1)
    forced = (jl == 0) | (jl == n_slc - 1) | (jl == n_slc - 2)
    score = jnp.where(jl < n_slc, jnp.where(forced, BIG, imp), LOWEST)
    s_rows = jnp.broadcast_to(score, (nsp, nsp))
    kk = lax.broadcasted_iota(jnp.int32, (nsp, nsp), 0)
    jj = lax.broadcasted_iota(jnp.int32, (nsp, nsp), 1)
    s_col = jnp.sum(jnp.where(kk == jj, s_rows, 0.0), axis=1, keepdims=True)
    ahead = jnp.where(s_col > s_rows, 1.0, jnp.where((s_col == s_rows) & (kk < jj), 1.0, 0.0))
    rank = jnp.sum(ahead, axis=0, keepdims=True)
    lane = lax.broadcasted_iota(jnp.int32, (1, LANES), 1)
    out = jnp.zeros((1, LANES), F32)
    jf = jl.astype(F32)
    for k in range(min(TOP_N, n_slc)):
        idx_k = jnp.sum(jnp.where(rank == k, jf, 0.0), axis=1, keepdims=True)
        out = out + jnp.where(lane == k, idx_k, 0.0)
    sel_ref[0, 0] = jnp.broadcast_to(out, (8, LANES)).astype(jnp.int32)


def _attn_sample_cmp(q3, comp, cbias, mimp, q_norm, n_kv, n_slc):
    bs, n_heads, hd = q3.shape
    r_ = n_heads // n_kv
    nr = comp.shape[2]
    nsp = mimp.shape[1]
    return pl.pallas_call(
        functools.partial(_attn_sample_cmp_kernel, n_slc=n_slc),
        grid=(bs, n_kv),
        in_specs=[
            pl.BlockSpec((1, r_, hd), lambda b, g: (b, g, 0)),
            pl.BlockSpec((1, 1, nr, hd), lambda b, g: (b, g, 0, 0)),
            pl.BlockSpec((1, 1, nr, hd), lambda b, g: (b, n_kv + g, 0, 0)),
            pl.BlockSpec((1, r_, nr), lambda b, g: (g, 0, 0)),
            pl.BlockSpec((nr, nsp), lambda b, g: (0, 0)),
            pl.BlockSpec((1, hd), lambda b, g: (0, 0)),
        ],
        out_specs=[pl.BlockSpec((1, r_, hd), lambda b, g: (b, g, 0)),
                   pl.BlockSpec((1, 1, 8, LANES), lambda b, g: (b, g, 0, 0))],
        out_shape=[jax.ShapeDtypeStruct((bs, n_heads, hd), F32),
                   jax.ShapeDtypeStruct((bs, n_kv, 8, LANES), jnp.int32)],
        compiler_params=_cparams("parallel", "parallel"),
        name="attn_sample_cmp",
    )(q3, comp, comp, cbias, mimp, q_norm.reshape(1, hd))


def _attn_sample_kernel(sel_ref, pt_ref, q_ref, gate_ref, bg_ref, gb_ref, oc_ref, ksn_ref, vsn_ref,
                        kw_ref, vw_ref, kwn_ref, vwn_ref, tb_ref, wb_ref, qn_ref, kn_ref, *rest,
                        n_sel, n_slc, n_kv):
    del pt_ref
    k_refs, v_refs, o_ref = rest[:n_sel], rest[n_sel:2 * n_sel], rest[2 * n_sel]
    b, g = pl.program_id(0), pl.program_id(1)
    r_, hd = q_ref.shape[1], q_ref.shape[2]
    qb = (_rms(q_ref[0], qn_ref[...]) * hd ** -0.5).astype(BF16)

    def own_group(ref):
        rows = ref[0, :, 0, 0, :]
        for gi in range(1, n_kv):
            rows = jnp.where(g == gi, ref[0, :, 0, gi, :], rows)
        return rows

    k_new = ksn_ref[0, 0]
    v_new = vsn_ref[0, 0]
    logits, values = [], []
    for k in range(n_sel):
        idx = sel_ref[(b * n_kv + g) * TOP_N + k]
        is_new = idx == n_slc - 1
        kt = jnp.where(is_new, jnp.broadcast_to(k_new, (SLC_BLK, hd)), own_group(k_refs[k]))
        vt = jnp.where(is_new, jnp.broadcast_to(v_new, (SLC_BLK, hd)), own_group(v_refs[k]))
        logits.append(_dot_nt(qb, _rms(kt, kn_ref[1:2]).astype(BF16)) + tb_ref[0, idx])
        values.append(vt.astype(BF16))
    m = logits[0].max(axis=-1, keepdims=True)
    for s in logits[1:]:
        m = jnp.maximum(m, s.max(axis=-1, keepdims=True))
    es = [jnp.exp(s - m) for s in logits]
    den = es[0].sum(axis=-1, keepdims=True)
    for e in es[1:]:
        den = den + e.sum(axis=-1, keepdims=True)
    o_slc = jnp.zeros((r_, hd), F32)
    for e, v in zip(es, values):
        o_slc = o_slc + _dot((e / den).astype(BF16), v)

    n_w = kw_ref.shape[1]
    lw = _dot_nt(qb, _rms(own_group(kw_ref), kn_ref[2:3]).astype(BF16)) + wb_ref[0, :, 0:n_w]
    kwn = _rms(kwn_ref[0, 0], kn_ref[2:3]).astype(BF16).astype(F32)
    l_new = jnp.sum(qb.astype(F32) * kwn, axis=-1, keepdims=True) + wb_ref[0, :, n_w:n_w + 1]
    mw = jnp.maximum(lw.max(axis=-1, keepdims=True), l_new)
    ew, e_new = jnp.exp(lw - mw), jnp.exp(l_new - mw)
    dw = ew.sum(axis=-1, keepdims=True) + e_new
    o_win = (_dot((ew / dw).astype(BF16), own_group(vw_ref).astype(BF16))
             + (e_new / dw).astype(BF16).astype(F32) * vwn_ref[0, 0].astype(BF16).astype(F32))

    bg = _sigmoid(bg_ref[0] + gb_ref[...])
    o = bg[:, 0:1] * oc_ref[0] + bg[:, 1:2] * o_slc + bg[:, 2:3] * o_win
    o_ref[0] = o * _silu(gate_ref[0])


def _attn_sample(sel, page_table, q3, gate3, bg3, gbias2, o_cmp, cache_slc, slc_new, win_state,
                 win_new, tb, wb, q_norm, k_norm, n_kv, n_slc):
    bs, n_heads, hd = q3.shape
    r_ = n_heads // n_kv
    n_pages = page_table.shape[1]
    n_pool = cache_slc.shape[0]
    n_sel = min(TOP_N, n_slc)
    halves = PAGE_SIZE // SLC_BLK
    n_w = win_state.shape[1]

    def blk_map(k, kv_idx):
        def index_map(b, g, sel_ref, pt_ref):
            idx = jnp.minimum(sel_ref[(b * n_kv + g) * TOP_N + k], n_slc - 2)
            return (pt_ref[b * n_pages + idx // halves], idx % halves, kv_idx, 0, 0)
        return index_map

    head_spec = lambda: pl.BlockSpec((1, r_, hd), lambda b, g, s, p: (b, g, 0))
    in_specs = [
        head_spec(), head_spec(),
        pl.BlockSpec((1, r_, 3), lambda b, g, s, p: (b, g, 0)),
        pl.BlockSpec((r_, 3), lambda b, g, s, p: (g, 0)),
        head_spec(),
        pl.BlockSpec((1, 1, 1, hd), lambda b, g, s, p: (b, g, 0, 0)),
        pl.BlockSpec((1, 1, 1, hd), lambda b, g, s, p: (b, n_kv + g, 0, 0)),
        pl.BlockSpec((1, n_w, 1, n_kv, hd), lambda b, g, s, p: (b, 0, 0, 0, 0)),
        pl.BlockSpec((1, n_w, 1, n_kv, hd), lambda b, g, s, p: (b, 0, 1, 0, 0)),
        pl.BlockSpec((1, 1, 1, hd), lambda b, g, s, p: (b, g, 0, 0)),
        pl.BlockSpec((1, 1, 1, hd), lambda b, g, s, p: (b, n_kv + g, 0, 0)),
        pl.BlockSpec((1, n_slc, r_, SLC_BLK), lambda b, g, s, p: (g, 0, 0, 0)),
        pl.BlockSpec((1, r_, wb.shape[2]), lambda b, g, s, p: (g, 0, 0)),
        pl.BlockSpec((1, hd), lambda b, g, s, p: (0, 0)),
        pl.BlockSpec((3, hd), lambda b, g, s, p: (0, 0)),
    ]
    in_specs += [pl.BlockSpec((1, SLC_BLK, 1, n_kv, hd), blk_map(k, 0)) for k in range(n_sel)]
    in_specs += [pl.BlockSpec((1, SLC_BLK, 1, n_kv, hd), blk_map(k, 1)) for k in range(n_sel)]
    return pl.pallas_call(
        functools.partial(_attn_sample_kernel, n_sel=n_sel, n_slc=n_slc, n_kv=n_kv),
        grid_spec=pltpu.PrefetchScalarGridSpec(
            num_scalar_prefetch=2,
            grid=(bs, n_kv),
            in_specs=in_specs,
            out_specs=pl.BlockSpec((1, r_, hd), lambda b, g, s, p: (b, g, 0)),
        ),
        out_shape=jax.ShapeDtypeStruct((bs, n_heads, hd), F32),
        compiler_params=_cparams("arbitrary", "arbitrary"),
        name="attn_sample",
    )(sel, page_table.reshape(-1), q3, gate3, bg3, gbias2, o_cmp, slc_new, slc_new, win_state,
      win_state, win_new, win_new, tb, wb, q_norm.reshape(1, hd), k_norm,
      *([cache_slc] * (2 * n_sel)))


def _bucket_np(d):
    d = np.maximum(d, 0)
    n_exact = N_BUCKETS // 2
    df = np.maximum(d, 1).astype(np.float64)
    large = n_exact + (np.log(df / n_exact) / math.log(MAX_DIST / n_exact)
                       * (N_BUCKETS - n_exact)).astype(np.int64)
    return np.where(d < n_exact, d, np.minimum(large, N_BUCKETS - 1))


def _dist_bias(rel_table, d, valid, shift):
    fd = rel_table.astype(F32)[_bucket_np(np.arange(MAX_DIST + 1))]
    if shift:
        fd = fd - fd[MAX_DIST:MAX_DIST + 1]
    vals = jnp.moveaxis(fd[np.clip(d, 0, MAX_DIST)], -1, 0)
    return jnp.where(jnp.asarray(valid)[None], vals, NEG)


def _overlap_np(n_cmp, n_slc, rows, cols):
    cs = np.arange(n_cmp)[:, None] * CMP_STRIDE
    ss = np.arange(n_slc)[None, :] * SLC_BLK
    ov = np.minimum(cs + CMP_BLK, ss + SLC_BLK) - np.maximum(cs, ss)
    out = np.zeros((rows, cols), np.float32)
    out[:n_cmp, :n_slc] = np.maximum(ov, 0).astype(np.float32) / CMP_BLK
    return out


def _round_up(x, m):
    return (x + m - 1) // m * m


def kernel(x_prompt, x_sample, cache_cmp_kv, cache_slc_kv, state_win_kv, state_lru_h, state_conv,
           page_table, a_norm, a_w_in, a_conv_w, a_conv_b, a_w_rg, a_b_rg, a_w_ig, a_b_ig, a_lambda,
           a_w_out, kv_norm, w_kv, k_norm, cmp_pos, w_cmp1, w_cmp2, rel_table, b_norm, b_w_in,
           b_gate_bias, b_q_norm, b_w_out):
    batch, seq, d_model = x_prompt.shape
    bs = x_sample.shape[0]
    n_a, n_b = a_norm.shape[0], b_norm.shape[0]
    d_rnn = a_w_in.shape[2] // 2
    n_kv, hd = cache_cmp_kv.shape[3], cache_cmp_kv.shape[4]
    n_heads = rel_table.shape[1]
    r_ = n_heads // n_kv
    hq = n_heads * hd
    n_sg = 2 * n_kv
    cols = n_sg * hd
    n_pages = page_table.shape[1]
    past = n_pages * PAGE_SIZE
    m = batch * seq
    assert x_sample.shape[1] == 1 and seq % QB == 0 and seq >= WINDOW and hd == LANES
    assert seq // CMP_STRIDE == LANES and past >= WINDOW and state_win_kv.shape[1] == WINDOW

    xp = x_prompt.reshape(m, d_model)
    xs = jnp.pad(x_sample.reshape(bs, d_model), ((0, SAMPLE_ROWS - bs), (0, 0)))
    pad_rows = lambda a: jnp.pad(a, ((0, SAMPLE_ROWS - bs), (0, 0)))

    p_h, p_c, s_h, s_c = [], [], [], []
    h0_p = jnp.zeros((batch, d_rnn), F32)
    c0_p = jnp.zeros((batch, CONV_W - 1, d_rnn), F32)
    for l in range(n_a):
        wrg, wig = a_w_rg[l].astype(BF16), a_w_ig[l].astype(BF16)
        lru_w = (a_conv_w[l], a_conv_b[l], wrg, a_b_rg[l], wig, a_b_ig[l], a_lambda[l])
        u, us = _matmul(_norm_cast(xp, a_norm[l]), _norm_cast(xs, a_norm[l]), a_w_in, layer=l)
        gp, hl, ct = _lru_prompt(u, batch, seq, *lru_w, h0_p, c0_p)
        c0 = jnp.pad(jnp.swapaxes(state_conv[l], 0, 1), ((0, 0), (0, SAMPLE_ROWS - bs), (0, 0)))
        gs, hs = _lru_sample(us, *lru_w, pad_rows(state_lru_h[l]), c0)
        xp, xs = _matmul(gp, gs, a_w_out, layer=l, res=xp, res_s=xs)
        p_h.append(hl.reshape(batch, d_rnn))
        p_c.append(ct)
        s_h.append(hs[:bs])
        s_c.append(jnp.concatenate([state_conv[l][:, 1:], us[:bs, None, :d_rnn]], axis=1))

    kv, kvs, p_cmp_kv, p_slc_kv, p_win_rows = _matmul(
        _norm_cast(xp, kv_norm), _norm_cast(xs, kv_norm), w_kv[None], tm=512, tn=n_kv * hd,
        rows5=(batch, seq, n_kv, hd))
    kvs = kvs[:bs]
    rows5 = lambda a, n: a.reshape(n, -1, 2, n_kv, hd)
    p_win_kv = p_win_rows[:, seq - WINDOW:]
    s_cmp_kv, s_slc_kv = rows5(kvs[:, :cols], bs), rows5(kvs[:, cols:2 * cols], bs)
    s_win_rows = rows5(kvs[:, 2 * cols:], bs)
    s_win_kv = jnp.concatenate([state_win_kv, s_win_rows], axis=1)[:, 1:]

    half = CMP_BLK // 2
    hid = w_cmp1.shape[3]
    w1ab = jnp.concatenate([w_cmp1[:, :half].reshape(2, half * hd, hid),
                            w_cmp1[:, half:].reshape(2, half * hd, hid)], axis=2).astype(BF16)
    posab = jnp.stack([cmp_pos[:, :half].reshape(2, half * hd),
                       cmp_pos[:, half:].reshape(2, half * hd)], axis=1)
    w2b = w_cmp2.astype(BF16)
    own_pages = jnp.arange(m // PAGE_SIZE, dtype=jnp.int32).reshape(batch, seq // PAGE_SIZE)
    r16_p = _page_gather(p_cmp_kv.reshape(m // PAGE_SIZE, PAGE_SIZE, 2, n_kv, hd), own_pages)
    comp_p = _compress(r16_p, posab, w1ab, w2b, k_norm[0], n_kv)
    comp_s = _compress(_page_gather(cache_cmp_kv, page_table), posab, w1ab, w2b, k_norm[0], n_kv)

    n_c = seq // CMP_STRIDE
    n_cmp_p = (seq - CMP_BLK) // CMP_STRIDE + 1
    n_slc_p = seq // SLC_BLK
    wk = WINDOW + QB
    dv = WINDOW - np.arange(wk + 1)
    vrow = _dist_bias(rel_table, dv, dv >= 0, True)
    wbias = jnp.tile(vrow, (1, QB))[:, :QB * wk].reshape(n_heads, QB, wk)
    dc = (np.arange(QB)[:, None] - (np.arange(n_c)[None, :] - (n_c - QB // CMP_STRIDE)) * CMP_STRIDE
          - (CMP_BLK - 1))
    tcfix = jnp.where(jnp.asarray(dc >= 0)[None], _dist_bias(rel_table, dc, dc >= 0, True), 0.0)
    mimp_t = jnp.asarray(_overlap_np(n_cmp_p, n_slc_p, n_c, n_slc_p).T, BF16)
    expand = jnp.asarray((np.arange(n_slc_p)[:, None] == np.arange(seq)[None, :] // SLC_BLK), BF16)

    total = past + 1
    n_cmp_s = (total - CMP_BLK) // CMP_STRIDE + 1
    n_slc_s = -(-total // SLC_BLK)
    nr_s = past // CMP_STRIDE
    assert n_cmp_s == nr_s - 1 and n_slc_s == past // SLC_BLK + 1
    dcs = past - (np.arange(nr_s) * CMP_STRIDE + CMP_BLK - 1)
    cbias_s = _dist_bias(rel_table, dcs, (dcs >= 0) & (np.arange(nr_s) < n_cmp_s), False)
    cbias_s = cbias_s.reshape(n_kv, r_, nr_s)
    mimp_s = jnp.asarray(_overlap_np(n_cmp_s, n_slc_s, nr_s, _round_up(n_slc_s, LANES)), BF16)
    dss = past - np.arange(n_slc_s * SLC_BLK)
    tb_s = _dist_bias(rel_table, dss, dss >= 0, False)
    tb_s = tb_s.reshape(n_kv, r_, n_slc_s, SLC_BLK).transpose(0, 2, 1, 3)
    dws = WINDOW - np.arange(WINDOW + LANES)
    wb_s = _dist_bias(rel_table, dws, dws >= 0, False).reshape(n_kv, r_, WINDOW + LANES)

    slc_new = kvs[:, cols:2 * cols].reshape(bs, n_sg, 1, hd)
    win_new = kvs[:, 2 * cols:].reshape(bs, n_sg, 1, hd)
    slab_pad = ((0, 0), (0, 0), (0, LANES - 3 * r_))
    for l in range(n_b):
        w_bg = jnp.pad(b_w_in[l][:, 2 * hq:].reshape(d_model, n_kv, 3 * r_), slab_pad)
        w_bg = w_bg.reshape(d_model, n_kv * LANES)
        gbias = b_gate_bias[l]
        gb_slab = jnp.pad(gbias.reshape(n_kv, 1, 3 * r_), slab_pad)

        xn, xns = _norm_cast(xp, b_norm[l]), _norm_cast(xs, b_norm[l])
        u, us = _matmul(xn, xns, b_w_in, layer=l, n=2 * hq)
        bgs, bgs_s = _matmul(xn, xns, w_bg[None])
        og = _attn_prompt(u, bgs, gb_slab, kv, comp_p, wbias, tcfix, mimp_t, expand,
                          b_q_norm[l], k_norm, batch, seq, n_kv, n_heads)

        us = us[:bs]
        q3 = us[:, :hq].reshape(bs, n_heads, hd)
        gate3 = us[:, hq:].reshape(bs, n_heads, hd)
        bg3 = bgs_s[:bs].reshape(bs, n_kv, LANES)[:, :, :3 * r_].reshape(bs, n_heads, 3)
        o_cmp, sel = _attn_sample_cmp(q3, comp_s, cbias_s, mimp_s, b_q_norm[l], n_kv, n_slc_s)
        os_ = _attn_sample(sel[:, :, 0, :TOP_N].reshape(-1), page_table, q3, gate3, bg3,
                           gbias.reshape(n_heads, 3), o_cmp, cache_slc_kv, slc_new, state_win_kv,
                           win_new, tb_s, wb_s, b_q_norm[l], k_norm, n_kv, n_slc_s)
        xp, xs = _matmul(og, pad_rows(os_.reshape(bs, hq).astype(BF16)), b_w_out, layer=l,
                         res=xp, res_s=xs)

    return (xp.reshape(batch, seq, d_model), xs[:bs].reshape(bs, 1, d_model),
            p_cmp_kv, p_slc_kv, p_win_kv, jnp.stack(p_h), jnp.stack(p_c),
            s_cmp_kv, s_slc_kv, s_win_kv, jnp.stack(s_h), jnp.stack(s_c))
```

```python
import functools
import math

import numpy as np
import jax
import jax.numpy as jnp
from jax import lax
from jax.experimental import pallas as pl
from jax.experimental.pallas import tpu as pltpu

F32 = jnp.float32
BF16 = jnp.bfloat16

EPS = 1e-6
NEG = -1e30
BIG = 1e30
LOWEST = -3e38
LRU_C = 8.0
CONV_W = 4
CMP_BLK = 32
CMP_STRIDE = 16
SLC_BLK = 64
TOP_N = 16
WINDOW = 512
N_BUCKETS = 32
MAX_DIST = 128
PAGE_SIZE = 128
QB = 128
LANES = 128
SUBLANES = 8
SEG_PAD = 8
SAMPLE_ROWS = 16
VMEM_LIMIT = 48 * 1024 * 1024


def _cparams(*sem):
    return pltpu.CompilerParams(dimension_semantics=sem, vmem_limit_bytes=VMEM_LIMIT)


def _dot(a, b):
    return jnp.dot(a, b, preferred_element_type=F32)


def _dot_nt(a, b):
    return lax.dot_general(a, b, (((1,), (1,)), ((), ())), preferred_element_type=F32)


def _rms(x, g):
    return x * lax.rsqrt(jnp.mean(x * x, axis=-1, keepdims=True) + EPS) * g


def _sigmoid(x):
    return jax.nn.sigmoid(x)


def _silu(x):
    return x * jax.nn.sigmoid(x)


def _norm_kernel(x_ref, g_ref, o_ref):
    o_ref[...] = _rms(x_ref[...], g_ref[...]).astype(o_ref.dtype)


def _norm_cast(x, g):
    m, d = x.shape
    tm = min(m, 512)
    return pl.pallas_call(
        _norm_kernel,
        grid=(m // tm,),
        in_specs=[pl.BlockSpec((tm, d), lambda i: (i, 0)),
                  pl.BlockSpec((1, d), lambda i: (0, 0))],
        out_specs=pl.BlockSpec((tm, d), lambda i: (i, 0)),
        out_shape=jax.ShapeDtypeStruct((m, d), BF16),
        compiler_params=_cparams("parallel"),
        name="norm_cast",
    )(x, g.reshape(1, d))


def _mm_kernel(a_ref, as_ref, w_ref, *rest, has_res, n_rows5):
    if has_res:
        r_ref, rs_ref, o_ref, os_ref = rest[:4]
    else:
        o_ref, os_ref = rest[:2]
    wb_scr = rest[-1]
    rows5_refs = rest[-1 - n_rows5:-1]

    @pl.when(pl.program_id(1) == 0)
    def _():
        wb_scr[...] = w_ref[...].astype(BF16)
        acc_s = _dot(as_ref[...], wb_scr[...])
        os_ref[...] = rs_ref[...] + acc_s if has_res else acc_s

    acc = _dot(a_ref[...], wb_scr[...])
    o_ref[...] = r_ref[...] + acc if has_res else acc

    for br, r5_ref in enumerate(rows5_refs):
        @pl.when(pl.program_id(0) // 2 == br)
        def _(r5_ref=r5_ref):
            n_grp, hd = r5_ref.shape[3], r5_ref.shape[4]
            for g in range(n_grp):
                r5_ref[0, :, 0, g, :] = acc[:, g * hd:(g + 1) * hd]


def _matmul(a, a_s, w, layer=0, n=None, res=None, res_s=None, tm=1024, tn=512, rows5=None):
    m, k = a.shape
    ms = a_s.shape[0]
    n = w.shape[2] if n is None else n
    tm = min(m, tm)
    tn = min(n, tn)
    has_res = res is not None
    in_specs = [pl.BlockSpec((tm, k), lambda j, i: (i, 0)),
                pl.BlockSpec((ms, k), lambda j, i: (0, 0)),
                pl.BlockSpec((None, k, tn), lambda j, i: (layer, 0, j))]
    args = [a, a_s, w]
    if has_res:
        in_specs += [pl.BlockSpec((tm, tn), lambda j, i: (i, j)),
                     pl.BlockSpec((ms, tn), lambda j, i: (0, j))]
        args += [res, res_s]
    out_specs = [pl.BlockSpec((tm, tn), lambda j, i: (i, j)),
                 pl.BlockSpec((ms, tn), lambda j, i: (0, j))]
    out_shape = [jax.ShapeDtypeStruct((m, n), F32), jax.ShapeDtypeStruct((ms, n), F32)]
    n_rows5 = 0
    if rows5 is not None:
        batch, seq, n_kv, hd = rows5
        n_rows5 = n // (2 * tn)
        nt = seq // tm
        assert tn == n_kv * hd and seq % tm == 0 and n % (2 * tn) == 0

        def rows5_map(br):
            def index_map(j, i):
                active, before = j // 2 == br, j < 2 * br
                park = lambda first, last: jnp.where(before, first, last)
                return (jnp.where(active, i // nt, park(0, batch - 1)),
                        jnp.where(active, i % nt, park(0, nt - 1)),
                        jnp.where(active, j % 2, park(0, 1)), 0, 0)
            return index_map

        for br in range(n_rows5):
            out_specs.append(pl.BlockSpec((1, tm, 1, n_kv, hd), rows5_map(br)))
            out_shape.append(jax.ShapeDtypeStruct((batch, seq, 2, n_kv, hd), F32))
    return pl.pallas_call(
        functools.partial(_mm_kernel, has_res=has_res, n_rows5=n_rows5),
        grid=(n // tn, m // tm),
        in_specs=in_specs,
        out_specs=out_specs,
        out_shape=out_shape,
        scratch_shapes=[pltpu.VMEM((k, tn), BF16)],
        compiler_params=_cparams("arbitrary", "arbitrary"),
        name="matmul",
    )(*args)


def _lru_coeffs(xc, wrg_ref, wig_ref, brg, big, lam, pos0_row):
    rows, cb = xc.shape
    bs = wrg_ref.shape[-1]
    xcb = xc.astype(BF16)
    r_parts, i_parts = [], []
    for n in range(cb // bs):
        xs = xcb[:, n * bs:(n + 1) * bs]
        r_parts.append(_dot(xs, wrg_ref[n]))
        i_parts.append(_dot(xs, wig_ref[n]))
    r = _sigmoid(jnp.concatenate(r_parts, axis=1) + brg)
    i = _sigmoid(jnp.concatenate(i_parts, axis=1) + big)
    nl = -lam
    softplus = jnp.maximum(nl, 0.0) + jnp.log1p(jnp.exp(-jnp.abs(nl)))
    log_a = -LRU_C * r * softplus
    a = jnp.exp(log_a)
    mult = jnp.sqrt(-jnp.tanh(log_a) * (a * a + 1.0))
    if pos0_row is not None:
        row = lax.broadcasted_iota(jnp.int32, (rows, cb), 0)
        mult = jnp.where(row == pos0_row, 1.0, mult)
    return a, mult * i * xc


def _lru_prompt_kernel(xb_ref, gate_ref, cw_ref, cb_ref, wrg_ref, brg_ref, wig_ref, big_ref,
                       lam_ref, h0_ref, c0_ref, g_ref, hl_ref, ct_ref, st_scr, un_scr, tail_scr, h_scr):
    tc = pl.program_id(2)
    tt, cb = xb_ref.shape
    nlb = cb // LANES
    seg = tt // SUBLANES
    tail = (CONV_W - 1) * SUBLANES

    @pl.when(tc == 0)
    def _():
        h_scr[...] = jnp.broadcast_to(h0_ref[0], (SUBLANES, cb))
        for i in range(CONV_W - 1):
            tail_scr[i * SUBLANES:(i + 1) * SUBLANES, :] = jnp.broadcast_to(c0_ref[0, i:i + 1, :],
                                                                             (SUBLANES, cb))

    pitch = seg + SEG_PAD

    def to_segments(src_ref, slot):
        cols = []
        for lb in range(nlb):
            for j in range(SUBLANES):
                st_scr[slot, lb, j * pitch:j * pitch + seg, :] = src_ref[j * seg:(j + 1) * seg,
                                                                         lb * LANES:(lb + 1) * LANES]
            cols.append(jnp.concatenate(
                [st_scr[slot, lb, pl.ds(k, SUBLANES, stride=pitch), :] for k in range(seg)], axis=0))
        return jnp.concatenate(cols, axis=1)

    x = to_segments(xb_ref, 0)

    sub = lax.broadcasted_iota(jnp.int32, (SUBLANES, cb), 0)
    heads = []
    for i in range(CONV_W - 1):
        cur = x[tt - tail + i * SUBLANES:tt - tail + (i + 1) * SUBLANES]
        prev = tail_scr[i * SUBLANES:(i + 1) * SUBLANES, :]
        heads.append(pltpu.roll(jnp.where(sub == SUBLANES - 1, prev, cur), 1, axis=0))
    tail_scr[...] = x[tt - tail:tt]
    w = cw_ref[...]
    xc = cb_ref[...]
    for k in range(CONV_W - 1):
        m = CONV_W - 1 - k
        xc = xc + w[k:k + 1] * jnp.concatenate(heads[CONV_W - 1 - m:] + [x[0:tt - m * SUBLANES]], axis=0)
    xc = xc + w[CONV_W - 1:CONV_W] * x

    a, b_in = _lru_coeffs(xc, wrg_ref, wig_ref, brg_ref[...], big_ref[...], lam_ref[...],
                           jnp.where(tc == 0, 0, -1))

    blk = lambda v, k: v[k * SUBLANES:(k + 1) * SUBLANES]
    e, p = blk(b_in, 0), blk(a, 0)
    for k in range(1, seg):
        e = blk(a, k) * e + blk(b_in, k)
        p = blk(a, k) * p
    carry = h_scr[0:1, :]
    carries = [carry]
    for j in range(SUBLANES - 1):
        carry = p[j:j + 1] * carry + e[j:j + 1]
        carries.append(carry)
    h = jnp.concatenate(carries, axis=0)
    hs = []
    for k in range(seg):
        h = blk(a, k) * h + blk(b_in, k)
        hs.append(h)
    h_scr[...] = jnp.broadcast_to(h[SUBLANES - 1:SUBLANES], (SUBLANES, cb))
    g = jnp.concatenate(hs, axis=0) * _silu(to_segments(gate_ref, 1))

    for lb in range(nlb):
        for k in range(seg):
            un_scr[lb, pl.ds(k, SUBLANES, stride=pitch), :] = g[k * SUBLANES:(k + 1) * SUBLANES,
                                                                lb * LANES:(lb + 1) * LANES]
        for j in range(SUBLANES):
            g_ref[j * seg:(j + 1) * seg, lb * LANES:(lb + 1) * LANES] = (
                un_scr[lb, j * pitch:j * pitch + seg, :].astype(g_ref.dtype))

    @pl.when(tc == pl.num_programs(2) - 1)
    def _():
        hl_ref[0] = h[SUBLANES - 1:SUBLANES]
        ct_ref[0] = xb_ref[tt - (CONV_W - 1):tt, :]


def _lru_prompt(u, batch, seq, cw, cb_, wrg, brg, wig, big, lam, h0, c0):
    m, c2 = u.shape
    c = c2 // 2
    cb = min(c, 1024)
    tt = min(seq, 256)
    nt = seq // tt
    ncb = c // cb
    bs = wrg.shape[-1]
    nb = cb // bs
    vec = lambda: pl.BlockSpec((1, cb), lambda b, j, t: (0, j))
    return pl.pallas_call(
        _lru_prompt_kernel,
        grid=(batch, ncb, nt),
        in_specs=[
            pl.BlockSpec((tt, cb), lambda b, j, t: (b * nt + t, j)),
            pl.BlockSpec((tt, cb), lambda b, j, t: (b * nt + t, ncb + j)),
            pl.BlockSpec((CONV_W, cb), lambda b, j, t: (0, j)),
            vec(),
            pl.BlockSpec((nb, bs, bs), lambda b, j, t: (j, 0, 0)),
            vec(),
            pl.BlockSpec((nb, bs, bs), lambda b, j, t: (j, 0, 0)),
            vec(),
            vec(),
            pl.BlockSpec((1, 1, cb), lambda b, j, t: (b, 0, j)),
            pl.BlockSpec((1, CONV_W - 1, cb), lambda b, j, t: (b, 0, j)),
        ],
        out_specs=[
            pl.BlockSpec((tt, cb), lambda b, j, t: (b * nt + t, j)),
            pl.BlockSpec((1, 1, cb), lambda b, j, t: (b, 0, j)),
            pl.BlockSpec((1, CONV_W - 1, cb), lambda b, j, t: (b, 0, j)),
        ],
        out_shape=[
            jax.ShapeDtypeStruct((m, c), BF16),
            jax.ShapeDtypeStruct((batch, 1, c), F32),
            jax.ShapeDtypeStruct((batch, CONV_W - 1, c), F32),
        ],
        scratch_shapes=[pltpu.VMEM((2, cb // LANES, tt + SUBLANES * SEG_PAD, LANES), F32),
                        pltpu.VMEM((cb // LANES, tt + SUBLANES * SEG_PAD, LANES), F32),
                        pltpu.VMEM(((CONV_W - 1) * SUBLANES, cb), F32),
                        pltpu.VMEM((SUBLANES, cb), F32)],
        compiler_params=_cparams("parallel", "parallel", "arbitrary"),
        name="lru_prompt",
    )(u, u, cw, cb_.reshape(1, c), wrg, brg.reshape(1, c), wig, big.reshape(1, c),
      lam.reshape(1, c), h0.reshape(batch, 1, c), c0)


def _lru_sample_kernel(xb_ref, gate_ref, cw_ref, cb_ref, wrg_ref, brg_ref, wig_ref, big_ref,
                       lam_ref, h0_ref, c0_ref, g_ref, h_ref):
    w = cw_ref[...]
    xc = cb_ref[...]
    for k in range(CONV_W - 1):
        xc = xc + w[k:k + 1] * c0_ref[k]
    xc = xc + w[CONV_W - 1:CONV_W] * xb_ref[...]
    a, b_in = _lru_coeffs(xc, wrg_ref, wig_ref, brg_ref[...], big_ref[...], lam_ref[...], None)
    h = a * h0_ref[...] + b_in
    h_ref[...] = h
    g_ref[...] = (h * _silu(gate_ref[...])).astype(g_ref.dtype)


def _lru_sample(u, cw, cb_, wrg, brg, wig, big, lam, h0, c0):
    rows, c2 = u.shape
    c = c2 // 2
    cb = min(c, 1024)
    ncb = c // cb
    bs = wrg.shape[-1]
    nb = cb // bs
    vec = lambda: pl.BlockSpec((1, cb), lambda j: (0, j))
    return pl.pallas_call(
        _lru_sample_kernel,
        grid=(ncb,),
        in_specs=[
            pl.BlockSpec((rows, cb), lambda j: (0, j)),
            pl.BlockSpec((rows, cb), lambda j: (0, ncb + j)),
            pl.BlockSpec((CONV_W, cb), lambda j: (0, j)),
            vec(),
            pl.BlockSpec((nb, bs, bs), lambda j: (j, 0, 0)),
            vec(),
            pl.BlockSpec((nb, bs, bs), lambda j: (j, 0, 0)),
            vec(),
            vec(),
            pl.BlockSpec((rows, cb), lambda j: (0, j)),
            pl.BlockSpec((CONV_W - 1, rows, cb), lambda j: (0, 0, j)),
        ],
        out_specs=[pl.BlockSpec((rows, cb), lambda j: (0, j)),
                   pl.BlockSpec((rows, cb), lambda j: (0, j))],
        out_shape=[jax.ShapeDtypeStruct((rows, c), BF16),
                   jax.ShapeDtypeStruct((rows, c), F32)],
        compiler_params=_cparams("parallel"),
        name="lru_sample",
    )(u, u, cw, cb_.reshape(1, c), wrg, brg.reshape(1, c), wig, big.reshape(1, c),
      lam.reshape(1, c), h0, c0)


GATHER_PAGES = 4


def _page_gather_kernel(pt_ref, *refs):
    del pt_ref
    c_refs, o_ref, slab_scr = refs[:GATHER_PAGES], refs[GATHER_PAGES], refs[GATHER_PAGES + 1]
    n_kv, hd = c_refs[0].shape[3], c_refs[0].shape[4]
    rows = PAGE_SIZE // CMP_STRIDE
    for q, c_ref in enumerate(c_refs):
        for sg in range(2 * n_kv):
            slab_scr[q, sg] = c_ref[0, :, sg // n_kv, sg % n_kv, :]
            for l in range(CMP_STRIDE):
                o_ref[0, sg, q * rows:(q + 1) * rows, l * hd:(l + 1) * hd] = (
                    slab_scr[q, sg, pl.ds(l, rows, stride=CMP_STRIDE), :])


def _page_gather(pages, page_table):
    bs, n_pages = page_table.shape
    n_kv, hd = pages.shape[3], pages.shape[4]
    n_sg = 2 * n_kv
    rows = PAGE_SIZE // CMP_STRIDE
    assert n_pages % GATHER_PAGES == 0

    def page_map(q):
        return lambda b, j, pt: (pt[b * n_pages + j * GATHER_PAGES + q], 0, 0, 0, 0)

    return pl.pallas_call(
        _page_gather_kernel,
        grid_spec=pltpu.PrefetchScalarGridSpec(
            num_scalar_prefetch=1,
            grid=(bs, n_pages // GATHER_PAGES),
            in_specs=[pl.BlockSpec((1, PAGE_SIZE, 2, n_kv, hd), page_map(q))
                      for q in range(GATHER_PAGES)],
            out_specs=pl.BlockSpec((1, n_sg, GATHER_PAGES * rows, CMP_STRIDE * hd),
                                   lambda b, j, pt: (b, 0, j, 0)),
            scratch_shapes=[pltpu.VMEM((GATHER_PAGES, n_sg, PAGE_SIZE, hd), F32)],
        ),
        out_shape=jax.ShapeDtypeStruct((bs, n_sg, n_pages * rows, CMP_STRIDE * hd), F32),
        compiler_params=_cparams("parallel", "arbitrary"),
        name="page_gather",
    )(page_table.reshape(-1), *([pages] * GATHER_PAGES))


def _compress_kernel(r_ref, pos_ref, w1_ref, w2_ref, kn_ref, o_ref, q_scr, *, n_kv):
    sg = pl.program_id(1)
    rows = r_ref[0, 0]
    nr = rows.shape[0]
    hid = w1_ref.shape[2] // 2
    xa = (rows + pos_ref[0, 0:1]).astype(BF16)
    xb = (rows + pos_ref[0, 1:2]).astype(BF16)
    w1 = w1_ref[0]
    p = _dot(xa, w1[:, :hid])
    q_scr[0:nr, :] = _dot(xb, w1[:, hid:])
    q_scr[nr:nr + 8, :] = jnp.zeros((8, hid), F32)
    pre = p + q_scr[1:nr + 1, :]
    comp = _dot(_silu(pre).astype(BF16), w2_ref[0])
    normed = _rms(comp, kn_ref[...])
    o_ref[0, 0] = jnp.where(sg < n_kv, normed, comp).astype(o_ref.dtype)


def _compress(r16, posab, w1ab, w2, k_norm0, n_kv):
    b, n_sg, nr, kk = r16.shape
    hid2 = w1ab.shape[2]
    hd = w2.shape[2]
    return pl.pallas_call(
        functools.partial(_compress_kernel, n_kv=n_kv),
        grid=(b, n_sg),
        in_specs=[
            pl.BlockSpec((1, 1, nr, kk), lambda i, s: (i, s, 0, 0)),
            pl.BlockSpec((1, 2, kk), lambda i, s: (s // n_kv, 0, 0)),
            pl.BlockSpec((1, kk, hid2), lambda i, s: (s // n_kv, 0, 0)),
            pl.BlockSpec((1, hid2 // 2, hd), lambda i, s: (s // n_kv, 0, 0)),
            pl.BlockSpec((1, hd), lambda i, s: (0, 0)),
        ],
        out_specs=pl.BlockSpec((1, 1, nr, hd), lambda i, s: (i, s, 0, 0)),
        out_shape=jax.ShapeDtypeStruct((b, n_sg, nr, hd), BF16),
        scratch_shapes=[pltpu.VMEM((nr + 8, hid2 // 2), F32)],
        compiler_params=_cparams("parallel", "arbitrary"),
        name="compress",
    )(r16, posab, w1ab, w2, k_norm0.reshape(1, hd))


def _attn_prompt_kernel(uq_ref, ug_ref, bg_ref, gb_ref, ks_ref, vs_ref, kw_ref, vw_ref,
                        kc_ref, vc_ref, wb_ref, tc_ref, mit_ref, ex_ref, qn_ref, kn_ref, og_ref,
                        ks_scr, vs_scr, kw_scr, vw_scr, madd_scr, lg_scr, p_scr, m_scr, acc_scr,
                        *, n_heads_grp, n_slc):
    r_ = n_heads_grp
    c = pl.program_id(2)
    t0 = c * QB
    seq, hd = ks_ref.shape
    rq = r_ * QB
    wk = WINDOW + QB
    n_wt = wk // QB

    @pl.when(c == 0)
    def _():
        ks_scr[...] = _rms(ks_ref[...], kn_ref[1:2]).astype(BF16)
        vs_scr[:, 0:hd] = vs_ref[...].astype(BF16)
        vs_scr[:, hd:2 * hd] = jnp.ones((seq, hd), BF16)
        kw_scr[0:WINDOW, :] = jnp.zeros((WINDOW, hd), BF16)
        vw_scr[0:WINDOW, :] = jnp.zeros((WINDOW, 2 * hd), BF16)
        kw_scr[WINDOW:WINDOW + seq, :] = _rms(kw_ref[...], kn_ref[2:3]).astype(BF16)
        vw_scr[WINDOW:WINDOW + seq, 0:hd] = vw_ref[...].astype(BF16)
        vw_scr[WINDOW:WINDOW + seq, hd:2 * hd] = jnp.ones((seq, hd), BF16)
        p_scr[...] = jnp.zeros(p_scr.shape, BF16)

    uq = uq_ref[...]
    qg = qn_ref[...] * hd ** -0.5
    qb = jnp.concatenate(
        [uq[:, r * hd:(r + 1) * hd]
         * lax.rsqrt(jnp.mean(uq[:, r * hd:(r + 1) * hd] ** 2, axis=-1, keepdims=True) + EPS) * qg
         for r in range(r_)], axis=0).astype(BF16)

    n_c = kc_ref.shape[2]
    bias_c = pltpu.roll(tc_ref[...], (c * (QB // CMP_STRIDE) + QB // CMP_STRIDE) % n_c, axis=2)
    lc = _dot_nt(qb, kc_ref[0, 0]).reshape(r_, QB, n_c) + bias_c
    tq = t0 + lax.broadcasted_iota(jnp.int32, (QB, n_c), 0)
    nn = lax.broadcasted_iota(jnp.int32, (QB, n_c), 1)
    c_ok = (tq - (nn * CMP_STRIDE + (CMP_BLK - 1)) >= 0)[None]
    lc = jnp.where(c_ok, lc, NEG)
    e = jnp.exp(lc - jnp.max(lc, axis=-1, keepdims=True))
    pc = jnp.where(c_ok, e / jnp.sum(e, axis=-1, keepdims=True), 0.0)
    pcb = pc.reshape(rq, n_c).astype(BF16)
    o_cmp = _dot(pcb, vc_ref[0, 0])
    imp_all = _dot_nt(mit_ref[...], pcb)
    imp = imp_all[:, 0:QB]
    for r in range(1, r_):
        imp = imp + imp_all[:, r * QB:(r + 1) * QB]

    jj = lax.broadcasted_iota(jnp.int32, (n_slc, QB), 0)
    cur = (t0 + lax.broadcasted_iota(jnp.int32, (n_slc, QB), 1)) // SLC_BLK
    forced = (jj == 0) | (jj == cur) | (jj == cur - 1)
    allowed = jj <= cur
    score = jnp.where(forced, BIG, jnp.where(allowed, imp, NEG))
    rank = jnp.zeros((n_slc, QB), F32)
    for k in range(n_slc):
        sk = score[k:k + 1, :]
        rank = rank + jnp.where(sk > score, 1.0, jnp.where((sk == score) & (k < jj), 1.0, 0.0))
    sel = jnp.where((rank < min(TOP_N, n_slc)) & allowed, 1.0, 0.0).astype(BF16)
    selx = lax.dot_general(sel, ex_ref[...], (((0,), (0,)), ((), ())),
                           preferred_element_type=F32)
    tk = 2 * QB
    for kp in range(seq // tk):
        madd_scr[kp] = (selx[:, kp * tk:(kp + 1) * tk] - 1.0) * BIG

    m_scr[...] = jnp.full((rq, tk), NEG, F32)
    near1 = wb_ref[:, :, WINDOW - QB:WINDOW]
    near0 = wb_ref[:, :, WINDOW:WINDOW + QB]

    def logits_pair(kp, bias_lo, bias_hi):
        k0 = pl.multiple_of(kp * tk, tk)
        s3 = _dot_nt(qb, ks_scr[pl.ds(k0, tk), :]).reshape(r_, QB, tk) + madd_scr[kp][None]
        if bias_lo is not None or bias_hi is not None:
            lo, hi = s3[:, :, 0:QB], s3[:, :, QB:tk]
            lo = lo if bias_lo is None else lo + bias_lo
            hi = hi if bias_hi is None else hi + bias_hi
            s3 = jnp.concatenate([lo, hi], axis=2)
        s = s3.reshape(rq, tk)
        lg_scr[kp] = s
        m_scr[...] = jnp.maximum(m_scr[...], s)

    def far_body(kp, carry):
        logits_pair(kp, None, None)
        return carry

    last = c // 2
    c_odd = c % 2 == 1
    lax.fori_loop(0, jnp.where(c_odd, last, jnp.maximum(last - 1, 0)), far_body, 0)

    @pl.when(c_odd)
    def _():
        logits_pair(last, near1, near0)

    @pl.when(jnp.logical_not(c_odd) & (c >= 2))
    def _():
        logits_pair(last - 1, None, near1)

    @pl.when(jnp.logical_not(c_odd))
    def _():
        logits_pair(last, near0, None)

    m_scr[...] = jnp.broadcast_to(jnp.max(m_scr[...], axis=-1, keepdims=True), (rq, tk))

    def prob_body(kp, carry):
        k0 = pl.multiple_of(kp * tk, tk)
        p_scr[:, pl.ds(k0, tk)] = jnp.exp(lg_scr[kp] - m_scr[...]).astype(BF16)
        return carry

    lax.fori_loop(0, last + 1, prob_body, 0)
    n_pv = 4
    for v in range(n_pv):
        @pl.when(c * n_pv // (seq // QB) == v)
        def _(v=v):
            keys = (v + 1) * seq // n_pv
            acc_scr[...] = _dot(p_scr[:, 0:keys], vs_scr[0:keys, :])
    o_slc = acc_scr[:, 0:hd] / acc_scr[:, hd:2 * hd]

    w0 = pl.multiple_of(t0, QB)
    sw = _dot_nt(qb, kw_scr[pl.ds(w0, wk), :])
    tiles = []
    for j in range(n_wt):
        s = sw[:, j * QB:(j + 1) * QB]
        if j == 0 or j >= n_wt - 2:
            s = s + wb_ref[:, :, j * QB:(j + 1) * QB].reshape(rq, QB)
        if j < n_wt - 1:
            s = s + jnp.where(c + j >= n_wt - 1, 0.0, NEG)
        tiles.append(s)
    mw = tiles[0]
    for s in tiles[1:]:
        mw = jnp.maximum(mw, s)
    mw = jnp.max(mw, axis=-1, keepdims=True)
    pw = jnp.concatenate([jnp.exp(s - mw).astype(BF16) for s in tiles], axis=1)
    accw = _dot(pw, vw_scr[pl.ds(w0, wk), :])
    o_win = accw[:, 0:hd] / accw[:, hd:2 * hd]

    bg = _sigmoid(bg_ref[...] + gb_ref[0])
    ug = ug_ref[...]
    for r in range(r_):
        rows = slice(r * QB, (r + 1) * QB)
        o = (bg[:, 3 * r:3 * r + 1] * o_cmp[rows] + bg[:, 3 * r + 1:3 * r + 2] * o_slc[rows]
             + bg[:, 3 * r + 2:3 * r + 3] * o_win[rows])
        og_ref[:, r * hd:(r + 1) * hd] = (o * _silu(ug[:, r * hd:(r + 1) * hd])).astype(og_ref.dtype)


def _attn_prompt(u, bgs, gbias, kv, comp, wbias, tcfix, mimp_t, expand, q_norm, k_norm,
                 batch, seq, n_kv, n_heads):
    m = u.shape[0]
    hd = k_norm.shape[1]
    r_ = n_heads // n_kv
    hq = n_heads * hd
    nq = seq // QB
    n_c = comp.shape[2]
    n_slc = seq // SLC_BLK
    rq = r_ * QB
    wk = WINDOW + QB
    n_sg = 2 * n_kv
    kv_spec = lambda col0: pl.BlockSpec((seq, hd), lambda b, g, c: (b, col0 + g))
    return pl.pallas_call(
        functools.partial(_attn_prompt_kernel, n_heads_grp=r_, n_slc=n_slc),
        grid=(batch, n_kv, nq),
        in_specs=[
            pl.BlockSpec((QB, r_ * hd), lambda b, g, c: (b * nq + c, g)),
            pl.BlockSpec((QB, r_ * hd), lambda b, g, c: (b * nq + c, n_kv + g)),
            pl.BlockSpec((QB, LANES), lambda b, g, c: (b * nq + c, g)),
            pl.BlockSpec((1, 1, LANES), lambda b, g, c: (g, 0, 0)),
            kv_spec(n_sg), kv_spec(n_sg + n_kv), kv_spec(2 * n_sg), kv_spec(2 * n_sg + n_kv),
            pl.BlockSpec((1, 1, n_c, hd), lambda b, g, c: (b, g, 0, 0)),
            pl.BlockSpec((1, 1, n_c, hd), lambda b, g, c: (b, n_kv + g, 0, 0)),
            pl.BlockSpec((r_, QB, wk), lambda b, g, c: (g, 0, 0)),
            pl.BlockSpec((r_, QB, n_c), lambda b, g, c: (g, 0, 0)),
            pl.BlockSpec((n_slc, n_c), lambda b, g, c: (0, 0)),
            pl.BlockSpec((n_slc, seq), lambda b, g, c: (0, 0)),
            pl.BlockSpec((1, hd), lambda b, g, c: (0, 0)),
            pl.BlockSpec((3, hd), lambda b, g, c: (0, 0)),
        ],
        out_specs=pl.BlockSpec((QB, r_ * hd), lambda b, g, c: (b * nq + c, g)),
        out_shape=jax.ShapeDtypeStruct((m, hq), BF16),
        scratch_shapes=[
            pltpu.VMEM((seq, hd), BF16), pltpu.VMEM((seq, 2 * hd), BF16),
            pltpu.VMEM((seq + WINDOW, hd), BF16), pltpu.VMEM((seq + WINDOW, 2 * hd), BF16),
            pltpu.VMEM((seq // (2 * QB), QB, 2 * QB), F32),
            pltpu.VMEM((seq // (2 * QB), rq, 2 * QB), F32),
            pltpu.VMEM((rq, seq), BF16),
            pltpu.VMEM((rq, 2 * QB), F32),
            pltpu.VMEM((rq, 2 * hd), F32),
        ],
        compiler_params=_cparams("parallel", "parallel", "arbitrary"),
        name="attn_prompt",
    )(u, u, bgs, gbias, kv, kv, kv, kv, comp, comp, wbias, tcfix, mimp_t, expand,
      q_norm.reshape(1, hd), k_norm)


def _attn_sample_cmp_kernel(q_ref, kc_ref, vc_ref, cb_ref, mi_ref, qn_ref, oc_ref, sel_ref,
                            *, n_slc):
    r_, hd = q_ref.shape[1], q_ref.shape[2]
    qb = (_rms(q_ref[0], qn_ref[...]) * hd ** -0.5).astype(BF16)
    lc = _dot_nt(qb, kc_ref[0, 0]) + cb_ref[0]
    e = jnp.exp(lc - jnp.max(lc, axis=-1, keepdims=True))
    pcb = (e / jnp.sum(e, axis=-1, keepdims=True)).astype(BF16)
    oc_ref[0] = _dot(pcb, vc_ref[0, 0])
    nsp = mi_ref.shape[1]
    imp = jnp.sum(_dot(pcb, mi_ref[...]), axis=0, keepdims=True)

    jl = lax.broadcasted_iota(jnp.int32, (1, nsp), 1)
    forced = (jl == 0) | (jl == n_slc - 1) | (jl == n_slc - 2)
    score = jnp.where(jl < n_slc, jnp.where(forced, BIG, imp), LOWEST)
    s_rows = jnp.broadcast_to(score, (nsp, nsp))
    kk = lax.broadcasted_iota(jnp.int32, (nsp, nsp), 0)
    jj = lax.broadcasted_iota(jnp.int32, (nsp, nsp), 1)
    s_col = jnp.sum(jnp.where(kk == jj, s_rows, 0.0), axis=1, keepdims=True)
    ahead = jnp.where(s_col > s_rows, 1.0, jnp.where((s_col == s_rows) & (kk < jj), 1.0, 0.0))
    rank = jnp.sum(ahead, axis=0, keepdims=True)
    lane = lax.broadcasted_iota(jnp.int32, (1, LANES), 1)
    out = jnp.zeros((1, LANES), F32)
    jf = jl.astype(F32)
    for k in range(min(TOP_N, n_slc)):
        idx_k = jnp.sum(jnp.where(rank == k, jf, 0.0), axis=1, keepdims=True)
        out = out + jnp.where(lane == k, idx_k, 0.0)
    sel_ref[0, 0] = jnp.broadcast_to(out, (8, LANES)).astype(jnp.int32)


def _attn_sample_cmp(q3, comp, cbias, mimp, q_norm, n_kv, n_slc):
    bs, n_heads, hd = q3.shape
    r_ = n_heads // n_kv
    nr = comp.shape[2]
    nsp = mimp.shape[1]
    return pl.pallas_call(
        functools.partial(_attn_sample_cmp_kernel, n_slc=n_slc),
        grid=(bs, n_kv),
        in_specs=[
            pl.BlockSpec((1, r_, hd), lambda b, g: (b, g, 0)),
            pl.BlockSpec((1, 1, nr, hd), lambda b, g: (b, g, 0, 0)),
            pl.BlockSpec((1, 1, nr, hd), lambda b, g: (b, n_kv + g, 0, 0)),
            pl.BlockSpec((1, r_, nr), lambda b, g: (g, 0, 0)),
            pl.BlockSpec((nr, nsp), lambda b, g: (0, 0)),
            pl.BlockSpec((1, hd), lambda b, g: (0, 0)),
        ],
        out_specs=[pl.BlockSpec((1, r_, hd), lambda b, g: (b, g, 0)),
                   pl.BlockSpec((1, 1, 8, LANES), lambda b, g: (b, g, 0, 0))],
        out_shape=[jax.ShapeDtypeStruct((bs, n_heads, hd), F32),
                   jax.ShapeDtypeStruct((bs, n_kv, 8, LANES), jnp.int32)],
        compiler_params=_cparams("parallel", "parallel"),
        name="attn_sample_cmp",
    )(q3, comp, comp, cbias, mimp, q_norm.reshape(1, hd))


def _attn_sample_kernel(sel_ref, pt_ref, q_ref, gate_ref, bg_ref, gb_ref, oc_ref, ksn_ref, vsn_ref,
                        kw_ref, vw_ref, kwn_ref, vwn_ref, tb_ref, wb_ref, qn_ref, kn_ref, *rest,
                        n_sel, n_slc, n_kv):
    del pt_ref
    k_refs, v_refs, o_ref = rest[:n_sel], rest[n_sel:2 * n_sel], rest[2 * n_sel]
    b, g = pl.program_id(0), pl.program_id(1)
    r_, hd = q_ref.shape[1], q_ref.shape[2]
    qb = (_rms(q_ref[0], qn_ref[...]) * hd ** -0.5).astype(BF16)

    def own_group(ref):
        rows = ref[0, :, 0, 0, :]
        for gi in range(1, n_kv):
            rows = jnp.where(g == gi, ref[0, :, 0, gi, :], rows)
        return rows

    k_new = ksn_ref[0, 0]
    v_new = vsn_ref[0, 0]
    logits, values = [], []
    for k in range(n_sel):
        idx = sel_ref[(b * n_kv + g) * TOP_N + k]
        is_new = idx == n_slc - 1
        kt = jnp.where(is_new, jnp.broadcast_to(k_new, (SLC_BLK, hd)), own_group(k_refs[k]))
        vt = jnp.where(is_new, jnp.broadcast_to(v_new, (SLC_BLK, hd)), own_group(v_refs[k]))
        logits.append(_dot_nt(qb, _rms(kt, kn_ref[1:2]).astype(BF16)) + tb_ref[0, idx])
        values.append(vt.astype(BF16))
    m = logits[0].max(axis=-1, keepdims=True)
    for s in logits[1:]:
        m = jnp.maximum(m, s.max(axis=-1, keepdims=True))
    es = [jnp.exp(s - m) for s in logits]
    den = es[0].sum(axis=-1, keepdims=True)
    for e in es[1:]:
        den = den + e.sum(axis=-1, keepdims=True)
    o_slc = jnp.zeros((r_, hd), F32)
    for e, v in zip(es, values):
        o_slc = o_slc + _dot((e / den).astype(BF16), v)

    n_w = kw_ref.shape[1]
    lw = _dot_nt(qb, _rms(own_group(kw_ref), kn_ref[2:3]).astype(BF16)) + wb_ref[0, :, 0:n_w]
    kwn = _rms(kwn_ref[0, 0], kn_ref[2:3]).astype(BF16).astype(F32)
    l_new = jnp.sum(qb.astype(F32) * kwn, axis=-1, keepdims=True) + wb_ref[0, :, n_w:n_w + 1]
    mw = jnp.maximum(lw.max(axis=-1, keepdims=True), l_new)
    ew, e_new = jnp.exp(lw - mw), jnp.exp(l_new - mw)
    dw = ew.sum(axis=-1, keepdims=True) + e_new
    o_win = (_dot((ew / dw).astype(BF16), own_group(vw_ref).astype(BF16))
             + (e_new / dw).astype(BF16).astype(F32) * vwn_ref[0, 0].astype(BF16).astype(F32))

    bg = _sigmoid(bg_ref[0] + gb_ref[...])
    o = bg[:, 0:1] * oc_ref[0] + bg[:, 1:2] * o_slc + bg[:, 2:3] * o_win
    o_ref[0] = o * _silu(gate_ref[0])


def _attn_sample(sel, page_table, q3, gate3, bg3, gbias2, o_cmp, cache_slc, slc_new, win_state,
                 win_new, tb, wb, q_norm, k_norm, n_kv, n_slc):
    bs, n_heads, hd = q3.shape
    r_ = n_heads // n_kv
    n_pages = page_table.shape[1]
    n_pool = cache_slc.shape[0]
    n_sel = min(TOP_N, n_slc)
    halves = PAGE_SIZE // SLC_BLK
    n_w = win_state.shape[1]

    def blk_map(k, kv_idx):
        def index_map(b, g, sel_ref, pt_ref):
            idx = jnp.minimum(sel_ref[(b * n_kv + g) * TOP_N + k], n_slc - 2)
            return (pt_ref[b * n_pages + idx // halves], idx % halves, kv_idx, 0, 0)
        return index_map

    head_spec = lambda: pl.BlockSpec((1, r_, hd), lambda b, g, s, p: (b, g, 0))
    in_specs = [
        head_spec(), head_spec(),
        pl.BlockSpec((1, r_, 3), lambda b, g, s, p: (b, g, 0)),
        pl.BlockSpec((r_, 3), lambda b, g, s, p: (g, 0)),
        head_spec(),
        pl.BlockSpec((1, 1, 1, hd), lambda b, g, s, p: (b, g, 0, 0)),
        pl.BlockSpec((1, 1, 1, hd), lambda b, g, s, p: (b, n_kv + g, 0, 0)),
        pl.BlockSpec((1, n_w, 1, n_kv, hd), lambda b, g, s, p: (b, 0, 0, 0, 0)),
        pl.BlockSpec((1, n_w, 1, n_kv, hd), lambda b, g, s, p: (b, 0, 1, 0, 0)),
        pl.BlockSpec((1, 1, 1, hd), lambda b, g, s, p: (b, g, 0, 0)),
        pl.BlockSpec((1, 1, 1, hd), lambda b, g, s, p: (b, n_kv + g, 0, 0)),
        pl.BlockSpec((1, n_slc, r_, SLC_BLK), lambda b, g, s, p: (g, 0, 0, 0)),
        pl.BlockSpec((1, r_, wb.shape[2]), lambda b, g, s, p: (g, 0, 0)),
        pl.BlockSpec((1, hd), lambda b, g, s, p: (0, 0)),
        pl.BlockSpec((3, hd), lambda b, g, s, p: (0, 0)),
    ]
    in_specs += [pl.BlockSpec((1, SLC_BLK, 1, n_kv, hd), blk_map(k, 0)) for k in range(n_sel)]
    in_specs += [pl.BlockSpec((1, SLC_BLK, 1, n_kv, hd), blk_map(k, 1)) for k in range(n_sel)]
    return pl.pallas_call(
        functools.partial(_attn_sample_kernel, n_sel=n_sel, n_slc=n_slc, n_kv=n_kv),
        grid_spec=pltpu.PrefetchScalarGridSpec(
            num_scalar_prefetch=2,
            grid=(bs, n_kv),
            in_specs=in_specs,
            out_specs=pl.BlockSpec((1, r_, hd), lambda b, g, s, p: (b, g, 0)),
        ),
        out_shape=jax.ShapeDtypeStruct((bs, n_heads, hd), F32),
        compiler_params=_cparams("arbitrary", "arbitrary"),
        name="attn_sample",
    )(sel, page_table.reshape(-1), q3, gate3, bg3, gbias2, o_cmp, slc_new, slc_new, win_state,
      win_state, win_new, win_new, tb, wb, q_norm.reshape(1, hd), k_norm,
      *([cache_slc] * (2 * n_sel)))


def _bucket_np(d):
    d = np.maximum(d, 0)
    n_exact = N_BUCKETS // 2
    df = np.maximum(d, 1).astype(np.float64)
    large = n_exact + (np.log(df / n_exact) / math.log(MAX_DIST / n_exact)
                       * (N_BUCKETS - n_exact)).astype(np.int64)
    return np.where(d < n_exact, d, np.minimum(large, N_BUCKETS - 1))


def _dist_bias(rel_table, d, valid, shift):
    fd = rel_table.astype(F32)[_bucket_np(np.arange(MAX_DIST + 1))]
    if shift:
        fd = fd - fd[MAX_DIST:MAX_DIST + 1]
    vals = jnp.moveaxis(fd[np.clip(d, 0, MAX_DIST)], -1, 0)
    return jnp.where(jnp.asarray(valid)[None], vals, NEG)


def _overlap_np(n_cmp, n_slc, rows, cols):
    cs = np.arange(n_cmp)[:, None] * CMP_STRIDE
    ss = np.arange(n_slc)[None, :] * SLC_BLK
    ov = np.minimum(cs + CMP_BLK, ss + SLC_BLK) - np.maximum(cs, ss)
    out = np.zeros((rows, cols), np.float32)
    out[:n_cmp, :n_slc] = np.maximum(ov, 0).astype(np.float32) / CMP_BLK
    return out


def _round_up(x, m):
    return (x + m - 1) // m * m


def kernel(x_prompt, x_sample, cache_cmp_kv, cache_slc_kv, state_win_kv, state_lru_h, state_conv,
           page_table, a_norm, a_w_in, a_conv_w, a_conv_b, a_w_rg, a_b_rg, a_w_ig, a_b_ig, a_lambda,
           a_w_out, kv_norm, w_kv, k_norm, cmp_pos, w_cmp1, w_cmp2, rel_table, b_norm, b_w_in,
           b_gate_bias, b_q_norm, b_w_out):
    batch, seq, d_model = x_prompt.shape
    bs = x_sample.shape[0]
    n_a, n_b = a_norm.shape[0], b_norm.shape[0]
    d_rnn = a_w_in.shape[2] // 2
    n_kv, hd = cache_cmp_kv.shape[3], cache_cmp_kv.shape[4]
    n_heads = rel_table.shape[1]
    r_ = n_heads // n_kv
    hq = n_heads * hd
    n_sg = 2 * n_kv
    cols = n_sg * hd
    n_pages = page_table.shape[1]
    past = n_pages * PAGE_SIZE
    m = batch * seq
    assert x_sample.shape[1] == 1 and seq % QB == 0 and seq >= WINDOW and hd == LANES
    assert seq // CMP_STRIDE == LANES and past >= WINDOW and state_win_kv.shape[1] == WINDOW

    xp = x_prompt.reshape(m, d_model)
    xs = jnp.pad(x_sample.reshape(bs, d_model), ((0, SAMPLE_ROWS - bs), (0, 0)))
    pad_rows = lambda a: jnp.pad(a, ((0, SAMPLE_ROWS - bs), (0, 0)))

    p_h, p_c, s_h, s_c = [], [], [], []
    h0_p = jnp.zeros((batch, d_rnn), F32)
    c0_p = jnp.zeros((batch, CONV_W - 1, d_rnn), F32)
    for l in range(n_a):
        wrg, wig = a_w_rg[l].astype(BF16), a_w_ig[l].astype(BF16)
        lru_w = (a_conv_w[l], a_conv_b[l], wrg, a_b_rg[l], wig, a_b_ig[l], a_lambda[l])
        u, us = _matmul(_norm_cast(xp, a_norm[l]), _norm_cast(xs, a_norm[l]), a_w_in, layer=l)
        gp, hl, ct = _lru_prompt(u, batch, seq, *lru_w, h0_p, c0_p)
        c0 = jnp.pad(jnp.swapaxes(state_conv[l], 0, 1), ((0, 0), (0, SAMPLE_ROWS - bs), (0, 0)))
        gs, hs = _lru_sample(us, *lru_w, pad_rows(state_lru_h[l]), c0)
        xp, xs = _matmul(gp, gs, a_w_out, layer=l, res=xp, res_s=xs)
        p_h.append(hl.reshape(batch, d_rnn))
        p_c.append(ct)
        s_h.append(hs[:bs])
        s_c.append(jnp.concatenate([state_conv[l][:, 1:], us[:bs, None, :d_rnn]], axis=1))

    kv, kvs, p_cmp_kv, p_slc_kv, p_win_rows = _matmul(
        _norm_cast(xp, kv_norm), _norm_cast(xs, kv_norm), w_kv[None], tm=512, tn=n_kv * hd,
        rows5=(batch, seq, n_kv, hd))
    kvs = kvs[:bs]
    rows5 = lambda a, n: a.reshape(n, -1, 2, n_kv, hd)
    p_win_kv = p_win_rows[:, seq - WINDOW:]
    s_cmp_kv, s_slc_kv = rows5(kvs[:, :cols], bs), rows5(kvs[:, cols:2 * cols], bs)
    s_win_rows = rows5(kvs[:, 2 * cols:], bs)
    s_win_kv = jnp.concatenate([state_win_kv, s_win_rows], axis=1)[:, 1:]

    half = CMP_BLK // 2
    hid = w_cmp1.shape[3]
    w1ab = jnp.concatenate([w_cmp1[:, :half].reshape(2, half * hd, hid),
                            w_cmp1[:, half:].reshape(2, half * hd, hid)], axis=2).astype(BF16)
    posab = jnp.stack([cmp_pos[:, :half].reshape(2, half * hd),
                       cmp_pos[:, half:].reshape(2, half * hd)], axis=1)
    w2b = w_cmp2.astype(BF16)
    own_pages = jnp.arange(m // PAGE_SIZE, dtype=jnp.int32).reshape(batch, seq // PAGE_SIZE)
    r16_p = _page_gather(p_cmp_kv.reshape(m // PAGE_SIZE, PAGE_SIZE, 2, n_kv, hd), own_pages)
    comp_p = _compress(r16_p, posab, w1ab, w2b, k_norm[0], n_kv)
    comp_s = _compress(_page_gather(cache_cmp_kv, page_table), posab, w1ab, w2b, k_norm[0], n_kv)

    n_c = seq // CMP_STRIDE
    n_cmp_p = (seq - CMP_BLK) // CMP_STRIDE + 1
    n_slc_p = seq // SLC_BLK
    wk = WINDOW + QB
    dv = WINDOW - np.arange(wk + 1)
    vrow = _dist_bias(rel_table, dv, dv >= 0, True)
    wbias = jnp.tile(vrow, (1, QB))[:, :QB * wk].reshape(n_heads, QB, wk)
    dc = (np.arange(QB)[:, None] - (np.arange(n_c)[None, :] - (n_c - QB // CMP_STRIDE)) * CMP_STRIDE
          - (CMP_BLK - 1))
    tcfix = jnp.where(jnp.asarray(dc >= 0)[None], _dist_bias(rel_table, dc, dc >= 0, True), 0.0)
    mimp_t = jnp.asarray(_overlap_np(n_cmp_p, n_slc_p, n_c, n_slc_p).T, BF16)
    expand = jnp.asarray((np.arange(n_slc_p)[:, None] == np.arange(seq)[None, :] // SLC_BLK), BF16)

    total = past + 1
    n_cmp_s = (total - CMP_BLK) // CMP_STRIDE + 1
    n_slc_s = -(-total // SLC_BLK)
    nr_s = past // CMP_STRIDE
    assert n_cmp_s == nr_s - 1 and n_slc_s == past // SLC_BLK + 1
    dcs = past - (np.arange(nr_s) * CMP_STRIDE + CMP_BLK - 1)
    cbias_s = _dist_bias(rel_table, dcs, (dcs >= 0) & (np.arange(nr_s) < n_cmp_s), False)
    cbias_s = cbias_s.reshape(n_kv, r_, nr_s)
    mimp_s = jnp.asarray(_overlap_np(n_cmp_s, n_slc_s, nr_s, _round_up(n_slc_s, LANES)), BF16)
    dss = past - np.arange(n_slc_s * SLC_BLK)
    tb_s = _dist_bias(rel_table, dss, dss >= 0, False)
    tb_s = tb_s.reshape(n_kv, r_, n_slc_s, SLC_BLK).transpose(0, 2, 1, 3)
    dws = WINDOW - np.arange(WINDOW + LANES)
    wb_s = _dist_bias(rel_table, dws, dws >= 0, False).reshape(n_kv, r_, WINDOW + LANES)

    slc_new = kvs[:, cols:2 * cols].reshape(bs, n_sg, 1, hd)
    win_new = kvs[:, 2 * cols:].reshape(bs, n_sg, 1, hd)
    slab_pad = ((0, 0), (0, 0), (0, 0), (0, LANES - 3 * r_))
    w_qg = b_w_in[:, :, :2 * hq]
    w_bg = jnp.pad(b_w_in[:, :, 2 * hq:].reshape(n_b, d_model, n_kv, 3 * r_), slab_pad)
    w_bg = w_bg.reshape(n_b, d_model, n_kv * LANES)
    for l in range(n_b):
        gbias = b_gate_bias[l]
        gb_slab = jnp.pad(gbias.reshape(n_kv, 1, 3 * r_), slab_pad[1:])

        xn, xns = _norm_cast(xp, b_norm[l]), _norm_cast(xs, b_norm[l])
        u, us = _matmul(xn, xns, w_qg, layer=l)
        bgs, bgs_s = _matmul(xn, xns, w_bg, layer=l)
        og = _attn_prompt(u, bgs, gb_slab, kv, comp_p, wbias, tcfix, mimp_t, expand,
                          b_q_norm[l], k_norm, batch, seq, n_kv, n_heads)

        us = us[:bs]
        q3 = us[:, :hq].reshape(bs, n_heads, hd)
        gate3 = us[:, hq:].reshape(bs, n_heads, hd)
        bg3 = bgs_s[:bs].reshape(bs, n_kv, LANES)[:, :, :3 * r_].reshape(bs, n_heads, 3)
        o_cmp, sel = _attn_sample_cmp(q3, comp_s, cbias_s, mimp_s, b_q_norm[l], n_kv, n_slc_s)
        os_ = _attn_sample(sel[:, :, 0, :TOP_N].reshape(-1), page_table, q3, gate3, bg3,
                           gbias.reshape(n_heads, 3), o_cmp, cache_slc_kv, slc_new, state_win_kv,
                           win_new, tb_s, wb_s, b_q_norm[l], k_norm, n_kv, n_slc_s)
        xp, xs = _matmul(og, pad_rows(os_.reshape(bs, hq).astype(BF16)), b_w_out, layer=l,
                         res=xp, res_s=xs)

    return (xp.reshape(batch, seq, d_model), xs[:bs].reshape(bs, 1, d_model),
            p_cmp_kv, p_slc_kv, p_win_kv, jnp.stack(p_h), jnp.stack(p_c),
            s_cmp_kv, s_slc_kv, s_win_kv, jnp.stack(s_h), jnp.stack(s_c))
```

```python
import functools
import math

import numpy as np
import jax
import jax.numpy as jnp
from jax import lax
from jax.experimental import pallas as pl
from jax.experimental.pallas import tpu as pltpu

F32 = jnp.float32
BF16 = jnp.bfloat16

EPS = 1e-6
NEG = -1e30
BIG = 1e30
LOWEST = -3e38
LRU_C = 8.0
CONV_W = 4
CMP_BLK = 32
CMP_STRIDE = 16
SLC_BLK = 64
TOP_N = 16
WINDOW = 512
N_BUCKETS = 32
MAX_DIST = 128
PAGE_SIZE = 128
QB = 128
LANES = 128
SUBLANES = 8
SEG_PAD = 8
SAMPLE_ROWS = 16
VMEM_LIMIT = 48 * 1024 * 1024


def _cparams(*sem):
    return pltpu.CompilerParams(dimension_semantics=sem, vmem_limit_bytes=VMEM_LIMIT)


def _dot(a, b):
    return jnp.dot(a, b, preferred_element_type=F32)


def _dot_nt(a, b):
    return lax.dot_general(a, b, (((1,), (1,)), ((), ())), preferred_element_type=F32)


def _rms(x, g):
    return x * lax.rsqrt(jnp.mean(x * x, axis=-1, keepdims=True) + EPS) * g


def _sigmoid(x):
    return jax.nn.sigmoid(x)


def _silu(x):
    return x * jax.nn.sigmoid(x)


def _norm_kernel(x_ref, g_ref, o_ref):
    o_ref[...] = _rms(x_ref[...], g_ref[...]).astype(o_ref.dtype)


def _norm_cast(x, g):
    m, d = x.shape
    tm = min(m, 512)
    return pl.pallas_call(
        _norm_kernel,
        grid=(m // tm,),
        in_specs=[pl.BlockSpec((tm, d), lambda i: (i, 0)),
                  pl.BlockSpec((1, d), lambda i: (0, 0))],
        out_specs=pl.BlockSpec((tm, d), lambda i: (i, 0)),
        out_shape=jax.ShapeDtypeStruct((m, d), BF16),
        compiler_params=_cparams("parallel"),
        name="norm_cast",
    )(x, g.reshape(1, d))


def _mm_kernel(a_ref, as_ref, w_ref, *rest, has_res, n_rows5):
    if has_res:
        r_ref, rs_ref, o_ref, os_ref = rest[:4]
    else:
        o_ref, os_ref = rest[:2]
    wb_scr = rest[-1]
    rows5_refs = rest[-1 - n_rows5:-1]
    wb_ref = wb_scr if w_ref.dtype != BF16 else w_ref

    @pl.when(pl.program_id(1) == 0)
    def _():
        if w_ref.dtype != BF16:
            wb_scr[...] = w_ref[...].astype(BF16)
        acc_s = _dot(as_ref[...], wb_ref[...])
        os_ref[...] = rs_ref[...] + acc_s if has_res else acc_s

    acc = _dot(a_ref[...], wb_ref[...])
    o_ref[...] = r_ref[...] + acc if has_res else acc

    for br, r5_ref in enumerate(rows5_refs):
        @pl.when(pl.program_id(0) // 2 == br)
        def _(r5_ref=r5_ref):
            n_grp, hd = r5_ref.shape[3], r5_ref.shape[4]
            for g in range(n_grp):
                r5_ref[0, :, 0, g, :] = acc[:, g * hd:(g + 1) * hd]


def _matmul(a, a_s, w, layer=0, n=None, res=None, res_s=None, tm=1024, tn=512, rows5=None):
    m, k = a.shape
    ms = a_s.shape[0]
    n = w.shape[2] if n is None else n
    tm = min(m, tm)
    tn = min(n, tn)
    has_res = res is not None
    in_specs = [pl.BlockSpec((tm, k), lambda j, i: (i, 0)),
                pl.BlockSpec((ms, k), lambda j, i: (0, 0)),
                pl.BlockSpec((None, k, tn), lambda j, i: (layer, 0, j))]
    args = [a, a_s, w]
    if has_res:
        in_specs += [pl.BlockSpec((tm, tn), lambda j, i: (i, j)),
                     pl.BlockSpec((ms, tn), lambda j, i: (0, j))]
        args += [res, res_s]
    out_specs = [pl.BlockSpec((tm, tn), lambda j, i: (i, j)),
                 pl.BlockSpec((ms, tn), lambda j, i: (0, j))]
    out_shape = [jax.ShapeDtypeStruct((m, n), F32), jax.ShapeDtypeStruct((ms, n), F32)]
    n_rows5 = 0
    if rows5 is not None:
        batch, seq, n_kv, hd = rows5
        n_rows5 = n // (2 * tn)
        nt = seq // tm
        assert tn == n_kv * hd and seq % tm == 0 and n % (2 * tn) == 0

        def rows5_map(br):
            def index_map(j, i):
                active, before = j // 2 == br, j < 2 * br
                park = lambda first, last: jnp.where(before, first, last)
                return (jnp.where(active, i // nt, park(0, batch - 1)),
                        jnp.where(active, i % nt, park(0, nt - 1)),
                        jnp.where(active, j % 2, park(0, 1)), 0, 0)
            return index_map

        for br in range(n_rows5):
            out_specs.append(pl.BlockSpec((1, tm, 1, n_kv, hd), rows5_map(br)))
            out_shape.append(jax.ShapeDtypeStruct((batch, seq, 2, n_kv, hd), F32))
    return pl.pallas_call(
        functools.partial(_mm_kernel, has_res=has_res, n_rows5=n_rows5),
        grid=(n // tn, m // tm),
        in_specs=in_specs,
        out_specs=out_specs,
        out_shape=out_shape,
        scratch_shapes=[pltpu.VMEM((k, tn) if w.dtype != BF16 else (SUBLANES, LANES), BF16)],
        compiler_params=_cparams("arbitrary", "arbitrary"),
        name="matmul",
    )(*args)


def _lru_coeffs(xc, wrg_ref, wig_ref, brg, big, lam, pos0_row):
    rows, cb = xc.shape
    bs = wrg_ref.shape[-1]
    xcb = xc.astype(BF16)
    r_parts, i_parts = [], []
    for n in range(cb // bs):
        xs = xcb[:, n * bs:(n + 1) * bs]
        r_parts.append(_dot(xs, wrg_ref[n]))
        i_parts.append(_dot(xs, wig_ref[n]))
    r = _sigmoid(jnp.concatenate(r_parts, axis=1) + brg)
    i = _sigmoid(jnp.concatenate(i_parts, axis=1) + big)
    nl = -lam
    softplus = jnp.maximum(nl, 0.0) + jnp.log1p(jnp.exp(-jnp.abs(nl)))
    log_a = -LRU_C * r * softplus
    a = jnp.exp(log_a)
    mult = jnp.sqrt(-jnp.tanh(log_a) * (a * a + 1.0))
    if pos0_row is not None:
        row = lax.broadcasted_iota(jnp.int32, (rows, cb), 0)
        mult = jnp.where(row == pos0_row, 1.0, mult)
    return a, mult * i * xc


def _lru_prompt_kernel(xb_ref, gate_ref, cw_ref, cb_ref, wrg_ref, brg_ref, wig_ref, big_ref,
                       lam_ref, h0_ref, c0_ref, g_ref, hl_ref, ct_ref, st_scr, un_scr, tail_scr, h_scr):
    tc = pl.program_id(2)
    tt, cb = xb_ref.shape
    nlb = cb // LANES
    seg = tt // SUBLANES
    tail = (CONV_W - 1) * SUBLANES

    @pl.when(tc == 0)
    def _():
        h_scr[...] = jnp.broadcast_to(h0_ref[0], (SUBLANES, cb))
        for i in range(CONV_W - 1):
            tail_scr[i * SUBLANES:(i + 1) * SUBLANES, :] = jnp.broadcast_to(c0_ref[0, i:i + 1, :],
                                                                             (SUBLANES, cb))

    pitch = seg + SEG_PAD

    def to_segments(src_ref, slot):
        cols = []
        for lb in range(nlb):
            for j in range(SUBLANES):
                st_scr[slot, lb, j * pitch:j * pitch + seg, :] = src_ref[j * seg:(j + 1) * seg,
                                                                         lb * LANES:(lb + 1) * LANES]
            cols.append(jnp.concatenate(
                [st_scr[slot, lb, pl.ds(k, SUBLANES, stride=pitch), :] for k in range(seg)], axis=0))
        return jnp.concatenate(cols, axis=1)

    x = to_segments(xb_ref, 0)

    sub = lax.broadcasted_iota(jnp.int32, (SUBLANES, cb), 0)
    heads = []
    for i in range(CONV_W - 1):
        cur = x[tt - tail + i * SUBLANES:tt - tail + (i + 1) * SUBLANES]
        prev = tail_scr[i * SUBLANES:(i + 1) * SUBLANES, :]
        heads.append(pltpu.roll(jnp.where(sub == SUBLANES - 1, prev, cur), 1, axis=0))
    tail_scr[...] = x[tt - tail:tt]
    w = cw_ref[...]
    xc = cb_ref[...]
    for k in range(CONV_W - 1):
        m = CONV_W - 1 - k
        xc = xc + w[k:k + 1] * jnp.concatenate(heads[CONV_W - 1 - m:] + [x[0:tt - m * SUBLANES]], axis=0)
    xc = xc + w[CONV_W - 1:CONV_W] * x

    a, b_in = _lru_coeffs(xc, wrg_ref, wig_ref, brg_ref[...], big_ref[...], lam_ref[...],
                           jnp.where(tc == 0, 0, -1))

    blk = lambda v, k: v[k * SUBLANES:(k + 1) * SUBLANES]
    e, p = blk(b_in, 0), blk(a, 0)
    for k in range(1, seg):
        e = blk(a, k) * e + blk(b_in, k)
        p = blk(a, k) * p
    carry = h_scr[0:1, :]
    carries = [carry]
    for j in range(SUBLANES - 1):
        carry = p[j:j + 1] * carry + e[j:j + 1]
        carries.append(carry)
    h = jnp.concatenate(carries, axis=0)
    hs = []
    for k in range(seg):
        h = blk(a, k) * h + blk(b_in, k)
        hs.append(h)
    h_scr[...] = jnp.broadcast_to(h[SUBLANES - 1:SUBLANES], (SUBLANES, cb))
    g = jnp.concatenate(hs, axis=0) * _silu(to_segments(gate_ref, 1))

    for lb in range(nlb):
        for k in range(seg):
            un_scr[lb, pl.ds(k, SUBLANES, stride=pitch), :] = g[k * SUBLANES:(k + 1) * SUBLANES,
                                                                lb * LANES:(lb + 1) * LANES]
        for j in range(SUBLANES):
            g_ref[j * seg:(j + 1) * seg, lb * LANES:(lb + 1) * LANES] = (
                un_scr[lb, j * pitch:j * pitch + seg, :].astype(g_ref.dtype))

    @pl.when(tc == pl.num_programs(2) - 1)
    def _():
        hl_ref[0] = h[SUBLANES - 1:SUBLANES]
        ct_ref[0] = xb_ref[tt - (CONV_W - 1):tt, :]


def _lru_prompt(u, batch, seq, cw, cb_, wrg, brg, wig, big, lam, h0, c0):
    m, c2 = u.shape
    c = c2 // 2
    cb = min(c, 1024)
    tt = min(seq, 256)
    nt = seq // tt
    ncb = c // cb
    bs = wrg.shape[-1]
    nb = cb // bs
    vec = lambda: pl.BlockSpec((1, cb), lambda b, j, t: (0, j))
    return pl.pallas_call(
        _lru_prompt_kernel,
        grid=(batch, ncb, nt),
        in_specs=[
            pl.BlockSpec((tt, cb), lambda b, j, t: (b * nt + t, j)),
            pl.BlockSpec((tt, cb), lambda b, j, t: (b * nt + t, ncb + j)),
            pl.BlockSpec((CONV_W, cb), lambda b, j, t: (0, j)),
            vec(),
            pl.BlockSpec((nb, bs, bs), lambda b, j, t: (j, 0, 0)),
            vec(),
            pl.BlockSpec((nb, bs, bs), lambda b, j, t: (j, 0, 0)),
            vec(),
            vec(),
            pl.BlockSpec((1, 1, cb), lambda b, j, t: (b, 0, j)),
            pl.BlockSpec((1, CONV_W - 1, cb), lambda b, j, t: (b, 0, j)),
        ],
        out_specs=[
            pl.BlockSpec((tt, cb), lambda b, j, t: (b * nt + t, j)),
            pl.BlockSpec((1, 1, cb), lambda b, j, t: (b, 0, j)),
            pl.BlockSpec((1, CONV_W - 1, cb), lambda b, j, t: (b, 0, j)),
        ],
        out_shape=[
            jax.ShapeDtypeStruct((m, c), BF16),
            jax.ShapeDtypeStruct((batch, 1, c), F32),
            jax.ShapeDtypeStruct((batch, CONV_W - 1, c), F32),
        ],
        scratch_shapes=[pltpu.VMEM((2, cb // LANES, tt + SUBLANES * SEG_PAD, LANES), F32),
                        pltpu.VMEM((cb // LANES, tt + SUBLANES * SEG_PAD, LANES), F32),
                        pltpu.VMEM(((CONV_W - 1) * SUBLANES, cb), F32),
                        pltpu.VMEM((SUBLANES, cb), F32)],
        compiler_params=_cparams("parallel", "parallel", "arbitrary"),
        name="lru_prompt",
    )(u, u, cw, cb_.reshape(1, c), wrg, brg.reshape(1, c), wig, big.reshape(1, c),
      lam.reshape(1, c), h0.reshape(batch, 1, c), c0)


def _lru_sample_kernel(xb_ref, gate_ref, cw_ref, cb_ref, wrg_ref, brg_ref, wig_ref, big_ref,
                       lam_ref, h0_ref, c0_ref, g_ref, h_ref):
    w = cw_ref[...]
    xc = cb_ref[...]
    for k in range(CONV_W - 1):
        xc = xc + w[k:k + 1] * c0_ref[k]
    xc = xc + w[CONV_W - 1:CONV_W] * xb_ref[...]
    a, b_in = _lru_coeffs(xc, wrg_ref, wig_ref, brg_ref[...], big_ref[...], lam_ref[...], None)
    h = a * h0_ref[...] + b_in
    h_ref[...] = h
    g_ref[...] = (h * _silu(gate_ref[...])).astype(g_ref.dtype)


def _lru_sample(u, cw, cb_, wrg, brg, wig, big, lam, h0, c0):
    rows, c2 = u.shape
    c = c2 // 2
    cb = min(c, 1024)
    ncb = c // cb
    bs = wrg.shape[-1]
    nb = cb // bs
    vec = lambda: pl.BlockSpec((1, cb), lambda j: (0, j))
    return pl.pallas_call(
        _lru_sample_kernel,
        grid=(ncb,),
        in_specs=[
            pl.BlockSpec((rows, cb), lambda j: (0, j)),
            pl.BlockSpec((rows, cb), lambda j: (0, ncb + j)),
            pl.BlockSpec((CONV_W, cb), lambda j: (0, j)),
            vec(),
            pl.BlockSpec((nb, bs, bs), lambda j: (j, 0, 0)),
            vec(),
            pl.BlockSpec((nb, bs, bs), lambda j: (j, 0, 0)),
            vec(),
            vec(),
            pl.BlockSpec((rows, cb), lambda j: (0, j)),
            pl.BlockSpec((CONV_W - 1, rows, cb), lambda j: (0, 0, j)),
        ],
        out_specs=[pl.BlockSpec((rows, cb), lambda j: (0, j)),
                   pl.BlockSpec((rows, cb), lambda j: (0, j))],
        out_shape=[jax.ShapeDtypeStruct((rows, c), BF16),
                   jax.ShapeDtypeStruct((rows, c), F32)],
        compiler_params=_cparams("parallel"),
        name="lru_sample",
    )(u, u, cw, cb_.reshape(1, c), wrg, brg.reshape(1, c), wig, big.reshape(1, c),
      lam.reshape(1, c), h0, c0)


GATHER_PAGES = 4


def _page_gather_kernel(pt_ref, *refs):
    del pt_ref
    c_refs, o_ref, slab_scr = refs[:GATHER_PAGES], refs[GATHER_PAGES], refs[GATHER_PAGES + 1]
    n_kv, hd = c_refs[0].shape[3], c_refs[0].shape[4]
    rows = PAGE_SIZE // CMP_STRIDE
    for q, c_ref in enumerate(c_refs):
        for sg in range(2 * n_kv):
            slab_scr[q, sg] = c_ref[0, :, sg // n_kv, sg % n_kv, :]
            for l in range(CMP_STRIDE):
                o_ref[0, sg, q * rows:(q + 1) * rows, l * hd:(l + 1) * hd] = (
                    slab_scr[q, sg, pl.ds(l, rows, stride=CMP_STRIDE), :])


def _page_gather(pages, page_table):
    bs, n_pages = page_table.shape
    n_kv, hd = pages.shape[3], pages.shape[4]
    n_sg = 2 * n_kv
    rows = PAGE_SIZE // CMP_STRIDE
    assert n_pages % GATHER_PAGES == 0

    def page_map(q):
        return lambda b, j, pt: (pt[b * n_pages + j * GATHER_PAGES + q], 0, 0, 0, 0)

    return pl.pallas_call(
        _page_gather_kernel,
        grid_spec=pltpu.PrefetchScalarGridSpec(
            num_scalar_prefetch=1,
            grid=(bs, n_pages // GATHER_PAGES),
            in_specs=[pl.BlockSpec((1, PAGE_SIZE, 2, n_kv, hd), page_map(q))
                      for q in range(GATHER_PAGES)],
            out_specs=pl.BlockSpec((1, n_sg, GATHER_PAGES * rows, CMP_STRIDE * hd),
                                   lambda b, j, pt: (b, 0, j, 0)),
            scratch_shapes=[pltpu.VMEM((GATHER_PAGES, n_sg, PAGE_SIZE, hd), F32)],
        ),
        out_shape=jax.ShapeDtypeStruct((bs, n_sg, n_pages * rows, CMP_STRIDE * hd), F32),
        compiler_params=_cparams("parallel", "arbitrary"),
        name="page_gather",
    )(page_table.reshape(-1), *([pages] * GATHER_PAGES))


def _compress_kernel(r_ref, pos_ref, w1_ref, w2_ref, kn_ref, o_ref, q_scr, *, n_kv):
    sg = pl.program_id(1)
    rows = r_ref[0, 0]
    nr = rows.shape[0]
    hid = w1_ref.shape[2] // 2
    xa = (rows + pos_ref[0, 0:1]).astype(BF16)
    xb = (rows + pos_ref[0, 1:2]).astype(BF16)
    w1 = w1_ref[0]
    p = _dot(xa, w1[:, :hid])
    q_scr[0:nr, :] = _dot(xb, w1[:, hid:])
    q_scr[nr:nr + 8, :] = jnp.zeros((8, hid), F32)
    pre = p + q_scr[1:nr + 1, :]
    comp = _dot(_silu(pre).astype(BF16), w2_ref[0])
    normed = _rms(comp, kn_ref[...])
    o_ref[0, 0] = jnp.where(sg < n_kv, normed, comp).astype(o_ref.dtype)


def _compress(r16, posab, w1ab, w2, k_norm0, n_kv):
    b, n_sg, nr, kk = r16.shape
    hid2 = w1ab.shape[2]
    hd = w2.shape[2]
    return pl.pallas_call(
        functools.partial(_compress_kernel, n_kv=n_kv),
        grid=(b, n_sg),
        in_specs=[
            pl.BlockSpec((1, 1, nr, kk), lambda i, s: (i, s, 0, 0)),
            pl.BlockSpec((1, 2, kk), lambda i, s: (s // n_kv, 0, 0)),
            pl.BlockSpec((1, kk, hid2), lambda i, s: (s // n_kv, 0, 0)),
            pl.BlockSpec((1, hid2 // 2, hd), lambda i, s: (s // n_kv, 0, 0)),
            pl.BlockSpec((1, hd), lambda i, s: (0, 0)),
        ],
        out_specs=pl.BlockSpec((1, 1, nr, hd), lambda i, s: (i, s, 0, 0)),
        out_shape=jax.ShapeDtypeStruct((b, n_sg, nr, hd), BF16),
        scratch_shapes=[pltpu.VMEM((nr + 8, hid2 // 2), F32)],
        compiler_params=_cparams("parallel", "arbitrary"),
        name="compress",
    )(r16, posab, w1ab, w2, k_norm0.reshape(1, hd))


def _attn_prompt_kernel(uq_ref, ug_ref, bg_ref, gb_ref, ks_ref, vs_ref, kw_ref, vw_ref,
                        kc_ref, vc_ref, wb_ref, tc_ref, mit_ref, ex_ref, qn_ref, kn_ref, og_ref,
                        ks_scr, vs_scr, kw_scr, vw_scr, madd_scr, lg_scr, p_scr, m_scr,
                        *, n_heads_grp, n_slc):
    r_ = n_heads_grp
    c = pl.program_id(2)
    t0 = c * QB
    seq, hd = ks_ref.shape
    rq = r_ * QB
    wk = WINDOW + QB
    n_wt = wk // QB

    @pl.when(c == 0)
    def _():
        ks_scr[...] = _rms(ks_ref[...], kn_ref[1:2]).astype(BF16)
        vs_scr[:, 0:hd] = vs_ref[...].astype(BF16)
        vs_scr[:, hd:2 * hd] = jnp.ones((seq, hd), BF16)
        kw_scr[0:WINDOW, :] = jnp.zeros((WINDOW, hd), BF16)
        vw_scr[0:WINDOW, :] = jnp.zeros((WINDOW, 2 * hd), BF16)
        kw_scr[WINDOW:WINDOW + seq, :] = _rms(kw_ref[...], kn_ref[2:3]).astype(BF16)
        vw_scr[WINDOW:WINDOW + seq, 0:hd] = vw_ref[...].astype(BF16)
        vw_scr[WINDOW:WINDOW + seq, hd:2 * hd] = jnp.ones((seq, hd), BF16)
        lg_scr[...] = jnp.zeros(lg_scr.shape, F32)

    uq = uq_ref[...]
    qg = qn_ref[...] * hd ** -0.5
    qb = jnp.concatenate(
        [uq[:, r * hd:(r + 1) * hd]
         * lax.rsqrt(jnp.mean(uq[:, r * hd:(r + 1) * hd] ** 2, axis=-1, keepdims=True) + EPS) * qg
         for r in range(r_)], axis=0).astype(BF16)

    n_c = kc_ref.shape[2]
    bias_c = pltpu.roll(tc_ref[...], (c * (QB // CMP_STRIDE) + QB // CMP_STRIDE) % n_c, axis=2)
    lc = _dot_nt(qb, kc_ref[0, 0]).reshape(r_, QB, n_c) + bias_c
    tq = t0 + lax.broadcasted_iota(jnp.int32, (QB, n_c), 0)
    nn = lax.broadcasted_iota(jnp.int32, (QB, n_c), 1)
    c_ok = (tq - (nn * CMP_STRIDE + (CMP_BLK - 1)) >= 0)[None]
    lc = jnp.where(c_ok, lc, NEG)
    e = jnp.exp(lc - jnp.max(lc, axis=-1, keepdims=True))
    pc = jnp.where(c_ok, e / jnp.sum(e, axis=-1, keepdims=True), 0.0)
    pcb = pc.reshape(rq, n_c).astype(BF16)
    o_cmp = _dot(pcb, vc_ref[0, 0])
    imp_all = _dot_nt(mit_ref[...], pcb)
    imp = imp_all[:, 0:QB]
    for r in range(1, r_):
        imp = imp + imp_all[:, r * QB:(r + 1) * QB]

    jj = lax.broadcasted_iota(jnp.int32, (n_slc, QB), 0)
    cur = (t0 + lax.broadcasted_iota(jnp.int32, (n_slc, QB), 1)) // SLC_BLK
    forced = (jj == 0) | (jj == cur) | (jj == cur - 1)
    allowed = jj <= cur
    score = jnp.where(forced, BIG, jnp.where(allowed, imp, NEG))
    rank = jnp.zeros((n_slc, QB), F32)
    for k in range(n_slc):
        sk = score[k:k + 1, :]
        rank = rank + jnp.where(sk > score, 1.0, jnp.where((sk == score) & (k < jj), 1.0, 0.0))
    sel = jnp.where((rank < min(TOP_N, n_slc)) & allowed, 1.0, 0.0).astype(BF16)
    selx = lax.dot_general(sel, ex_ref[...], (((0,), (0,)), ((), ())),
                           preferred_element_type=F32)
    tk = 2 * QB
    for kp in range(seq // tk):
        madd_scr[kp] = (selx[:, kp * tk:(kp + 1) * tk] - 1.0) * BIG

    m_scr[...] = jnp.full((rq, tk), NEG, F32)
    near1 = wb_ref[:, :, WINDOW - QB:WINDOW]
    near0 = wb_ref[:, :, WINDOW:WINDOW + QB]

    def logits_pair(kp, bias_lo, bias_hi):
        k0 = pl.multiple_of(kp * tk, tk)
        s3 = _dot_nt(qb, ks_scr[pl.ds(k0, tk), :]).reshape(r_, QB, tk) + madd_scr[kp][None]
        if bias_lo is not None or bias_hi is not None:
            lo, hi = s3[:, :, 0:QB], s3[:, :, QB:tk]
            lo = lo if bias_lo is None else lo + bias_lo
            hi = hi if bias_hi is None else hi + bias_hi
            s3 = jnp.concatenate([lo, hi], axis=2)
        s = s3.reshape(rq, tk)
        lg_scr[kp] = s
        m_scr[...] = jnp.maximum(m_scr[...], s)

    def far_body(kp, carry):
        logits_pair(kp, None, None)
        return carry

    last = c // 2
    c_odd = c % 2 == 1
    lax.fori_loop(0, jnp.where(c_odd, last, jnp.maximum(last - 1, 0)), far_body, 0)

    @pl.when(c_odd)
    def _():
        logits_pair(last, near1, near0)

    @pl.when(jnp.logical_not(c_odd) & (c >= 2))
    def _():
        logits_pair(last - 1, None, near1)

    @pl.when(jnp.logical_not(c_odd))
    def _():
        logits_pair(last, near0, None)

    m_scr[...] = jnp.broadcast_to(jnp.max(m_scr[...], axis=-1, keepdims=True), (rq, tk))
    for kp in range(seq // tk):
        p = jnp.where(kp <= last, jnp.exp(lg_scr[kp] - m_scr[...]), 0.0)
        p_scr[:, kp * tk:(kp + 1) * tk] = p.astype(BF16)
    acc = _dot(p_scr[...], vs_scr[...])
    o_slc = acc[:, 0:hd] / acc[:, hd:2 * hd]

    w0 = pl.multiple_of(t0, QB)
    sw = _dot_nt(qb, kw_scr[pl.ds(w0, wk), :])
    tiles = []
    for j in range(n_wt):
        s = sw[:, j * QB:(j + 1) * QB]
        if j == 0 or j >= n_wt - 2:
            s = s + wb_ref[:, :, j * QB:(j + 1) * QB].reshape(rq, QB)
        if j < n_wt - 1:
            s = s + jnp.where(c + j >= n_wt - 1, 0.0, NEG)
        tiles.append(s)
    mw = tiles[0]
    for s in tiles[1:]:
        mw = jnp.maximum(mw, s)
    mw = jnp.max(mw, axis=-1, keepdims=True)
    pw = jnp.concatenate([jnp.exp(s - mw).astype(BF16) for s in tiles], axis=1)
    accw = _dot(pw, vw_scr[pl.ds(w0, wk), :])
    o_win = accw[:, 0:hd] / accw[:, hd:2 * hd]

    bg = _sigmoid(bg_ref[...] + gb_ref[0])
    ug = ug_ref[...]
    for r in range(r_):
        rows = slice(r * QB, (r + 1) * QB)
        o = (bg[:, 3 * r:3 * r + 1] * o_cmp[rows] + bg[:, 3 * r + 1:3 * r + 2] * o_slc[rows]
             + bg[:, 3 * r + 2:3 * r + 3] * o_win[rows])
        og_ref[:, r * hd:(r + 1) * hd] = (o * _silu(ug[:, r * hd:(r + 1) * hd])).astype(og_ref.dtype)


def _attn_prompt(u, bgs, gbias, kv, comp, wbias, tcfix, mimp_t, expand, q_norm, k_norm,
                 batch, seq, n_kv, n_heads):
    m = u.shape[0]
    hd = k_norm.shape[1]
    r_ = n_heads // n_kv
    hq = n_heads * hd
    nq = seq // QB
    n_c = comp.shape[2]
    n_slc = seq // SLC_BLK
    rq = r_ * QB
    wk = WINDOW + QB
    n_sg = 2 * n_kv
    kv_spec = lambda col0: pl.BlockSpec((seq, hd), lambda b, g, c: (b, col0 + g))
    return pl.pallas_call(
        functools.partial(_attn_prompt_kernel, n_heads_grp=r_, n_slc=n_slc),
        grid=(batch, n_kv, nq),
        in_specs=[
            pl.BlockSpec((QB, r_ * hd), lambda b, g, c: (b * nq + c, g)),
            pl.BlockSpec((QB, r_ * hd), lambda b, g, c: (b * nq + c, n_kv + g)),
            pl.BlockSpec((QB, LANES), lambda b, g, c: (b * nq + c, g)),
            pl.BlockSpec((1, 1, LANES), lambda b, g, c: (g, 0, 0)),
            kv_spec(n_sg), kv_spec(n_sg + n_kv), kv_spec(2 * n_sg), kv_spec(2 * n_sg + n_kv),
            pl.BlockSpec((1, 1, n_c, hd), lambda b, g, c: (b, g, 0, 0)),
            pl.BlockSpec((1, 1, n_c, hd), lambda b, g, c: (b, n_kv + g, 0, 0)),
            pl.BlockSpec((r_, QB, wk), lambda b, g, c: (g, 0, 0)),
            pl.BlockSpec((r_, QB, n_c), lambda b, g, c: (g, 0, 0)),
            pl.BlockSpec((n_slc, n_c), lambda b, g, c: (0, 0)),
            pl.BlockSpec((n_slc, seq), lambda b, g, c: (0, 0)),
            pl.BlockSpec((1, hd), lambda b, g, c: (0, 0)),
            pl.BlockSpec((3, hd), lambda b, g, c: (0, 0)),
        ],
        out_specs=pl.BlockSpec((QB, r_ * hd), lambda b, g, c: (b * nq + c, g)),
        out_shape=jax.ShapeDtypeStruct((m, hq), BF16),
        scratch_shapes=[
            pltpu.VMEM((seq, hd), BF16), pltpu.VMEM((seq, 2 * hd), BF16),
            pltpu.VMEM((seq + WINDOW, hd), BF16), pltpu.VMEM((seq + WINDOW, 2 * hd), BF16),
            pltpu.VMEM((seq // (2 * QB), QB, 2 * QB), F32),
            pltpu.VMEM((seq // (2 * QB), rq, 2 * QB), F32),
            pltpu.VMEM((rq, seq), BF16),
            pltpu.VMEM((rq, 2 * QB), F32),
        ],
        compiler_params=_cparams("arbitrary", "arbitrary", "arbitrary"),
        name="attn_prompt",
    )(u, u, bgs, gbias, kv, kv, kv, kv, comp, comp, wbias, tcfix, mimp_t, expand,
      q_norm.reshape(1, hd), k_norm)


def _attn_sample_cmp_kernel(q_ref, kc_ref, vc_ref, cb_ref, mi_ref, qn_ref, oc_ref, sel_ref,
                            *, n_slc):
    r_, hd = q_ref.shape[1], q_ref.shape[2]
    qb = (_rms(q_ref[0], qn_ref[...]) * hd ** -0.5).astype(BF16)
    lc = _dot_nt(qb, kc_ref[0, 0]) + cb_ref[0]
    e = jnp.exp(lc - jnp.max(lc, axis=-1, keepdims=True))
    pcb = (e / jnp.sum(e, axis=-1, keepdims=True)).astype(BF16)
    oc_ref[0] = _dot(pcb, vc_ref[0, 0])
    nsp = mi_ref.shape[1]
    imp = jnp.sum(_dot(pcb, mi_ref[...]), axis=0, keepdims=True)

    jl = lax.broadcasted_iota(jnp.int32, (1, nsp), 1)
    forced = (jl == 0) | (jl == n_slc - 1) | (jl == n_slc - 2)
    score = jnp.where(jl < n_slc, jnp.where(forced, BIG, imp), LOWEST)
    s_rows = jnp.broadcast_to(score, (nsp, nsp))
    kk = lax.broadcasted_iota(jnp.int32, (nsp, nsp), 0)
    jj = lax.broadcasted_iota(jnp.int32, (nsp, nsp), 1)
    s_col = jnp.sum(jnp.where(kk == jj, s_rows, 0.0), axis=1, keepdims=True)
    ahead = jnp.where(s_col > s_rows, 1.0, jnp.where((s_col == s_rows) & (kk < jj), 1.0, 0.0))
    rank = jnp.sum(ahead, axis=0, keepdims=True)
    lane = lax.broadcasted_iota(jnp.int32, (1, LANES), 1)
    out = jnp.zeros((1, LANES), F32)
    jf = jl.astype(F32)
    for k in range(min(TOP_N, n_slc)):
        idx_k = jnp.sum(jnp.where(rank == k, jf, 0.0), axis=1, keepdims=True)
        out = out + jnp.where(lane == k, idx_k, 0.0)
    sel_ref[0, 0] = jnp.broadcast_to(out, (8, LANES)).astype(jnp.int32)


def _attn_sample_cmp(q3, comp, cbias, mimp, q_norm, n_kv, n_slc):
    bs, n_heads, hd = q3.shape
    r_ = n_heads // n_kv
    nr = comp.shape[2]
    nsp = mimp.shape[1]
    return pl.pallas_call(
        functools.partial(_attn_sample_cmp_kernel, n_slc=n_slc),
        grid=(bs, n_kv),
        in_specs=[
            pl.BlockSpec((1, r_, hd), lambda b, g: (b, g, 0)),
            pl.BlockSpec((1, 1, nr, hd), lambda b, g: (b, g, 0, 0)),
            pl.BlockSpec((1, 1, nr, hd), lambda b, g: (b, n_kv + g, 0, 0)),
            pl.BlockSpec((1, r_, nr), lambda b, g: (g, 0, 0)),
            pl.BlockSpec((nr, nsp), lambda b, g: (0, 0)),
            pl.BlockSpec((1, hd), lambda b, g: (0, 0)),
        ],
        out_specs=[pl.BlockSpec((1, r_, hd), lambda b, g: (b, g, 0)),
                   pl.BlockSpec((1, 1, 8, LANES), lambda b, g: (b, g, 0, 0))],
        out_shape=[jax.ShapeDtypeStruct((bs, n_heads, hd), F32),
                   jax.ShapeDtypeStruct((bs, n_kv, 8, LANES), jnp.int32)],
        compiler_params=_cparams("parallel", "parallel"),
        name="attn_sample_cmp",
    )(q3, comp, comp, cbias, mimp, q_norm.reshape(1, hd))


def _attn_sample_kernel(sel_ref, pt_ref, q_ref, gate_ref, bg_ref, gb_ref, oc_ref, ksn_ref, vsn_ref,
                        kw_ref, vw_ref, kwn_ref, vwn_ref, tb_ref, wb_ref, qn_ref, kn_ref, *rest,
                        n_sel, n_slc, n_kv):
    del pt_ref
    k_refs, v_refs, o_ref = rest[:n_sel], rest[n_sel:2 * n_sel], rest[2 * n_sel]
    b, g = pl.program_id(0), pl.program_id(1)
    r_, hd = q_ref.shape[1], q_ref.shape[2]
    qb = (_rms(q_ref[0], qn_ref[...]) * hd ** -0.5).astype(BF16)

    def own_group(ref):
        rows = ref[0, :, 0, 0, :]
        for gi in range(1, n_kv):
            rows = jnp.where(g == gi, ref[0, :, 0, gi, :], rows)
        return rows

    k_new = ksn_ref[0, 0]
    v_new = vsn_ref[0, 0]
    logits, values = [], []
    for k in range(n_sel):
        idx = sel_ref[(b * n_kv + g) * TOP_N + k]
        is_new = idx == n_slc - 1
        kt = jnp.where(is_new, jnp.broadcast_to(k_new, (SLC_BLK, hd)), own_group(k_refs[k]))
        vt = jnp.where(is_new, jnp.broadcast_to(v_new, (SLC_BLK, hd)), own_group(v_refs[k]))
        logits.append(_dot_nt(qb, _rms(kt, kn_ref[1:2]).astype(BF16)) + tb_ref[0, idx])
        values.append(vt.astype(BF16))
    m = logits[0].max(axis=-1, keepdims=True)
    for s in logits[1:]:
        m = jnp.maximum(m, s.max(axis=-1, keepdims=True))
    es = [jnp.exp(s - m) for s in logits]
    den = es[0].sum(axis=-1, keepdims=True)
    for e in es[1:]:
        den = den + e.sum(axis=-1, keepdims=True)
    o_slc = jnp.zeros((r_, hd), F32)
    for e, v in zip(es, values):
        o_slc = o_slc + _dot((e / den).astype(BF16), v)

    n_w = kw_ref.shape[1]
    lw = _dot_nt(qb, _rms(own_group(kw_ref), kn_ref[2:3]).astype(BF16)) + wb_ref[0, :, 0:n_w]
    kwn = _rms(kwn_ref[0, 0], kn_ref[2:3]).astype(BF16).astype(F32)
    l_new = jnp.sum(qb.astype(F32) * kwn, axis=-1, keepdims=True) + wb_ref[0, :, n_w:n_w + 1]
    mw = jnp.maximum(lw.max(axis=-1, keepdims=True), l_new)
    ew, e_new = jnp.exp(lw - mw), jnp.exp(l_new - mw)
    dw = ew.sum(axis=-1, keepdims=True) + e_new
    o_win = (_dot((ew / dw).astype(BF16), own_group(vw_ref).astype(BF16))
             + (e_new / dw).astype(BF16).astype(F32) * vwn_ref[0, 0].astype(BF16).astype(F32))

    bg = _sigmoid(bg_ref[0] + gb_ref[...])
    o = bg[:, 0:1] * oc_ref[0] + bg[:, 1:2] * o_slc + bg[:, 2:3] * o_win
    o_ref[0] = o * _silu(gate_ref[0])


def _attn_sample(sel, page_table, q3, gate3, bg3, gbias2, o_cmp, cache_slc, slc_new, win_state,
                 win_new, tb, wb, q_norm, k_norm, n_kv, n_slc):
    bs, n_heads, hd = q3.shape
    r_ = n_heads // n_kv
    n_pages = page_table.shape[1]
    n_pool = cache_slc.shape[0]
    n_sel = min(TOP_N, n_slc)
    halves = PAGE_SIZE // SLC_BLK
    n_w = win_state.shape[1]

    def blk_map(k, kv_idx):
        def index_map(b, g, sel_ref, pt_ref):
            idx = jnp.minimum(sel_ref[(b * n_kv + g) * TOP_N + k], n_slc - 2)
            return (pt_ref[b * n_pages + idx // halves], idx % halves, kv_idx, 0, 0)
        return index_map

    head_spec = lambda: pl.BlockSpec((1, r_, hd), lambda b, g, s, p: (b, g, 0))
    in_specs = [
        head_spec(), head_spec(),
        pl.BlockSpec((1, r_, 3), lambda b, g, s, p: (b, g, 0)),
        pl.BlockSpec((r_, 3), lambda b, g, s, p: (g, 0)),
        head_spec(),
        pl.BlockSpec((1, 1, 1, hd), lambda b, g, s, p: (b, g, 0, 0)),
        pl.BlockSpec((1, 1, 1, hd), lambda b, g, s, p: (b, n_kv + g, 0, 0)),
        pl.BlockSpec((1, n_w, 1, n_kv, hd), lambda b, g, s, p: (b, 0, 0, 0, 0)),
        pl.BlockSpec((1, n_w, 1, n_kv, hd), lambda b, g, s, p: (b, 0, 1, 0, 0)),
        pl.BlockSpec((1, 1, 1, hd), lambda b, g, s, p: (b, g, 0, 0)),
        pl.BlockSpec((1, 1, 1, hd), lambda b, g, s, p: (b, n_kv + g, 0, 0)),
        pl.BlockSpec((1, n_slc, r_, SLC_BLK), lambda b, g, s, p: (g, 0, 0, 0)),
        pl.BlockSpec((1, r_, wb.shape[2]), lambda b, g, s, p: (g, 0, 0)),
        pl.BlockSpec((1, hd), lambda b, g, s, p: (0, 0)),
        pl.BlockSpec((3, hd), lambda b, g, s, p: (0, 0)),
    ]
    in_specs += [pl.BlockSpec((1, SLC_BLK, 1, n_kv, hd), blk_map(k, 0)) for k in range(n_sel)]
    in_specs += [pl.BlockSpec((1, SLC_BLK, 1, n_kv, hd), blk_map(k, 1)) for k in range(n_sel)]
    return pl.pallas_call(
        functools.partial(_attn_sample_kernel, n_sel=n_sel, n_slc=n_slc, n_kv=n_kv),
        grid_spec=pltpu.PrefetchScalarGridSpec(
            num_scalar_prefetch=2,
            grid=(bs, n_kv),
            in_specs=in_specs,
            out_specs=pl.BlockSpec((1, r_, hd), lambda b, g, s, p: (b, g, 0)),
        ),
        out_shape=jax.ShapeDtypeStruct((bs, n_heads, hd), F32),
        compiler_params=_cparams("arbitrary", "arbitrary"),
        name="attn_sample",
    )(sel, page_table.reshape(-1), q3, gate3, bg3, gbias2, o_cmp, slc_new, slc_new, win_state,
      win_state, win_new, win_new, tb, wb, q_norm.reshape(1, hd), k_norm,
      *([cache_slc] * (2 * n_sel)))


def _bucket_np(d):
    d = np.maximum(d, 0)
    n_exact = N_BUCKETS // 2
    df = np.maximum(d, 1).astype(np.float64)
    large = n_exact + (np.log(df / n_exact) / math.log(MAX_DIST / n_exact)
                       * (N_BUCKETS - n_exact)).astype(np.int64)
    return np.where(d < n_exact, d, np.minimum(large, N_BUCKETS - 1))


def _dist_bias(rel_table, d, valid, shift):
    fd = rel_table.astype(F32)[_bucket_np(np.arange(MAX_DIST + 1))]
    if shift:
        fd = fd - fd[MAX_DIST:MAX_DIST + 1]
    vals = jnp.moveaxis(fd[np.clip(d, 0, MAX_DIST)], -1, 0)
    return jnp.where(jnp.asarray(valid)[None], vals, NEG)


def _overlap_np(n_cmp, n_slc, rows, cols):
    cs = np.arange(n_cmp)[:, None] * CMP_STRIDE
    ss = np.arange(n_slc)[None, :] * SLC_BLK
    ov = np.minimum(cs + CMP_BLK, ss + SLC_BLK) - np.maximum(cs, ss)
    out = np.zeros((rows, cols), np.float32)
    out[:n_cmp, :n_slc] = np.maximum(ov, 0).astype(np.float32) / CMP_BLK
    return out


def _round_up(x, m):
    return (x + m - 1) // m * m


def kernel(x_prompt, x_sample, cache_cmp_kv, cache_slc_kv, state_win_kv, state_lru_h, state_conv,
           page_table, a_norm, a_w_in, a_conv_w, a_conv_b, a_w_rg, a_b_rg, a_w_ig, a_b_ig, a_lambda,
           a_w_out, kv_norm, w_kv, k_norm, cmp_pos, w_cmp1, w_cmp2, rel_table, b_norm, b_w_in,
           b_gate_bias, b_q_norm, b_w_out):
    batch, seq, d_model = x_prompt.shape
    bs = x_sample.shape[0]
    n_a, n_b = a_norm.shape[0], b_norm.shape[0]
    d_rnn = a_w_in.shape[2] // 2
    n_kv, hd = cache_cmp_kv.shape[3], cache_cmp_kv.shape[4]
    n_heads = rel_table.shape[1]
    r_ = n_heads // n_kv
    hq = n_heads * hd
    n_sg = 2 * n_kv
    cols = n_sg * hd
    n_pages = page_table.shape[1]
    past = n_pages * PAGE_SIZE
    m = batch * seq
    assert x_sample.shape[1] == 1 and seq % QB == 0 and seq >= WINDOW and hd == LANES
    assert seq // CMP_STRIDE == LANES and past >= WINDOW and state_win_kv.shape[1] == WINDOW

    xp = x_prompt.reshape(m, d_model)
    xs = jnp.pad(x_sample.reshape(bs, d_model), ((0, SAMPLE_ROWS - bs), (0, 0)))
    pad_rows = lambda a: jnp.pad(a, ((0, SAMPLE_ROWS - bs), (0, 0)))

    p_h, p_c, s_h, s_c = [], [], [], []
    h0_p = jnp.zeros((batch, d_rnn), F32)
    c0_p = jnp.zeros((batch, CONV_W - 1, d_rnn), F32)
    for l in range(n_a):
        wrg, wig = a_w_rg[l].astype(BF16), a_w_ig[l].astype(BF16)
        lru_w = (a_conv_w[l], a_conv_b[l], wrg, a_b_rg[l], wig, a_b_ig[l], a_lambda[l])
        u, us = _matmul(_norm_cast(xp, a_norm[l]), _norm_cast(xs, a_norm[l]), a_w_in, layer=l)
        gp, hl, ct = _lru_prompt(u, batch, seq, *lru_w, h0_p, c0_p)
        c0 = jnp.pad(jnp.swapaxes(state_conv[l], 0, 1), ((0, 0), (0, SAMPLE_ROWS - bs), (0, 0)))
        gs, hs = _lru_sample(us, *lru_w, pad_rows(state_lru_h[l]), c0)
        xp, xs = _matmul(gp, gs, a_w_out, layer=l, res=xp, res_s=xs)
        p_h.append(hl.reshape(batch, d_rnn))
        p_c.append(ct)
        s_h.append(hs[:bs])
        s_c.append(jnp.concatenate([state_conv[l][:, 1:], us[:bs, None, :d_rnn]], axis=1))

    kv, kvs, p_cmp_kv, p_slc_kv, p_win_rows = _matmul(
        _norm_cast(xp, kv_norm), _norm_cast(xs, kv_norm), w_kv[None], tm=512, tn=n_kv * hd,
        rows5=(batch, seq, n_kv, hd))
    kvs = kvs[:bs]
    rows5 = lambda a, n: a.reshape(n, -1, 2, n_kv, hd)
    p_win_kv = p_win_rows[:, seq - WINDOW:]
    s_cmp_kv, s_slc_kv = rows5(kvs[:, :cols], bs), rows5(kvs[:, cols:2 * cols], bs)
    s_win_rows = rows5(kvs[:, 2 * cols:], bs)
    s_win_kv = jnp.concatenate([state_win_kv, s_win_rows], axis=1)[:, 1:]

    half = CMP_BLK // 2
    hid = w_cmp1.shape[3]
    w1ab = jnp.concatenate([w_cmp1[:, :half].reshape(2, half * hd, hid),
                            w_cmp1[:, half:].reshape(2, half * hd, hid)], axis=2).astype(BF16)
    posab = jnp.stack([cmp_pos[:, :half].reshape(2, half * hd),
                       cmp_pos[:, half:].reshape(2, half * hd)], axis=1)
    w2b = w_cmp2.astype(BF16)
    own_pages = jnp.arange(m // PAGE_SIZE, dtype=jnp.int32).reshape(batch, seq // PAGE_SIZE)
    r16_p = _page_gather(p_cmp_kv.reshape(m // PAGE_SIZE, PAGE_SIZE, 2, n_kv, hd), own_pages)
    comp_p = _compress(r16_p, posab, w1ab, w2b, k_norm[0], n_kv)
    comp_s = _compress(_page_gather(cache_cmp_kv, page_table), posab, w1ab, w2b, k_norm[0], n_kv)

    n_c = seq // CMP_STRIDE
    n_cmp_p = (seq - CMP_BLK) // CMP_STRIDE + 1
    n_slc_p = seq // SLC_BLK
    wk = WINDOW + QB
    dv = WINDOW - np.arange(wk + 1)
    vrow = _dist_bias(rel_table, dv, dv >= 0, True)
    wbias = jnp.tile(vrow, (1, QB))[:, :QB * wk].reshape(n_heads, QB, wk)
    dc = (np.arange(QB)[:, None] - (np.arange(n_c)[None, :] - (n_c - QB // CMP_STRIDE)) * CMP_STRIDE
          - (CMP_BLK - 1))
    tcfix = jnp.where(jnp.asarray(dc >= 0)[None], _dist_bias(rel_table, dc, dc >= 0, True), 0.0)
    mimp_t = jnp.asarray(_overlap_np(n_cmp_p, n_slc_p, n_c, n_slc_p).T, BF16)
    expand = jnp.asarray((np.arange(n_slc_p)[:, None] == np.arange(seq)[None, :] // SLC_BLK), BF16)

    total = past + 1
    n_cmp_s = (total - CMP_BLK) // CMP_STRIDE + 1
    n_slc_s = -(-total // SLC_BLK)
    nr_s = past // CMP_STRIDE
    assert n_cmp_s == nr_s - 1 and n_slc_s == past // SLC_BLK + 1
    dcs = past - (np.arange(nr_s) * CMP_STRIDE + CMP_BLK - 1)
    cbias_s = _dist_bias(rel_table, dcs, (dcs >= 0) & (np.arange(nr_s) < n_cmp_s), False)
    cbias_s = cbias_s.reshape(n_kv, r_, nr_s)
    mimp_s = jnp.asarray(_overlap_np(n_cmp_s, n_slc_s, nr_s, _round_up(n_slc_s, LANES)), BF16)
    dss = past - np.arange(n_slc_s * SLC_BLK)
    tb_s = _dist_bias(rel_table, dss, dss >= 0, False)
    tb_s = tb_s.reshape(n_kv, r_, n_slc_s, SLC_BLK).transpose(0, 2, 1, 3)
    dws = WINDOW - np.arange(WINDOW + LANES)
    wb_s = _dist_bias(rel_table, dws, dws >= 0, False).reshape(n_kv, r_, WINDOW + LANES)

    slc_new = kvs[:, cols:2 * cols].reshape(bs, n_sg, 1, hd)
    win_new = kvs[:, 2 * cols:].reshape(bs, n_sg, 1, hd)
    slab_pad = ((0, 0), (0, 0), (0, LANES - 3 * r_))
    for l in range(n_b):
        w_qg = b_w_in[l][:, :2 * hq].astype(BF16)
        w_bg = jnp.pad(b_w_in[l][:, 2 * hq:], ((0, 0), (0, LANES - 3 * n_heads)))
        w_bg = jnp.pad(w_bg[:, :3 * n_heads].reshape(d_model, n_kv, 3 * r_), slab_pad)
        w_bg = w_bg.reshape(d_model, n_kv * LANES)
        gbias = b_gate_bias[l]
        gb_slab = jnp.pad(gbias.reshape(n_kv, 1, 3 * r_), slab_pad)

        xn, xns = _norm_cast(xp, b_norm[l]), _norm_cast(xs, b_norm[l])
        u, us = _matmul(xn, xns, w_qg[None])
        bgs, bgs_s = _matmul(xn, xns, w_bg[None])
        og = _attn_prompt(u, bgs, gb_slab, kv, comp_p, wbias, tcfix, mimp_t, expand,
                          b_q_norm[l], k_norm, batch, seq, n_kv, n_heads)

        us = us[:bs]
        q3 = us[:, :hq].reshape(bs, n_heads, hd)
        gate3 = us[:, hq:].reshape(bs, n_heads, hd)
        bg3 = bgs_s[:bs].reshape(bs, n_kv, LANES)[:, :, :3 * r_].reshape(bs, n_heads, 3)
        o_cmp, sel = _attn_sample_cmp(q3, comp_s, cbias_s, mimp_s, b_q_norm[l], n_kv, n_slc_s)
        os_ = _attn_sample(sel[:, :, 0, :TOP_N].reshape(-1), page_table, q3, gate3, bg3,
                           gbias.reshape(n_heads, 3), o_cmp, cache_slc_kv, slc_new, state_win_kv,
                           win_new, tb_s, wb_s, b_q_norm[l], k_norm, n_kv, n_slc_s)
        xp, xs = _matmul(og, pad_rows(os_.reshape(bs, hq).astype(BF16)), b_w_out, layer=l,
                         res=xp, res_s=xs)

    return (xp.reshape(batch, seq, d_model), xs[:bs].reshape(bs, 1, d_model),
            p_cmp_kv, p_slc_kv, p_win_kv, jnp.stack(p_h), jnp.stack(p_c),
            s_cmp_kv, s_slc_kv, s_win_kv, jnp.stack(s_h), jnp.stack(s_c))
```

```python
import functools
import math

import numpy as np
import jax
import jax.numpy as jnp
from jax import lax
from jax.experimental import pallas as pl
from jax.experimental.pallas import tpu as pltpu

F32 = jnp.float32
BF16 = jnp.bfloat16

EPS = 1e-6
NEG = -1e30
BIG = 1e30
LOWEST = -3e38
LRU_C = 8.0
CONV_W = 4
CMP_BLK = 32
CMP_STRIDE = 16
SLC_BLK = 64
TOP_N = 16
WINDOW = 512
N_BUCKETS = 32
MAX_DIST = 128
PAGE_SIZE = 128
QB = 128
LANES = 128
SUBLANES = 8
SEG_PAD = 8
SAMPLE_ROWS = 16
VMEM_LIMIT = 48 * 1024 * 1024


def _cparams(*sem):
    return pltpu.CompilerParams(dimension_semantics=sem, vmem_limit_bytes=VMEM_LIMIT)


def _dot(a, b):
    return jnp.dot(a, b, preferred_element_type=F32)


def _dot_nt(a, b):
    return lax.dot_general(a, b, (((1,), (1,)), ((), ())), preferred_element_type=F32)


def _rms(x, g):
    return x * lax.rsqrt(jnp.mean(x * x, axis=-1, keepdims=True) + EPS) * g


def _sigmoid(x):
    return jax.nn.sigmoid(x)


def _silu(x):
    return x * jax.nn.sigmoid(x)


def _norm_kernel(x_ref, g_ref, o_ref):
    o_ref[...] = _rms(x_ref[...], g_ref[...]).astype(o_ref.dtype)


def _norm_cast(x, g):
    m, d = x.shape
    tm = min(m, 512)
    return pl.pallas_call(
        _norm_kernel,
        grid=(m // tm,),
        in_specs=[pl.BlockSpec((tm, d), lambda i: (i, 0)),
                  pl.BlockSpec((1, d), lambda i: (0, 0))],
        out_specs=pl.BlockSpec((tm, d), lambda i: (i, 0)),
        out_shape=jax.ShapeDtypeStruct((m, d), BF16),
        compiler_params=_cparams("parallel"),
        name="norm_cast",
    )(x, g.reshape(1, d))


def _mm_kernel(a_ref, as_ref, w_ref, *rest, has_res, n_rows5, q_tiles):
    if q_tiles:
        qg_ref, rest = rest[0], rest[1:]
    if has_res:
        r_ref, rs_ref, o_ref, os_ref = rest[:4]
    else:
        o_ref, os_ref = rest[:2]
    wb_scr = rest[-1]
    rows5_refs = rest[-1 - n_rows5:-1]
    wb_ref = wb_scr if w_ref.dtype != BF16 else w_ref

    @pl.when(pl.program_id(1) == 0)
    def _():
        if w_ref.dtype != BF16:
            wb_scr[...] = w_ref[...].astype(BF16)
        acc_s = _dot(as_ref[...], wb_ref[...])
        os_ref[...] = rs_ref[...] + acc_s if has_res else acc_s

    acc = _dot(a_ref[...], wb_ref[...])
    if q_tiles:
        hd = qg_ref.shape[1]

        @pl.when(pl.program_id(0) < q_tiles)
        def _():
            for h in range(acc.shape[1] // hd):
                x = acc[:, h * hd:(h + 1) * hd]
                o_ref[:, h * hd:(h + 1) * hd] = (
                    x * lax.rsqrt(jnp.mean(x * x, axis=-1, keepdims=True) + EPS) * qg_ref[...])

        @pl.when(pl.program_id(0) >= q_tiles)
        def _():
            o_ref[...] = _silu(acc)
    else:
        o_ref[...] = r_ref[...] + acc if has_res else acc

    for br, r5_ref in enumerate(rows5_refs):
        @pl.when(pl.program_id(0) // 2 == br)
        def _(r5_ref=r5_ref):
            n_grp, hd = r5_ref.shape[3], r5_ref.shape[4]
            for g in range(n_grp):
                r5_ref[0, :, 0, g, :] = acc[:, g * hd:(g + 1) * hd]


def _matmul(a, a_s, w, layer=0, n=None, res=None, res_s=None, tm=1024, tn=512, rows5=None,
            q_gain=None):
    m, k = a.shape
    ms = a_s.shape[0]
    n = w.shape[2] if n is None else n
    tm = min(m, tm)
    tn = min(n, tn)
    has_res = res is not None
    in_specs = [pl.BlockSpec((tm, k), lambda j, i: (i, 0)),
                pl.BlockSpec((ms, k), lambda j, i: (0, 0)),
                pl.BlockSpec((None, k, tn), lambda j, i: (layer, 0, j))]
    args = [a, a_s, w]
    q_tiles = 0
    if q_gain is not None:
        assert not has_res and rows5 is None and (n // 2) % tn == 0 and tn % q_gain.shape[1] == 0
        q_tiles = n // 2 // tn
        in_specs.append(pl.BlockSpec(q_gain.shape, lambda j, i: (0, 0)))
        args.append(q_gain)
    if has_res:
        in_specs += [pl.BlockSpec((tm, tn), lambda j, i: (i, j)),
                     pl.BlockSpec((ms, tn), lambda j, i: (0, j))]
        args += [res, res_s]
    out_specs = [pl.BlockSpec((tm, tn), lambda j, i: (i, j)),
                 pl.BlockSpec((ms, tn), lambda j, i: (0, j))]
    out_shape = [jax.ShapeDtypeStruct((m, n), F32), jax.ShapeDtypeStruct((ms, n), F32)]
    n_rows5 = 0
    if rows5 is not None:
        batch, seq, n_kv, hd = rows5
        n_rows5 = n // (2 * tn)
        nt = seq // tm
        assert tn == n_kv * hd and seq % tm == 0 and n % (2 * tn) == 0

        def rows5_map(br):
            def index_map(j, i):
                active, before = j // 2 == br, j < 2 * br
                park = lambda first, last: jnp.where(before, first, last)
                return (jnp.where(active, i // nt, park(0, batch - 1)),
                        jnp.where(active, i % nt, park(0, nt - 1)),
                        jnp.where(active, j % 2, park(0, 1)), 0, 0)
            return index_map

        for br in range(n_rows5):
            out_specs.append(pl.BlockSpec((1, tm, 1, n_kv, hd), rows5_map(br)))
            out_shape.append(jax.ShapeDtypeStruct((batch, seq, 2, n_kv, hd), F32))
    return pl.pallas_call(
        functools.partial(_mm_kernel, has_res=has_res, n_rows5=n_rows5, q_tiles=q_tiles),
        grid=(n // tn, m // tm),
        in_specs=in_specs,
        out_specs=out_specs,
        out_shape=out_shape,
        scratch_shapes=[pltpu.VMEM((k, tn) if w.dtype != BF16 else (SUBLANES, LANES), BF16)],
        compiler_params=_cparams("arbitrary", "arbitrary"),
        name="matmul",
    )(*args)


def _lru_coeffs(xc, wrg_ref, wig_ref, brg, big, lam, pos0_row):
    rows, cb = xc.shape
    bs = wrg_ref.shape[-1]
    xcb = xc.astype(BF16)
    r_parts, i_parts = [], []
    for n in range(cb // bs):
        xs = xcb[:, n * bs:(n + 1) * bs]
        r_parts.append(_dot(xs, wrg_ref[n]))
        i_parts.append(_dot(xs, wig_ref[n]))
    r = _sigmoid(jnp.concatenate(r_parts, axis=1) + brg)
    i = _sigmoid(jnp.concatenate(i_parts, axis=1) + big)
    nl = -lam
    softplus = jnp.maximum(nl, 0.0) + jnp.log1p(jnp.exp(-jnp.abs(nl)))
    log_a = -LRU_C * r * softplus
    a = jnp.exp(log_a)
    mult = jnp.sqrt(-jnp.tanh(log_a) * (a * a + 1.0))
    if pos0_row is not None:
        row = lax.broadcasted_iota(jnp.int32, (rows, cb), 0)
        mult = jnp.where(row == pos0_row, 1.0, mult)
    return a, mult * i * xc


def _lru_prompt_kernel(xb_ref, gate_ref, cw_ref, cb_ref, wrg_ref, brg_ref, wig_ref, big_ref,
                       lam_ref, h0_ref, c0_ref, g_ref, hl_ref, ct_ref, st_scr, un_scr, tail_scr, h_scr):
    tc = pl.program_id(2)
    tt, cb = xb_ref.shape
    nlb = cb // LANES
    seg = tt // SUBLANES
    tail = (CONV_W - 1) * SUBLANES

    @pl.when(tc == 0)
    def _():
        h_scr[...] = jnp.broadcast_to(h0_ref[0], (SUBLANES, cb))
        for i in range(CONV_W - 1):
            tail_scr[i * SUBLANES:(i + 1) * SUBLANES, :] = jnp.broadcast_to(c0_ref[0, i:i + 1, :],
                                                                             (SUBLANES, cb))

    pitch = seg + SEG_PAD

    def to_segments(src_ref, slot):
        cols = []
        for lb in range(nlb):
            for j in range(SUBLANES):
                st_scr[slot, lb, j * pitch:j * pitch + seg, :] = src_ref[j * seg:(j + 1) * seg,
                                                                         lb * LANES:(lb + 1) * LANES]
            cols.append(jnp.concatenate(
                [st_scr[slot, lb, pl.ds(k, SUBLANES, stride=pitch), :] for k in range(seg)], axis=0))
        return jnp.concatenate(cols, axis=1)

    x = to_segments(xb_ref, 0)

    sub = lax.broadcasted_iota(jnp.int32, (SUBLANES, cb), 0)
    heads = []
    for i in range(CONV_W - 1):
        cur = x[tt - tail + i * SUBLANES:tt - tail + (i + 1) * SUBLANES]
        prev = tail_scr[i * SUBLANES:(i + 1) * SUBLANES, :]
        heads.append(pltpu.roll(jnp.where(sub == SUBLANES - 1, prev, cur), 1, axis=0))
    tail_scr[...] = x[tt - tail:tt]
    w = cw_ref[...]
    xc = cb_ref[...]
    for k in range(CONV_W - 1):
        m = CONV_W - 1 - k
        xc = xc + w[k:k + 1] * jnp.concatenate(heads[CONV_W - 1 - m:] + [x[0:tt - m * SUBLANES]], axis=0)
    xc = xc + w[CONV_W - 1:CONV_W] * x

    a, b_in = _lru_coeffs(xc, wrg_ref, wig_ref, brg_ref[...], big_ref[...], lam_ref[...],
                           jnp.where(tc == 0, 0, -1))

    blk = lambda v, k: v[k * SUBLANES:(k + 1) * SUBLANES]
    e, p = blk(b_in, 0), blk(a, 0)
    for k in range(1, seg):
        e = blk(a, k) * e + blk(b_in, k)
        p = blk(a, k) * p
    carry = h_scr[0:1, :]
    carries = [carry]
    for j in range(SUBLANES - 1):
        carry = p[j:j + 1] * carry + e[j:j + 1]
        carries.append(carry)
    h = jnp.concatenate(carries, axis=0)
    hs = []
    for k in range(seg):
        h = blk(a, k) * h + blk(b_in, k)
        hs.append(h)
    h_scr[...] = jnp.broadcast_to(h[SUBLANES - 1:SUBLANES], (SUBLANES, cb))
    g = jnp.concatenate(hs, axis=0) * _silu(to_segments(gate_ref, 1))

    for lb in range(nlb):
        for k in range(seg):
            un_scr[lb, pl.ds(k, SUBLANES, stride=pitch), :] = g[k * SUBLANES:(k + 1) * SUBLANES,
                                                                lb * LANES:(lb + 1) * LANES]
        for j in range(SUBLANES):
            g_ref[j * seg:(j + 1) * seg, lb * LANES:(lb + 1) * LANES] = (
                un_scr[lb, j * pitch:j * pitch + seg, :].astype(g_ref.dtype))

    @pl.when(tc == pl.num_programs(2) - 1)
    def _():
        hl_ref[0] = h[SUBLANES - 1:SUBLANES]
        ct_ref[0] = xb_ref[tt - (CONV_W - 1):tt, :]


def _lru_prompt(u, batch, seq, cw, cb_, wrg, brg, wig, big, lam, h0, c0):
    m, c2 = u.shape
    c = c2 // 2
    cb = min(c, 1024)
    tt = min(seq, 256)
    nt = seq // tt
    ncb = c // cb
    bs = wrg.shape[-1]
    nb = cb // bs
    vec = lambda: pl.BlockSpec((1, cb), lambda b, j, t: (0, j))
    return pl.pallas_call(
        _lru_prompt_kernel,
        grid=(batch, ncb, nt),
        in_specs=[
            pl.BlockSpec((tt, cb), lambda b, j, t: (b * nt + t, j)),
            pl.BlockSpec((tt, cb), lambda b, j, t: (b * nt + t, ncb + j)),
            pl.BlockSpec((CONV_W, cb), lambda b, j, t: (0, j)),
            vec(),
            pl.BlockSpec((nb, bs, bs), lambda b, j, t: (j, 0, 0)),
            vec(),
            pl.BlockSpec((nb, bs, bs), lambda b, j, t: (j, 0, 0)),
            vec(),
            vec(),
            pl.BlockSpec((1, 1, cb), lambda b, j, t: (b, 0, j)),
            pl.BlockSpec((1, CONV_W - 1, cb), lambda b, j, t: (b, 0, j)),
        ],
        out_specs=[
            pl.BlockSpec((tt, cb), lambda b, j, t: (b * nt + t, j)),
            pl.BlockSpec((1, 1, cb), lambda b, j, t: (b, 0, j)),
            pl.BlockSpec((1, CONV_W - 1, cb), lambda b, j, t: (b, 0, j)),
        ],
        out_shape=[
            jax.ShapeDtypeStruct((m, c), BF16),
            jax.ShapeDtypeStruct((batch, 1, c), F32),
            jax.ShapeDtypeStruct((batch, CONV_W - 1, c), F32),
        ],
        scratch_shapes=[pltpu.VMEM((2, cb // LANES, tt + SUBLANES * SEG_PAD, LANES), F32),
                        pltpu.VMEM((cb // LANES, tt + SUBLANES * SEG_PAD, LANES), F32),
                        pltpu.VMEM(((CONV_W - 1) * SUBLANES, cb), F32),
                        pltpu.VMEM((SUBLANES, cb), F32)],
        compiler_params=_cparams("parallel", "parallel", "arbitrary"),
        name="lru_prompt",
    )(u, u, cw, cb_.reshape(1, c), wrg, brg.reshape(1, c), wig, big.reshape(1, c),
      lam.reshape(1, c), h0.reshape(batch, 1, c), c0)


def _lru_sample_kernel(xb_ref, gate_ref, cw_ref, cb_ref, wrg_ref, brg_ref, wig_ref, big_ref,
                       lam_ref, h0_ref, c0_ref, g_ref, h_ref):
    w = cw_ref[...]
    xc = cb_ref[...]
    for k in range(CONV_W - 1):
        xc = xc + w[k:k + 1] * c0_ref[k]
    xc = xc + w[CONV_W - 1:CONV_W] * xb_ref[...]
    a, b_in = _lru_coeffs(xc, wrg_ref, wig_ref, brg_ref[...], big_ref[...], lam_ref[...], None)
    h = a * h0_ref[...] + b_in
    h_ref[...] = h
    g_ref[...] = (h * _silu(gate_ref[...])).astype(g_ref.dtype)


def _lru_sample(u, cw, cb_, wrg, brg, wig, big, lam, h0, c0):
    rows, c2 = u.shape
    c = c2 // 2
    cb = min(c, 1024)
    ncb = c // cb
    bs = wrg.shape[-1]
    nb = cb // bs
    vec = lambda: pl.BlockSpec((1, cb), lambda j: (0, j))
    return pl.pallas_call(
        _lru_sample_kernel,
        grid=(ncb,),
        in_specs=[
            pl.BlockSpec((rows, cb), lambda j: (0, j)),
            pl.BlockSpec((rows, cb), lambda j: (0, ncb + j)),
            pl.BlockSpec((CONV_W, cb), lambda j: (0, j)),
            vec(),
            pl.BlockSpec((nb, bs, bs), lambda j: (j, 0, 0)),
            vec(),
            pl.BlockSpec((nb, bs, bs), lambda j: (j, 0, 0)),
            vec(),
            vec(),
            pl.BlockSpec((rows, cb), lambda j: (0, j)),
            pl.BlockSpec((CONV_W - 1, rows, cb), lambda j: (0, 0, j)),
        ],
        out_specs=[pl.BlockSpec((rows, cb), lambda j: (0, j)),
                   pl.BlockSpec((rows, cb), lambda j: (0, j))],
        out_shape=[jax.ShapeDtypeStruct((rows, c), BF16),
                   jax.ShapeDtypeStruct((rows, c), F32)],
        compiler_params=_cparams("parallel"),
        name="lru_sample",
    )(u, u, cw, cb_.reshape(1, c), wrg, brg.reshape(1, c), wig, big.reshape(1, c),
      lam.reshape(1, c), h0, c0)


GATHER_PAGES = 4


def _page_gather_kernel(pt_ref, *refs):
    del pt_ref
    c_refs, o_ref, slab_scr = refs[:GATHER_PAGES], refs[GATHER_PAGES], refs[GATHER_PAGES + 1]
    n_kv, hd = c_refs[0].shape[3], c_refs[0].shape[4]
    rows = PAGE_SIZE // CMP_STRIDE
    for q, c_ref in enumerate(c_refs):
        for sg in range(2 * n_kv):
            slab_scr[q, sg] = c_ref[0, :, sg // n_kv, sg % n_kv, :]
            for l in range(CMP_STRIDE):
                o_ref[0, sg, q * rows:(q + 1) * rows, l * hd:(l + 1) * hd] = (
                    slab_scr[q, sg, pl.ds(l, rows, stride=CMP_STRIDE), :])


def _page_gather(pages, page_table):
    bs, n_pages = page_table.shape
    n_kv, hd = pages.shape[3], pages.shape[4]
    n_sg = 2 * n_kv
    rows = PAGE_SIZE // CMP_STRIDE
    assert n_pages % GATHER_PAGES == 0

    def page_map(q):
        return lambda b, j, pt: (pt[b * n_pages + j * GATHER_PAGES + q], 0, 0, 0, 0)

    return pl.pallas_call(
        _page_gather_kernel,
        grid_spec=pltpu.PrefetchScalarGridSpec(
            num_scalar_prefetch=1,
            grid=(bs, n_pages // GATHER_PAGES),
            in_specs=[pl.BlockSpec((1, PAGE_SIZE, 2, n_kv, hd), page_map(q))
                      for q in range(GATHER_PAGES)],
            out_specs=pl.BlockSpec((1, n_sg, GATHER_PAGES * rows, CMP_STRIDE * hd),
                                   lambda b, j, pt: (b, 0, j, 0)),
            scratch_shapes=[pltpu.VMEM((GATHER_PAGES, n_sg, PAGE_SIZE, hd), F32)],
        ),
        out_shape=jax.ShapeDtypeStruct((bs, n_sg, n_pages * rows, CMP_STRIDE * hd), F32),
        compiler_params=_cparams("parallel", "arbitrary"),
        name="page_gather",
    )(page_table.reshape(-1), *([pages] * GATHER_PAGES))


def _compress_kernel(r_ref, pos_ref, w1_ref, w2_ref, kn_ref, o_ref, q_scr, *, n_kv):
    sg = pl.program_id(1)
    rows = r_ref[0, 0]
    nr = rows.shape[0]
    hid = w1_ref.shape[2] // 2
    xa = (rows + pos_ref[0, 0:1]).astype(BF16)
    xb = (rows + pos_ref[0, 1:2]).astype(BF16)
    w1 = w1_ref[0]
    p = _dot(xa, w1[:, :hid])
    q_scr[0:nr, :] = _dot(xb, w1[:, hid:])
    q_scr[nr:nr + 8, :] = jnp.zeros((8, hid), F32)
    pre = p + q_scr[1:nr + 1, :]
    comp = _dot(_silu(pre).astype(BF16), w2_ref[0])
    normed = _rms(comp, kn_ref[...])
    o_ref[0, 0] = jnp.where(sg < n_kv, normed, comp).astype(o_ref.dtype)


def _compress(r16, posab, w1ab, w2, k_norm0, n_kv):
    b, n_sg, nr, kk = r16.shape
    hid2 = w1ab.shape[2]
    hd = w2.shape[2]
    return pl.pallas_call(
        functools.partial(_compress_kernel, n_kv=n_kv),
        grid=(b, n_sg),
        in_specs=[
            pl.BlockSpec((1, 1, nr, kk), lambda i, s: (i, s, 0, 0)),
            pl.BlockSpec((1, 2, kk), lambda i, s: (s // n_kv, 0, 0)),
            pl.BlockSpec((1, kk, hid2), lambda i, s: (s // n_kv, 0, 0)),
            pl.BlockSpec((1, hid2 // 2, hd), lambda i, s: (s // n_kv, 0, 0)),
            pl.BlockSpec((1, hd), lambda i, s: (0, 0)),
        ],
        out_specs=pl.BlockSpec((1, 1, nr, hd), lambda i, s: (i, s, 0, 0)),
        out_shape=jax.ShapeDtypeStruct((b, n_sg, nr, hd), BF16),
        scratch_shapes=[pltpu.VMEM((nr + 8, hid2 // 2), F32)],
        compiler_params=_cparams("parallel", "arbitrary"),
        name="compress",
    )(r16, posab, w1ab, w2, k_norm0.reshape(1, hd))


def _attn_prompt_kernel(uq_ref, ug_ref, bg_ref, gb_ref, ks_ref, vs_ref, kw_ref, vw_ref,
                        kc_ref, vc_ref, wb_ref, tc_ref, mit_ref, ex_ref, kn_ref, og_ref,
                        ks_scr, vs_scr, kw_scr, vw_scr, madd_scr, lg_scr, p_scr, m_scr,
                        *, n_heads_grp, n_slc):
    r_ = n_heads_grp
    c = pl.program_id(2)
    t0 = c * QB
    seq, hd = ks_ref.shape
    rq = r_ * QB
    wk = WINDOW + QB
    n_wt = wk // QB

    @pl.when(c == 0)
    def _():
        ks_scr[...] = _rms(ks_ref[...], kn_ref[1:2]).astype(BF16)
        vs_scr[:, 0:hd] = vs_ref[...].astype(BF16)
        vs_scr[:, hd:2 * hd] = jnp.ones((seq, hd), BF16)
        kw_scr[0:WINDOW, :] = jnp.zeros((WINDOW, hd), BF16)
        vw_scr[0:WINDOW, :] = jnp.zeros((WINDOW, 2 * hd), BF16)
        kw_scr[WINDOW:WINDOW + seq, :] = _rms(kw_ref[...], kn_ref[2:3]).astype(BF16)
        vw_scr[WINDOW:WINDOW + seq, 0:hd] = vw_ref[...].astype(BF16)
        vw_scr[WINDOW:WINDOW + seq, hd:2 * hd] = jnp.ones((seq, hd), BF16)
        lg_scr[...] = jnp.zeros(lg_scr.shape, F32)

    uq = uq_ref[...]
    qb = jnp.concatenate([uq[:, r * hd:(r + 1) * hd] for r in range(r_)],
                         axis=0).astype(BF16)

    n_c = kc_ref.shape[2]
    bias_c = pltpu.roll(tc_ref[...], (c * (QB // CMP_STRIDE) + QB // CMP_STRIDE) % n_c, axis=2)
    lc = _dot_nt(qb, kc_ref[0, 0]).reshape(r_, QB, n_c) + bias_c
    tq = t0 + lax.broadcasted_iota(jnp.int32, (QB, n_c), 0)
    nn = lax.broadcasted_iota(jnp.int32, (QB, n_c), 1)
    c_ok = (tq - (nn * CMP_STRIDE + (CMP_BLK - 1)) >= 0)[None]
    lc = jnp.where(c_ok, lc, NEG)
    e = jnp.exp(lc - jnp.max(lc, axis=-1, keepdims=True))
    pc = jnp.where(c_ok, e / jnp.sum(e, axis=-1, keepdims=True), 0.0)
    pcb = pc.reshape(rq, n_c).astype(BF16)
    o_cmp = _dot(pcb, vc_ref[0, 0])
    imp_all = _dot_nt(mit_ref[...], pcb)
    imp = imp_all[:, 0:QB]
    for r in range(1, r_):
        imp = imp + imp_all[:, r * QB:(r + 1) * QB]

    jj = lax.broadcasted_iota(jnp.int32, (n_slc, QB), 0)
    cur = (t0 + lax.broadcasted_iota(jnp.int32, (n_slc, QB), 1)) // SLC_BLK
    forced = (jj == 0) | (jj == cur) | (jj == cur - 1)
    allowed = jj <= cur
    score = jnp.where(forced, BIG, jnp.where(allowed, imp, NEG))
    rank = jnp.zeros((n_slc, QB), F32)
    for k in range(n_slc):
        sk = score[k:k + 1, :]
        rank = rank + jnp.where(sk > score, 1.0, jnp.where((sk == score) & (k < jj), 1.0, 0.0))
    sel = jnp.where((rank < min(TOP_N, n_slc)) & allowed, 1.0, 0.0).astype(BF16)
    selx = lax.dot_general(sel, ex_ref[...], (((0,), (0,)), ((), ())),
                           preferred_element_type=F32)
    tk = 2 * QB
    for kp in range(seq // tk):
        madd_scr[kp] = (selx[:, kp * tk:(kp + 1) * tk] - 1.0) * BIG

    m_scr[...] = jnp.full((rq, tk), NEG, F32)
    near1 = wb_ref[:, :, WINDOW - QB:WINDOW]
    near0 = wb_ref[:, :, WINDOW:WINDOW + QB]

    def logits_pair(kp, near):
        k0 = pl.multiple_of(kp * tk, tk)
        s3 = _dot_nt(qb, ks_scr[pl.ds(k0, tk), :]).reshape(r_, QB, tk) + madd_scr[kp][None]
        if near:
            halves = []
            for h in range(2):
                dist = c - (2 * kp + h)
                bias = jnp.where(dist == 0, near0, jnp.where(dist == 1, near1, 0.0))
                halves.append(s3[:, :, h * QB:(h + 1) * QB] + bias)
            s3 = jnp.concatenate(halves, axis=2)
        s = s3.reshape(rq, tk)
        lg_scr[kp] = s
        m_scr[...] = jnp.maximum(m_scr[...], s)

    def far_body(kp, carry):
        logits_pair(kp, False)
        return carry

    last = c // 2
    prev = jnp.maximum(last - 1, 0)
    logits_pair(last, True)
    logits_pair(prev, True)
    lax.fori_loop(0, prev, far_body, 0)

    m_scr[...] = jnp.broadcast_to(jnp.max(m_scr[...], axis=-1, keepdims=True), (rq, tk))
    for kp in range(seq // tk):
        p = jnp.where(kp <= last, jnp.exp(lg_scr[kp] - m_scr[...]), 0.0)
        p_scr[:, kp * tk:(kp + 1) * tk] = p.astype(BF16)
    acc = _dot(p_scr[...], vs_scr[...])
    o_slc = acc[:, 0:hd] / acc[:, hd:2 * hd]

    w0 = pl.multiple_of(t0, QB)
    sw = _dot_nt(qb, kw_scr[pl.ds(w0, wk), :])
    tiles = []
    for j in range(n_wt):
        s = sw[:, j * QB:(j + 1) * QB]
        if j == 0 or j >= n_wt - 2:
            s = s + wb_ref[:, :, j * QB:(j + 1) * QB].reshape(rq, QB)
        if j < n_wt - 1:
            s = s + jnp.where(c + j >= n_wt - 1, 0.0, NEG)
        tiles.append(s)
    mw = tiles[0]
    for s in tiles[1:]:
        mw = jnp.maximum(mw, s)
    mw = jnp.max(mw, axis=-1, keepdims=True)
    pw = jnp.concatenate([jnp.exp(s - mw).astype(BF16) for s in tiles], axis=1)
    accw = _dot(pw, vw_scr[pl.ds(w0, wk), :])
    o_win = accw[:, 0:hd] / accw[:, hd:2 * hd]

    bg = _sigmoid(bg_ref[...] + gb_ref[0])
    ug = ug_ref[...]
    for r in range(r_):
        rows = slice(r * QB, (r + 1) * QB)
        o = (bg[:, 3 * r:3 * r + 1] * o_cmp[rows] + bg[:, 3 * r + 1:3 * r + 2] * o_slc[rows]
             + bg[:, 3 * r + 2:3 * r + 3] * o_win[rows])
        og_ref[:, r * hd:(r + 1) * hd] = (o * ug[:, r * hd:(r + 1) * hd]).astype(og_ref.dtype)


def _attn_prompt(u, bgs, gbias, kv, comp, wbias, tcfix, mimp_t, expand, k_norm,
                 batch, seq, n_kv, n_heads):
    m = u.shape[0]
    hd = k_norm.shape[1]
    r_ = n_heads // n_kv
    hq = n_heads * hd
    nq = seq // QB
    n_c = comp.shape[2]
    n_slc = seq // SLC_BLK
    rq = r_ * QB
    wk = WINDOW + QB
    n_sg = 2 * n_kv
    kv_spec = lambda col0: pl.BlockSpec((seq, hd), lambda b, g, c: (b, col0 + g))
    return pl.pallas_call(
        functools.partial(_attn_prompt_kernel, n_heads_grp=r_, n_slc=n_slc),
        grid=(batch, n_kv, nq),
        in_specs=[
            pl.BlockSpec((QB, r_ * hd), lambda b, g, c: (b * nq + c, g)),
            pl.BlockSpec((QB, r_ * hd), lambda b, g, c: (b * nq + c, n_kv + g)),
            pl.BlockSpec((QB, LANES), lambda b, g, c: (b * nq + c, g)),
            pl.BlockSpec((1, 1, LANES), lambda b, g, c: (g, 0, 0)),
            kv_spec(n_sg), kv_spec(n_sg + n_kv), kv_spec(2 * n_sg), kv_spec(2 * n_sg + n_kv),
            pl.BlockSpec((1, 1, n_c, hd), lambda b, g, c: (b, g, 0, 0)),
            pl.BlockSpec((1, 1, n_c, hd), lambda b, g, c: (b, n_kv + g, 0, 0)),
            pl.BlockSpec((r_, QB, wk), lambda b, g, c: (g, 0, 0)),
            pl.BlockSpec((r_, QB, n_c), lambda b, g, c: (g, 0, 0)),
            pl.BlockSpec((n_slc, n_c), lambda b, g, c: (0, 0)),
            pl.BlockSpec((n_slc, seq), lambda b, g, c: (0, 0)),
            pl.BlockSpec((3, hd), lambda b, g, c: (0, 0)),
        ],
        out_specs=pl.BlockSpec((QB, r_ * hd), lambda b, g, c: (b * nq + c, g)),
        out_shape=jax.ShapeDtypeStruct((m, hq), BF16),
        scratch_shapes=[
            pltpu.VMEM((seq, hd), BF16), pltpu.VMEM((seq, 2 * hd), BF16),
            pltpu.VMEM((seq + WINDOW, hd), BF16), pltpu.VMEM((seq + WINDOW, 2 * hd), BF16),
            pltpu.VMEM((seq // (2 * QB), QB, 2 * QB), F32),
            pltpu.VMEM((seq // (2 * QB), rq, 2 * QB), F32),
            pltpu.VMEM((rq, seq), BF16),
            pltpu.VMEM((rq, 2 * QB), F32),
        ],
        compiler_params=_cparams("arbitrary", "arbitrary", "arbitrary"),
        name="attn_prompt",
    )(u, u, bgs, gbias, kv, kv, kv, kv, comp, comp, wbias, tcfix, mimp_t, expand, k_norm)


def _attn_sample_cmp_kernel(q_ref, kc_ref, vc_ref, cb_ref, mi_ref, qn_ref, oc_ref, sel_ref,
                            *, n_slc):
    r_, hd = q_ref.shape[1], q_ref.shape[2]
    qb = (_rms(q_ref[0], qn_ref[...]) * hd ** -0.5).astype(BF16)
    lc = _dot_nt(qb, kc_ref[0, 0]) + cb_ref[0]
    e = jnp.exp(lc - jnp.max(lc, axis=-1, keepdims=True))
    pcb = (e / jnp.sum(e, axis=-1, keepdims=True)).astype(BF16)
    oc_ref[0] = _dot(pcb, vc_ref[0, 0])
    nsp = mi_ref.shape[1]
    imp = jnp.sum(_dot(pcb, mi_ref[...]), axis=0, keepdims=True)

    jl = lax.broadcasted_iota(jnp.int32, (1, nsp), 1)
    forced = (jl == 0) | (jl == n_slc - 1) | (jl == n_slc - 2)
    score = jnp.where(jl < n_slc, jnp.where(forced, BIG, imp), LOWEST)
    s_rows = jnp.broadcast_to(score, (nsp, nsp))
    kk = lax.broadcasted_iota(jnp.int32, (nsp, nsp), 0)
    jj = lax.broadcasted_iota(jnp.int32, (nsp, nsp), 1)
    s_col = jnp.sum(jnp.where(kk == jj, s_rows, 0.0), axis=1, keepdims=True)
    ahead = jnp.where(s_col > s_rows, 1.0, jnp.where((s_col == s_rows) & (kk < jj), 1.0, 0.0))
    rank = jnp.sum(ahead, axis=0, keepdims=True)
    lane = lax.broadcasted_iota(jnp.int32, (1, LANES), 1)
    out = jnp.zeros((1, LANES), F32)
    jf = jl.astype(F32)
    for k in range(min(TOP_N, n_slc)):
        idx_k = jnp.sum(jnp.where(rank == k, jf, 0.0), axis=1, keepdims=True)
        out = out + jnp.where(lane == k, idx_k, 0.0)
    sel_ref[0, 0] = jnp.broadcast_to(out, (8, LANES)).astype(jnp.int32)


def _attn_sample_cmp(q3, comp, cbias, mimp, q_norm, n_kv, n_slc):
    bs, n_heads, hd = q3.shape
    r_ = n_heads // n_kv
    nr = comp.shape[2]
    nsp = mimp.shape[1]
    return pl.pallas_call(
        functools.partial(_attn_sample_cmp_kernel, n_slc=n_slc),
        grid=(bs, n_kv),
        in_specs=[
            pl.BlockSpec((1, r_, hd), lambda b, g: (b, g, 0)),
            pl.BlockSpec((1, 1, nr, hd), lambda b, g: (b, g, 0, 0)),
            pl.BlockSpec((1, 1, nr, hd), lambda b, g: (b, n_kv + g, 0, 0)),
            pl.BlockSpec((1, r_, nr), lambda b, g: (g, 0, 0)),
            pl.BlockSpec((nr, nsp), lambda b, g: (0, 0)),
            pl.BlockSpec((1, hd), lambda b, g: (0, 0)),
        ],
        out_specs=[pl.BlockSpec((1, r_, hd), lambda b, g: (b, g, 0)),
                   pl.BlockSpec((1, 1, 8, LANES), lambda b, g: (b, g, 0, 0))],
        out_shape=[jax.ShapeDtypeStruct((bs, n_heads, hd), F32),
                   jax.ShapeDtypeStruct((bs, n_kv, 8, LANES), jnp.int32)],
        compiler_params=_cparams("parallel", "parallel"),
        name="attn_sample_cmp",
    )(q3, comp, comp, cbias, mimp, q_norm.reshape(1, hd))


def _attn_sample_kernel(sel_ref, pt_ref, q_ref, gate_ref, bg_ref, gb_ref, oc_ref, ksn_ref, vsn_ref,
                        kw_ref, vw_ref, kwn_ref, vwn_ref, tb_ref, wb_ref, qn_ref, kn_ref, *rest,
                        n_sel, n_slc, n_kv):
    del pt_ref
    k_refs, v_refs, o_ref = rest[:n_sel], rest[n_sel:2 * n_sel], rest[2 * n_sel]
    b, g = pl.program_id(0), pl.program_id(1)
    r_, hd = q_ref.shape[1], q_ref.shape[2]
    qb = (_rms(q_ref[0], qn_ref[...]) * hd ** -0.5).astype(BF16)

    def own_group(ref):
        rows = ref[0, :, 0, 0, :]
        for gi in range(1, n_kv):
            rows = jnp.where(g == gi, ref[0, :, 0, gi, :], rows)
        return rows

    k_new = ksn_ref[0, 0]
    v_new = vsn_ref[0, 0]
    logits, values = [], []
    for k in range(n_sel):
        idx = sel_ref[(b * n_kv + g) * TOP_N + k]
        is_new = idx == n_slc - 1
        kt = jnp.where(is_new, jnp.broadcast_to(k_new, (SLC_BLK, hd)), own_group(k_refs[k]))
        vt = jnp.where(is_new, jnp.broadcast_to(v_new, (SLC_BLK, hd)), own_group(v_refs[k]))
        logits.append(_dot_nt(qb, _rms(kt, kn_ref[1:2]).astype(BF16)) + tb_ref[0, idx])
        values.append(vt.astype(BF16))
    m = logits[0].max(axis=-1, keepdims=True)
    for s in logits[1:]:
        m = jnp.maximum(m, s.max(axis=-1, keepdims=True))
    es = [jnp.exp(s - m) for s in logits]
    den = es[0].sum(axis=-1, keepdims=True)
    for e in es[1:]:
        den = den + e.sum(axis=-1, keepdims=True)
    o_slc = jnp.zeros((r_, hd), F32)
    for e, v in zip(es, values):
        o_slc = o_slc + _dot((e / den).astype(BF16), v)

    n_w = kw_ref.shape[1]
    lw = _dot_nt(qb, _rms(own_group(kw_ref), kn_ref[2:3]).astype(BF16)) + wb_ref[0, :, 0:n_w]
    kwn = _rms(kwn_ref[0, 0], kn_ref[2:3]).astype(BF16).astype(F32)
    l_new = jnp.sum(qb.astype(F32) * kwn, axis=-1, keepdims=True) + wb_ref[0, :, n_w:n_w + 1]
    mw = jnp.maximum(lw.max(axis=-1, keepdims=True), l_new)
    ew, e_new = jnp.exp(lw - mw), jnp.exp(l_new - mw)
    dw = ew.sum(axis=-1, keepdims=True) + e_new
    o_win = (_dot((ew / dw).astype(BF16), own_group(vw_ref).astype(BF16))
             + (e_new / dw).astype(BF16).astype(F32) * vwn_ref[0, 0].astype(BF16).astype(F32))

    bg = _sigmoid(bg_ref[0] + gb_ref[...])
    o = bg[:, 0:1] * oc_ref[0] + bg[:, 1:2] * o_slc + bg[:, 2:3] * o_win
    o_ref[0] = o * _silu(gate_ref[0])


def _attn_sample(sel, page_table, q3, gate3, bg3, gbias2, o_cmp, cache_slc, slc_new, win_state,
                 win_new, tb, wb, q_norm, k_norm, n_kv, n_slc):
    bs, n_heads, hd = q3.shape
    r_ = n_heads // n_kv
    n_pages = page_table.shape[1]
    n_pool = cache_slc.shape[0]
    n_sel = min(TOP_N, n_slc)
    halves = PAGE_SIZE // SLC_BLK
    n_w = win_state.shape[1]

    def blk_map(k, kv_idx):
        def index_map(b, g, sel_ref, pt_ref):
            idx = jnp.minimum(sel_ref[(b * n_kv + g) * TOP_N + k], n_slc - 2)
            return (pt_ref[b * n_pages + idx // halves], idx % halves, kv_idx, 0, 0)
        return index_map

    head_spec = lambda: pl.BlockSpec((1, r_, hd), lambda b, g, s, p: (b, g, 0))
    in_specs = [
        head_spec(), head_spec(),
        pl.BlockSpec((1, r_, 3), lambda b, g, s, p: (b, g, 0)),
        pl.BlockSpec((r_, 3), lambda b, g, s, p: (g, 0)),
        head_spec(),
        pl.BlockSpec((1, 1, 1, hd), lambda b, g, s, p: (b, g, 0, 0)),
        pl.BlockSpec((1, 1, 1, hd), lambda b, g, s, p: (b, n_kv + g, 0, 0)),
        pl.BlockSpec((1, n_w, 1, n_kv, hd), lambda b, g, s, p: (b, 0, 0, 0, 0)),
        pl.BlockSpec((1, n_w, 1, n_kv, hd), lambda b, g, s, p: (b, 0, 1, 0, 0)),
        pl.BlockSpec((1, 1, 1, hd), lambda b, g, s, p: (b, g, 0, 0)),
        pl.BlockSpec((1, 1, 1, hd), lambda b, g, s, p: (b, n_kv + g, 0, 0)),
        pl.BlockSpec((1, n_slc, r_, SLC_BLK), lambda b, g, s, p: (g, 0, 0, 0)),
        pl.BlockSpec((1, r_, wb.shape[2]), lambda b, g, s, p: (g, 0, 0)),
        pl.BlockSpec((1, hd), lambda b, g, s, p: (0, 0)),
        pl.BlockSpec((3, hd), lambda b, g, s, p: (0, 0)),
    ]
    in_specs += [pl.BlockSpec((1, SLC_BLK, 1, n_kv, hd), blk_map(k, 0)) for k in range(n_sel)]
    in_specs += [pl.BlockSpec((1, SLC_BLK, 1, n_kv, hd), blk_map(k, 1)) for k in range(n_sel)]
    return pl.pallas_call(
        functools.partial(_attn_sample_kernel, n_sel=n_sel, n_slc=n_slc, n_kv=n_kv),
        grid_spec=pltpu.PrefetchScalarGridSpec(
            num_scalar_prefetch=2,
            grid=(bs, n_kv),
            in_specs=in_specs,
            out_specs=pl.BlockSpec((1, r_, hd), lambda b, g, s, p: (b, g, 0)),
        ),
        out_shape=jax.ShapeDtypeStruct((bs, n_heads, hd), F32),
        compiler_params=_cparams("arbitrary", "arbitrary"),
        name="attn_sample",
    )(sel, page_table.reshape(-1), q3, gate3, bg3, gbias2, o_cmp, slc_new, slc_new, win_state,
      win_state, win_new, win_new, tb, wb, q_norm.reshape(1, hd), k_norm,
      *([cache_slc] * (2 * n_sel)))


def _bucket_np(d):
    d = np.maximum(d, 0)
    n_exact = N_BUCKETS // 2
    df = np.maximum(d, 1).astype(np.float64)
    large = n_exact + (np.log(df / n_exact) / math.log(MAX_DIST / n_exact)
                       * (N_BUCKETS - n_exact)).astype(np.int64)
    return np.where(d < n_exact, d, np.minimum(large, N_BUCKETS - 1))


def _dist_bias(rel_table, d, valid, shift):
    fd = rel_table.astype(F32)[_bucket_np(np.arange(MAX_DIST + 1))]
    if shift:
        fd = fd - fd[MAX_DIST:MAX_DIST + 1]
    vals = jnp.moveaxis(fd[np.clip(d, 0, MAX_DIST)], -1, 0)
    return jnp.where(jnp.asarray(valid)[None], vals, NEG)


def _overlap_np(n_cmp, n_slc, rows, cols):
    cs = np.arange(n_cmp)[:, None] * CMP_STRIDE
    ss = np.arange(n_slc)[None, :] * SLC_BLK
    ov = np.minimum(cs + CMP_BLK, ss + SLC_BLK) - np.maximum(cs, ss)
    out = np.zeros((rows, cols), np.float32)
    out[:n_cmp, :n_slc] = np.maximum(ov, 0).astype(np.float32) / CMP_BLK
    return out


def _round_up(x, m):
    return (x + m - 1) // m * m


def kernel(x_prompt, x_sample, cache_cmp_kv, cache_slc_kv, state_win_kv, state_lru_h, state_conv,
           page_table, a_norm, a_w_in, a_conv_w, a_conv_b, a_w_rg, a_b_rg, a_w_ig, a_b_ig, a_lambda,
           a_w_out, kv_norm, w_kv, k_norm, cmp_pos, w_cmp1, w_cmp2, rel_table, b_norm, b_w_in,
           b_gate_bias, b_q_norm, b_w_out):
    batch, seq, d_model = x_prompt.shape
    bs = x_sample.shape[0]
    n_a, n_b = a_norm.shape[0], b_norm.shape[0]
    d_rnn = a_w_in.shape[2] // 2
    n_kv, hd = cache_cmp_kv.shape[3], cache_cmp_kv.shape[4]
    n_heads = rel_table.shape[1]
    r_ = n_heads // n_kv
    hq = n_heads * hd
    n_sg = 2 * n_kv
    cols = n_sg * hd
    n_pages = page_table.shape[1]
    past = n_pages * PAGE_SIZE
    m = batch * seq
    assert x_sample.shape[1] == 1 and seq % QB == 0 and seq >= WINDOW and hd == LANES
    assert seq // CMP_STRIDE == LANES and past >= WINDOW and state_win_kv.shape[1] == WINDOW

    xp = x_prompt.reshape(m, d_model)
    xs = jnp.pad(x_sample.reshape(bs, d_model), ((0, SAMPLE_ROWS - bs), (0, 0)))
    pad_rows = lambda a: jnp.pad(a, ((0, SAMPLE_ROWS - bs), (0, 0)))

    p_h, p_c, s_h, s_c = [], [], [], []
    h0_p = jnp.zeros((batch, d_rnn), F32)
    c0_p = jnp.zeros((batch, CONV_W - 1, d_rnn), F32)
    for l in range(n_a):
        wrg, wig = a_w_rg[l].astype(BF16), a_w_ig[l].astype(BF16)
        lru_w = (a_conv_w[l], a_conv_b[l], wrg, a_b_rg[l], wig, a_b_ig[l], a_lambda[l])
        u, us = _matmul(_norm_cast(xp, a_norm[l]), _norm_cast(xs, a_norm[l]), a_w_in, layer=l)
        gp, hl, ct = _lru_prompt(u, batch, seq, *lru_w, h0_p, c0_p)
        c0 = jnp.pad(jnp.swapaxes(state_conv[l], 0, 1), ((0, 0), (0, SAMPLE_ROWS - bs), (0, 0)))
        gs, hs = _lru_sample(us, *lru_w, pad_rows(state_lru_h[l]), c0)
        xp, xs = _matmul(gp, gs, a_w_out, layer=l, res=xp, res_s=xs)
        p_h.append(hl.reshape(batch, d_rnn))
        p_c.append(ct)
        s_h.append(hs[:bs])
        s_c.append(jnp.concatenate([state_conv[l][:, 1:], us[:bs, None, :d_rnn]], axis=1))

    kv, kvs, p_cmp_kv, p_slc_kv, p_win_rows = _matmul(
        _norm_cast(xp, kv_norm), _norm_cast(xs, kv_norm), w_kv[None], tm=512, tn=n_kv * hd,
        rows5=(batch, seq, n_kv, hd))
    kvs = kvs[:bs]
    rows5 = lambda a, n: a.reshape(n, -1, 2, n_kv, hd)
    p_win_kv = p_win_rows[:, seq - WINDOW:]
    s_cmp_kv, s_slc_kv = rows5(kvs[:, :cols], bs), rows5(kvs[:, cols:2 * cols], bs)
    s_win_rows = rows5(kvs[:, 2 * cols:], bs)
    s_win_kv = jnp.concatenate([state_win_kv, s_win_rows], axis=1)[:, 1:]

    half = CMP_BLK // 2
    hid = w_cmp1.shape[3]
    w1ab = jnp.concatenate([w_cmp1[:, :half].reshape(2, half * hd, hid),
                            w_cmp1[:, half:].reshape(2, half * hd, hid)], axis=2).astype(BF16)
    posab = jnp.stack([cmp_pos[:, :half].reshape(2, half * hd),
                       cmp_pos[:, half:].reshape(2, half * hd)], axis=1)
    w2b = w_cmp2.astype(BF16)
    own_pages = jnp.arange(m // PAGE_SIZE, dtype=jnp.int32).reshape(batch, seq // PAGE_SIZE)
    r16_p = _page_gather(p_cmp_kv.reshape(m // PAGE_SIZE, PAGE_SIZE, 2, n_kv, hd), own_pages)
    comp_p = _compress(r16_p, posab, w1ab, w2b, k_norm[0], n_kv)
    comp_s = _compress(_page_gather(cache_cmp_kv, page_table), posab, w1ab, w2b, k_norm[0], n_kv)

    n_c = seq // CMP_STRIDE
    n_cmp_p = (seq - CMP_BLK) // CMP_STRIDE + 1
    n_slc_p = seq // SLC_BLK
    wk = WINDOW + QB
    dv = WINDOW - np.arange(wk + 1)
    vrow = _dist_bias(rel_table, dv, dv >= 0, True)
    wbias = jnp.tile(vrow, (1, QB))[:, :QB * wk].reshape(n_heads, QB, wk)
    dc = (np.arange(QB)[:, None] - (np.arange(n_c)[None, :] - (n_c - QB // CMP_STRIDE)) * CMP_STRIDE
          - (CMP_BLK - 1))
    tcfix = jnp.where(jnp.asarray(dc >= 0)[None], _dist_bias(rel_table, dc, dc >= 0, True), 0.0)
    mimp_t = jnp.asarray(_overlap_np(n_cmp_p, n_slc_p, n_c, n_slc_p).T, BF16)
    expand = jnp.asarray((np.arange(n_slc_p)[:, None] == np.arange(seq)[None, :] // SLC_BLK), BF16)

    total = past + 1
    n_cmp_s = (total - CMP_BLK) // CMP_STRIDE + 1
    n_slc_s = -(-total // SLC_BLK)
    nr_s = past // CMP_STRIDE
    assert n_cmp_s == nr_s - 1 and n_slc_s == past // SLC_BLK + 1
    dcs = past - (np.arange(nr_s) * CMP_STRIDE + CMP_BLK - 1)
    cbias_s = _dist_bias(rel_table, dcs, (dcs >= 0) & (np.arange(nr_s) < n_cmp_s), False)
    cbias_s = cbias_s.reshape(n_kv, r_, nr_s)
    mimp_s = jnp.asarray(_overlap_np(n_cmp_s, n_slc_s, nr_s, _round_up(n_slc_s, LANES)), BF16)
    dss = past - np.arange(n_slc_s * SLC_BLK)
    tb_s = _dist_bias(rel_table, dss, dss >= 0, False)
    tb_s = tb_s.reshape(n_kv, r_, n_slc_s, SLC_BLK).transpose(0, 2, 1, 3)
    dws = WINDOW - np.arange(WINDOW + LANES)
    wb_s = _dist_bias(rel_table, dws, dws >= 0, False).reshape(n_kv, r_, WINDOW + LANES)

    slc_new = kvs[:, cols:2 * cols].reshape(bs, n_sg, 1, hd)
    win_new = kvs[:, 2 * cols:].reshape(bs, n_sg, 1, hd)
    slab_pad = ((0, 0), (0, 0), (0, LANES - 3 * r_))
    for l in range(n_b):
        w_qg = b_w_in[l][:, :2 * hq].astype(BF16)
        w_bg = jnp.pad(b_w_in[l][:, 2 * hq:], ((0, 0), (0, LANES - 3 * n_heads)))
        w_bg = jnp.pad(w_bg[:, :3 * n_heads].reshape(d_model, n_kv, 3 * r_), slab_pad)
        w_bg = w_bg.reshape(d_model, n_kv * LANES)
        gbias = b_gate_bias[l]
        gb_slab = jnp.pad(gbias.reshape(n_kv, 1, 3 * r_), slab_pad)

        xn, xns = _norm_cast(xp, b_norm[l]), _norm_cast(xs, b_norm[l])
        u, us = _matmul(xn, xns, w_qg[None], q_gain=b_q_norm[l].reshape(1, hd) * hd ** -0.5)
        bgs, bgs_s = _matmul(xn, xns, w_bg[None])
        og = _attn_prompt(u, bgs, gb_slab, kv, comp_p, wbias, tcfix, mimp_t, expand,
                          k_norm, batch, seq, n_kv, n_heads)

        us = us[:bs]
        q3 = us[:, :hq].reshape(bs, n_heads, hd)
        gate3 = us[:, hq:].reshape(bs, n_heads, hd)
        bg3 = bgs_s[:bs].reshape(bs, n_kv, LANES)[:, :, :3 * r_].reshape(bs, n_heads, 3)
        o_cmp, sel = _attn_sample_cmp(q3, comp_s, cbias_s, mimp_s, b_q_norm[l], n_kv, n_slc_s)
        os_ = _attn_sample(sel[:, :, 0, :TOP_N].reshape(-1), page_table, q3, gate3, bg3,
                           gbias.reshape(n_heads, 3), o_cmp, cache_slc_kv, slc_new, state_win_kv,
                           win_new, tb_s, wb_s, b_q_norm[l], k_norm, n_kv, n_slc_s)
        xp, xs = _matmul(og, pad_rows(os_.reshape(bs, hq).astype(BF16)), b_w_out, layer=l,
                         res=xp, res_s=xs)

    return (xp.reshape(batch, seq, d_model), xs[:bs].reshape(bs, 1, d_model),
            p_cmp_kv, p_slc_kv, p_win_kv, jnp.stack(p_h), jnp.stack(p_c),
            s_cmp_kv, s_slc_kv, s_win_kv, jnp.stack(s_h), jnp.stack(s_c))
```

```python
import functools
import math

import numpy as np
import jax
import jax.numpy as jnp
from jax import lax
from jax.experimental import pallas as pl
from jax.experimental.pallas import tpu as pltpu

F32 = jnp.float32
BF16 = jnp.bfloat16

EPS = 1e-6
NEG = -1e30
BIG = 1e30
LOWEST = -3e38
LRU_C = 8.0
CONV_W = 4
CMP_BLK = 32
CMP_STRIDE = 16
SLC_BLK = 64
TOP_N = 16
WINDOW = 512
N_BUCKETS = 32
MAX_DIST = 128
PAGE_SIZE = 128
QB = 128
GATHER_PAGES = 4
LANES = 128
SUBLANES = 8
SEG_PAD = 8
SAMPLE_ROWS = 16
VMEM_LIMIT = 48 * 1024 * 1024


def _cparams(*sem, vmem_limit=VMEM_LIMIT):
    return pltpu.CompilerParams(dimension_semantics=sem, vmem_limit_bytes=vmem_limit)


def _dot(a, b):
    return jnp.dot(a, b, preferred_element_type=F32)


def _dot_nt(a, b):
    return lax.dot_general(a, b, (((1,), (1,)), ((), ())), preferred_element_type=F32)


def _rms(x, g):
    return x * lax.rsqrt(jnp.mean(x * x, axis=-1, keepdims=True) + EPS) * g


def _sigmoid(x):
    return jax.nn.sigmoid(x)


def _silu(x):
    return x * jax.nn.sigmoid(x)


def _norm_kernel(x_ref, g_ref, o_ref):
    o_ref[...] = _rms(x_ref[...], g_ref[...]).astype(o_ref.dtype)


def _norm_cast(x, g):
    m, d = x.shape
    tm = min(m, 512)
    return pl.pallas_call(
        _norm_kernel,
        grid=(m // tm,),
        in_specs=[pl.BlockSpec((tm, d), lambda i: (i, 0)),
                  pl.BlockSpec((1, d), lambda i: (0, 0))],
        out_specs=pl.BlockSpec((tm, d), lambda i: (i, 0)),
        out_shape=jax.ShapeDtypeStruct((m, d), BF16),
        compiler_params=_cparams("parallel"),
        name="norm_cast",
    )(x, g.reshape(1, d))


def _mm_kernel(a_ref, as_ref, w_ref, *rest, has_res, n_rows5, q_tiles):
    if q_tiles:
        qg_ref, rest = rest[0], rest[1:]
    if has_res:
        r_ref, rs_ref, o_ref, os_ref = rest[:4]
    else:
        o_ref, os_ref = rest[:2]
    wb_scr = rest[-1]
    rows5_refs = rest[-1 - n_rows5:-1]
    wb_ref = wb_scr if w_ref.dtype != BF16 else w_ref

    @pl.when(pl.program_id(1) == 0)
    def _():
        if w_ref.dtype != BF16:
            wb_scr[...] = w_ref[...].astype(BF16)
        acc_s = _dot(as_ref[...], wb_ref[...])
        os_ref[...] = rs_ref[...] + acc_s if has_res else acc_s

    acc = _dot(a_ref[...], wb_ref[...])
    if q_tiles:
        hd = qg_ref.shape[1]

        @pl.when(pl.program_id(0) < q_tiles)
        def _():
            for h in range(acc.shape[1] // hd):
                x = acc[:, h * hd:(h + 1) * hd]
                o_ref[:, h * hd:(h + 1) * hd] = (
                    x * lax.rsqrt(jnp.mean(x * x, axis=-1, keepdims=True) + EPS) * qg_ref[...])

        @pl.when(pl.program_id(0) >= q_tiles)
        def _():
            o_ref[...] = _silu(acc)
    else:
        o_ref[...] = r_ref[...] + acc if has_res else acc

    for br, r5_ref in enumerate(rows5_refs):
        @pl.when(pl.program_id(0) // 2 == br)
        def _(r5_ref=r5_ref):
            n_grp, hd = r5_ref.shape[3], r5_ref.shape[4]
            for g in range(n_grp):
                r5_ref[0, :, 0, g, :] = acc[:, g * hd:(g + 1) * hd]


def _mm_vmem_bytes(tm, tn, k, ms, w_dtype, has_res):
    w_bytes = jnp.dtype(w_dtype).itemsize
    blocks = (tm + ms) * k * 2 + k * tn * w_bytes + (tm + ms) * tn * 4 * (2 if has_res else 1)
    cast = k * tn * 2 if w_bytes != 2 else 0
    return 2 * blocks + cast + 3 * tm * tn * 4


def _matmul(a, a_s, w, layer=0, n=None, res=None, res_s=None, tm=1024, tn=512, rows5=None,
            q_gain=None):
    m, k = a.shape
    ms = a_s.shape[0]
    n = w.shape[2] if n is None else n
    tm = min(m, tm)
    tn = min(n, tn)
    has_res = res is not None
    in_specs = [pl.BlockSpec((tm, k), lambda j, i: (i, 0)),
                pl.BlockSpec((ms, k), lambda j, i: (0, 0)),
                pl.BlockSpec((None, k, tn), lambda j, i: (layer, 0, j))]
    args = [a, a_s, w]
    q_tiles = 0
    if q_gain is not None:
        assert not has_res and rows5 is None and (n // 2) % tn == 0 and tn % q_gain.shape[1] == 0
        q_tiles = n // 2 // tn
        in_specs.append(pl.BlockSpec(q_gain.shape, lambda j, i: (0, 0)))
        args.append(q_gain)
    if has_res:
        in_specs += [pl.BlockSpec((tm, tn), lambda j, i: (i, j)),
                     pl.BlockSpec((ms, tn), lambda j, i: (0, j))]
        args += [res, res_s]
    out_specs = [pl.BlockSpec((tm, tn), lambda j, i: (i, j)),
                 pl.BlockSpec((ms, tn), lambda j, i: (0, j))]
    out_shape = [jax.ShapeDtypeStruct((m, n), F32), jax.ShapeDtypeStruct((ms, n), F32)]
    n_rows5 = 0
    if rows5 is not None:
        batch, seq, n_kv, hd = rows5
        n_rows5 = n // (2 * tn)
        nt = seq // tm
        assert tn == n_kv * hd and seq % tm == 0 and n % (2 * tn) == 0

        def rows5_map(br):
            def index_map(j, i):
                active, before = j // 2 == br, j < 2 * br
                park = lambda first, last: jnp.where(before, first, last)
                return (jnp.where(active, i // nt, park(0, batch - 1)),
                        jnp.where(active, i % nt, park(0, nt - 1)),
                        jnp.where(active, j % 2, park(0, 1)), 0, 0)
            return index_map

        for br in range(n_rows5):
            out_specs.append(pl.BlockSpec((1, tm, 1, n_kv, hd), rows5_map(br)))
            out_shape.append(jax.ShapeDtypeStruct((batch, seq, 2, n_kv, hd), F32))
    return pl.pallas_call(
        functools.partial(_mm_kernel, has_res=has_res, n_rows5=n_rows5, q_tiles=q_tiles),
        grid=(n // tn, m // tm),
        in_specs=in_specs,
        out_specs=out_specs,
        out_shape=out_shape,
        scratch_shapes=[pltpu.VMEM((k, tn) if w.dtype != BF16 else (SUBLANES, LANES), BF16)],
        compiler_params=_cparams("arbitrary", "arbitrary",
                                 vmem_limit=max(VMEM_LIMIT, _mm_vmem_bytes(tm, tn, k, ms, w.dtype,
                                                                           has_res))),
        name="matmul",
    )(*args)


def _lru_coeffs(xc, wrg_ref, wig_ref, brg, big, lam, pos0_row):
    rows, cb = xc.shape
    bs = wrg_ref.shape[-1]
    xcb = xc.astype(BF16)
    r_parts, i_parts = [], []
    for n in range(cb // bs):
        xs = xcb[:, n * bs:(n + 1) * bs]
        r_parts.append(_dot(xs, wrg_ref[n]))
        i_parts.append(_dot(xs, wig_ref[n]))
    r = _sigmoid(jnp.concatenate(r_parts, axis=1) + brg)
    i = _sigmoid(jnp.concatenate(i_parts, axis=1) + big)
    nl = -lam
    softplus = jnp.maximum(nl, 0.0) + jnp.log1p(jnp.exp(-jnp.abs(nl)))
    log_a = -LRU_C * r * softplus
    a = jnp.exp(log_a)
    mult = jnp.sqrt(-jnp.tanh(log_a) * (a * a + 1.0))
    if pos0_row is not None:
        row = lax.broadcasted_iota(jnp.int32, (rows, cb), 0)
        mult = jnp.where(row == pos0_row, 1.0, mult)
    return a, mult * i * xc


def _lru_prompt_kernel(xb_ref, gate_ref, cw_ref, cb_ref, wrg_ref, brg_ref, wig_ref, big_ref,
                       lam_ref, h0_ref, c0_ref, g_ref, hl_ref, ct_ref, st_scr, un_scr, tail_scr, h_scr):
    tc = pl.program_id(2)
    tt, cb = xb_ref.shape
    nlb = cb // LANES
    seg = tt // SUBLANES
    tail = (CONV_W - 1) * SUBLANES

    @pl.when(tc == 0)
    def _():
        h_scr[...] = jnp.broadcast_to(h0_ref[0], (SUBLANES, cb))
        for i in range(CONV_W - 1):
            tail_scr[i * SUBLANES:(i + 1) * SUBLANES, :] = jnp.broadcast_to(c0_ref[0, i:i + 1, :],
                                                                             (SUBLANES, cb))

    pitch = seg + SEG_PAD

    def to_segments(src_ref, slot):
        cols = []
        for lb in range(nlb):
            for j in range(SUBLANES):
                st_scr[slot, lb, j * pitch:j * pitch + seg, :] = src_ref[j * seg:(j + 1) * seg,
                                                                         lb * LANES:(lb + 1) * LANES]
            cols.append(jnp.concatenate(
                [st_scr[slot, lb, pl.ds(k, SUBLANES, stride=pitch), :] for k in range(seg)], axis=0))
        return jnp.concatenate(cols, axis=1)

    x = to_segments(xb_ref, 0)

    sub = lax.broadcasted_iota(jnp.int32, (SUBLANES, cb), 0)
    heads = []
    for i in range(CONV_W - 1):
        cur = x[tt - tail + i * SUBLANES:tt - tail + (i + 1) * SUBLANES]
        prev = tail_scr[i * SUBLANES:(i + 1) * SUBLANES, :]
        heads.append(pltpu.roll(jnp.where(sub == SUBLANES - 1, prev, cur), 1, axis=0))
    tail_scr[...] = x[tt - tail:tt]
    w = cw_ref[...]
    xc = cb_ref[...]
    for k in range(CONV_W - 1):
        m = CONV_W - 1 - k
        xc = xc + w[k:k + 1] * jnp.concatenate(heads[CONV_W - 1 - m:] + [x[0:tt - m * SUBLANES]], axis=0)
    xc = xc + w[CONV_W - 1:CONV_W] * x

    a, b_in = _lru_coeffs(xc, wrg_ref, wig_ref, brg_ref[...], big_ref[...], lam_ref[...],
                           jnp.where(tc == 0, 0, -1))

    blk = lambda v, k: v[k * SUBLANES:(k + 1) * SUBLANES]
    e, p = blk(b_in, 0), blk(a, 0)
    for k in range(1, seg):
        e = blk(a, k) * e + blk(b_in, k)
        p = blk(a, k) * p
    carry = h_scr[0:1, :]
    carries = [carry]
    for j in range(SUBLANES - 1):
        carry = p[j:j + 1] * carry + e[j:j + 1]
        carries.append(carry)
    h = jnp.concatenate(carries, axis=0)
    hs = []
    for k in range(seg):
        h = blk(a, k) * h + blk(b_in, k)
        hs.append(h)
    h_scr[...] = jnp.broadcast_to(h[SUBLANES - 1:SUBLANES], (SUBLANES, cb))
    g = jnp.concatenate(hs, axis=0) * _silu(to_segments(gate_ref, 1))

    for lb in range(nlb):
        for k in range(seg):
            un_scr[lb, pl.ds(k, SUBLANES, stride=pitch), :] = g[k * SUBLANES:(k + 1) * SUBLANES,
                                                                lb * LANES:(lb + 1) * LANES]
        for j in range(SUBLANES):
            g_ref[j * seg:(j + 1) * seg, lb * LANES:(lb + 1) * LANES] = (
                un_scr[lb, j * pitch:j * pitch + seg, :].astype(g_ref.dtype))

    @pl.when(tc == pl.num_programs(2) - 1)
    def _():
        hl_ref[0] = h[SUBLANES - 1:SUBLANES]
        ct_ref[0] = xb_ref[tt - (CONV_W - 1):tt, :]


def _lru_prompt(u, batch, seq, cw, cb_, wrg, brg, wig, big, lam, h0, c0):
    m, c2 = u.shape
    c = c2 // 2
    cb = min(c, 1024)
    tt = min(seq, 256)
    nt = seq // tt
    ncb = c // cb
    bs = wrg.shape[-1]
    nb = cb // bs
    vec = lambda: pl.BlockSpec((1, cb), lambda b, j, t: (0, j))
    return pl.pallas_call(
        _lru_prompt_kernel,
        grid=(batch, ncb, nt),
        in_specs=[
            pl.BlockSpec((tt, cb), lambda b, j, t: (b * nt + t, j)),
            pl.BlockSpec((tt, cb), lambda b, j, t: (b * nt + t, ncb + j)),
            pl.BlockSpec((CONV_W, cb), lambda b, j, t: (0, j)),
            vec(),
            pl.BlockSpec((nb, bs, bs), lambda b, j, t: (j, 0, 0)),
            vec(),
            pl.BlockSpec((nb, bs, bs), lambda b, j, t: (j, 0, 0)),
            vec(),
            vec(),
            pl.BlockSpec((1, 1, cb), lambda b, j, t: (b, 0, j)),
            pl.BlockSpec((1, CONV_W - 1, cb), lambda b, j, t: (b, 0, j)),
        ],
        out_specs=[
            pl.BlockSpec((tt, cb), lambda b, j, t: (b * nt + t, j)),
            pl.BlockSpec((1, 1, cb), lambda b, j, t: (b, 0, j)),
            pl.BlockSpec((1, CONV_W - 1, cb), lambda b, j, t: (b, 0, j)),
        ],
        out_shape=[
            jax.ShapeDtypeStruct((m, c), BF16),
            jax.ShapeDtypeStruct((batch, 1, c), F32),
            jax.ShapeDtypeStruct((batch, CONV_W - 1, c), F32),
        ],
        scratch_shapes=[pltpu.VMEM((2, cb // LANES, tt + SUBLANES * SEG_PAD, LANES), F32),
                        pltpu.VMEM((cb // LANES, tt + SUBLANES * SEG_PAD, LANES), F32),
                        pltpu.VMEM(((CONV_W - 1) * SUBLANES, cb), F32),
                        pltpu.VMEM((SUBLANES, cb), F32)],
        compiler_params=_cparams("parallel", "parallel", "arbitrary"),
        name="lru_prompt",
    )(u, u, cw, cb_.reshape(1, c), wrg, brg.reshape(1, c), wig, big.reshape(1, c),
      lam.reshape(1, c), h0.reshape(batch, 1, c), c0)


def _lru_sample_kernel(xb_ref, gate_ref, cw_ref, cb_ref, wrg_ref, brg_ref, wig_ref, big_ref,
                       lam_ref, h0_ref, c0_ref, g_ref, h_ref):
    w = cw_ref[...]
    xc = cb_ref[...]
    for k in range(CONV_W - 1):
        xc = xc + w[k:k + 1] * c0_ref[k]
    xc = xc + w[CONV_W - 1:CONV_W] * xb_ref[...]
    a, b_in = _lru_coeffs(xc, wrg_ref, wig_ref, brg_ref[...], big_ref[...], lam_ref[...], None)
    h = a * h0_ref[...] + b_in
    h_ref[...] = h
    g_ref[...] = (h * _silu(gate_ref[...])).astype(g_ref.dtype)


def _lru_sample(u, cw, cb_, wrg, brg, wig, big, lam, h0, c0):
    rows, c2 = u.shape
    c = c2 // 2
    cb = min(c, 1024)
    ncb = c // cb
    bs = wrg.shape[-1]
    nb = cb // bs
    vec = lambda: pl.BlockSpec((1, cb), lambda j: (0, j))
    return pl.pallas_call(
        _lru_sample_kernel,
        grid=(ncb,),
        in_specs=[
            pl.BlockSpec((rows, cb), lambda j: (0, j)),
            pl.BlockSpec((rows, cb), lambda j: (0, ncb + j)),
            pl.BlockSpec((CONV_W, cb), lambda j: (0, j)),
            vec(),
            pl.BlockSpec((nb, bs, bs), lambda j: (j, 0, 0)),
            vec(),
            pl.BlockSpec((nb, bs, bs), lambda j: (j, 0, 0)),
            vec(),
            vec(),
            pl.BlockSpec((rows, cb), lambda j: (0, j)),
            pl.BlockSpec((CONV_W - 1, rows, cb), lambda j: (0, 0, j)),
        ],
        out_specs=[pl.BlockSpec((rows, cb), lambda j: (0, j)),
                   pl.BlockSpec((rows, cb), lambda j: (0, j))],
        out_shape=[jax.ShapeDtypeStruct((rows, c), BF16),
                   jax.ShapeDtypeStruct((rows, c), F32)],
        compiler_params=_cparams("parallel"),
        name="lru_sample",
    )(u, u, cw, cb_.reshape(1, c), wrg, brg.reshape(1, c), wig, big.reshape(1, c),
      lam.reshape(1, c), h0, c0)


def _page_gather_kernel(pt_ref, *refs):
    del pt_ref
    c_refs, o_ref, slab_scr = refs[:GATHER_PAGES], refs[GATHER_PAGES], refs[GATHER_PAGES + 1]
    n_kv, hd = c_refs[0].shape[3], c_refs[0].shape[4]
    rows = PAGE_SIZE // CMP_STRIDE
    for q, c_ref in enumerate(c_refs):
        for sg in range(2 * n_kv):
            slab_scr[q, sg] = c_ref[0, :, sg // n_kv, sg % n_kv, :]
            for l in range(CMP_STRIDE):
                o_ref[0, sg, q * rows:(q + 1) * rows, l * hd:(l + 1) * hd] = (
                    slab_scr[q, sg, pl.ds(l, rows, stride=CMP_STRIDE), :])


def _page_gather(pages, page_table):
    bs, n_pages = page_table.shape
    n_kv, hd = pages.shape[3], pages.shape[4]
    n_sg = 2 * n_kv
    rows = PAGE_SIZE // CMP_STRIDE
    assert n_pages % GATHER_PAGES == 0

    def page_map(q):
        return lambda b, j, pt: (pt[b * n_pages + j * GATHER_PAGES + q], 0, 0, 0, 0)

    return pl.pallas_call(
        _page_gather_kernel,
        grid_spec=pltpu.PrefetchScalarGridSpec(
            num_scalar_prefetch=1,
            grid=(bs, n_pages // GATHER_PAGES),
            in_specs=[pl.BlockSpec((1, PAGE_SIZE, 2, n_kv, hd), page_map(q))
                      for q in range(GATHER_PAGES)],
            out_specs=pl.BlockSpec((1, n_sg, GATHER_PAGES * rows, CMP_STRIDE * hd),
                                   lambda b, j, pt: (b, 0, j, 0)),
            scratch_shapes=[pltpu.VMEM((GATHER_PAGES, n_sg, PAGE_SIZE, hd), F32)],
        ),
        out_shape=jax.ShapeDtypeStruct((bs, n_sg, n_pages * rows, CMP_STRIDE * hd), F32),
        compiler_params=_cparams("parallel", "arbitrary"),
        name="page_gather",
    )(page_table.reshape(-1), *([pages] * GATHER_PAGES))


def _compress_kernel(r_ref, pos_ref, w1_ref, w2_ref, kn_ref, o_ref, q_scr, *, n_kv):
    sg = pl.program_id(1)
    rows = r_ref[0, 0]
    nr = rows.shape[0]
    hid = w1_ref.shape[2] // 2
    xa = (rows + pos_ref[0, 0:1]).astype(BF16)
    xb = (rows + pos_ref[0, 1:2]).astype(BF16)
    w1 = w1_ref[0]
    p = _dot(xa, w1[:, :hid])
    q_scr[0:nr, :] = _dot(xb, w1[:, hid:])
    q_scr[nr:nr + 8, :] = jnp.zeros((8, hid), F32)
    pre = p + q_scr[1:nr + 1, :]
    comp = _dot(_silu(pre).astype(BF16), w2_ref[0])
    normed = _rms(comp, kn_ref[...])
    o_ref[0, 0] = jnp.where(sg < n_kv, normed, comp).astype(o_ref.dtype)


def _compress(r16, posab, w1ab, w2, k_norm0, n_kv):
    b, n_sg, nr, kk = r16.shape
    hid2 = w1ab.shape[2]
    hd = w2.shape[2]
    return pl.pallas_call(
        functools.partial(_compress_kernel, n_kv=n_kv),
        grid=(b, n_sg),
        in_specs=[
            pl.BlockSpec((1, 1, nr, kk), lambda i, s: (i, s, 0, 0)),
            pl.BlockSpec((1, 2, kk), lambda i, s: (s // n_kv, 0, 0)),
            pl.BlockSpec((1, kk, hid2), lambda i, s: (s // n_kv, 0, 0)),
            pl.BlockSpec((1, hid2 // 2, hd), lambda i, s: (s // n_kv, 0, 0)),
            pl.BlockSpec((1, hd), lambda i, s: (0, 0)),
        ],
        out_specs=pl.BlockSpec((1, 1, nr, hd), lambda i, s: (i, s, 0, 0)),
        out_shape=jax.ShapeDtypeStruct((b, n_sg, nr, hd), BF16),
        scratch_shapes=[pltpu.VMEM((nr + 8, hid2 // 2), F32)],
        compiler_params=_cparams("parallel", "arbitrary"),
        name="compress",
    )(r16, posab, w1ab, w2, k_norm0.reshape(1, hd))


def _attn_prompt_kernel(uq_ref, ug_ref, bg_ref, gb_ref, ks_ref, vs_ref, kw_ref, vw_ref,
                        kc_ref, vc_ref, wb_ref, tc_ref, mit_ref, ex_ref, kn_ref, og_ref,
                        ks_scr, vs_scr, kw_scr, vw_scr, madd_scr, lg_scr, p_scr, m_scr,
                        *, n_heads_grp, n_slc):
    r_ = n_heads_grp
    c = pl.program_id(2)
    t0 = c * QB
    seq, hd = ks_ref.shape
    rq = r_ * QB
    wk = WINDOW + QB
    n_wt = wk // QB

    @pl.when(c == 0)
    def _():
        ks_scr[...] = _rms(ks_ref[...], kn_ref[1:2]).astype(BF16)
        vs_scr[:, 0:hd] = vs_ref[...].astype(BF16)
        vs_scr[:, hd:2 * hd] = jnp.ones((seq, hd), BF16)
        kw_scr[0:WINDOW, :] = jnp.zeros((WINDOW, hd), BF16)
        vw_scr[0:WINDOW, :] = jnp.zeros((WINDOW, 2 * hd), BF16)
        kw_scr[WINDOW:WINDOW + seq, :] = _rms(kw_ref[...], kn_ref[2:3]).astype(BF16)
        vw_scr[WINDOW:WINDOW + seq, 0:hd] = vw_ref[...].astype(BF16)
        vw_scr[WINDOW:WINDOW + seq, hd:2 * hd] = jnp.ones((seq, hd), BF16)
        lg_scr[...] = jnp.zeros(lg_scr.shape, F32)

    uq = uq_ref[...]
    qb = jnp.concatenate([uq[:, r * hd:(r + 1) * hd] for r in range(r_)],
                         axis=0).astype(BF16)

    n_c = kc_ref.shape[2]
    bias_c = pltpu.roll(tc_ref[...], (c * (QB // CMP_STRIDE) + QB // CMP_STRIDE) % n_c, axis=2)
    lc = _dot_nt(qb, kc_ref[0, 0]).reshape(r_, QB, n_c) + bias_c
    tq = t0 + lax.broadcasted_iota(jnp.int32, (QB, n_c), 0)
    nn = lax.broadcasted_iota(jnp.int32, (QB, n_c), 1)
    c_ok = (tq - (nn * CMP_STRIDE + (CMP_BLK - 1)) >= 0)[None]
    lc = jnp.where(c_ok, lc, NEG)
    e = jnp.exp(lc - jnp.max(lc, axis=-1, keepdims=True))
    pc = jnp.where(c_ok, e / jnp.sum(e, axis=-1, keepdims=True), 0.0)
    pcb = pc.reshape(rq, n_c).astype(BF16)
    o_cmp = _dot(pcb, vc_ref[0, 0])
    imp_all = _dot_nt(mit_ref[...], pcb)
    imp = imp_all[:, 0:QB]
    for r in range(1, r_):
        imp = imp + imp_all[:, r * QB:(r + 1) * QB]

    jj = lax.broadcasted_iota(jnp.int32, (n_slc, QB), 0)
    cur = (t0 + lax.broadcasted_iota(jnp.int32, (n_slc, QB), 1)) // SLC_BLK
    forced = (jj == 0) | (jj == cur) | (jj == cur - 1)
    allowed = jj <= cur
    score = jnp.where(forced, BIG, jnp.where(allowed, imp, NEG))
    rank = jnp.zeros((n_slc, QB), F32)
    for k in range(n_slc):
        sk = score[k:k + 1, :]
        rank = rank + jnp.where(sk > score, 1.0, jnp.where((sk == score) & (k < jj), 1.0, 0.0))
    sel = jnp.where((rank < min(TOP_N, n_slc)) & allowed, 1.0, 0.0).astype(BF16)
    selx = lax.dot_general(sel, ex_ref[...], (((0,), (0,)), ((), ())),
                           preferred_element_type=F32)
    tk = 2 * QB
    for kp in range(seq // tk):
        madd_scr[kp] = (selx[:, kp * tk:(kp + 1) * tk] - 1.0) * BIG

    m_scr[...] = jnp.full((rq, tk), NEG, F32)
    near1 = wb_ref[:, :, WINDOW - QB:WINDOW]
    near0 = wb_ref[:, :, WINDOW:WINDOW + QB]

    def logits_pair(kp, near):
        k0 = pl.multiple_of(kp * tk, tk)
        s3 = _dot_nt(qb, ks_scr[pl.ds(k0, tk), :]).reshape(r_, QB, tk) + madd_scr[kp][None]
        if near:
            halves = []
            for h in range(2):
                dist = c - (2 * kp + h)
                bias = jnp.where(dist == 0, near0, jnp.where(dist == 1, near1, 0.0))
                halves.append(s3[:, :, h * QB:(h + 1) * QB] + bias)
            s3 = jnp.concatenate(halves, axis=2)
        s = s3.reshape(rq, tk)
        lg_scr[kp] = s
        m_scr[...] = jnp.maximum(m_scr[...], s)

    def far_body(kp, carry):
        logits_pair(kp, False)
        return carry

    last = c // 2
    prev = jnp.maximum(last - 1, 0)
    logits_pair(last, True)
    logits_pair(prev, True)
    lax.fori_loop(0, prev, far_body, 0)

    m_scr[...] = jnp.broadcast_to(jnp.max(m_scr[...], axis=-1, keepdims=True), (rq, tk))
    for kp in range(seq // tk):
        p = jnp.where(kp <= last, jnp.exp(lg_scr[kp] - m_scr[...]), 0.0)
        p_scr[:, kp * tk:(kp + 1) * tk] = p.astype(BF16)
    acc = _dot(p_scr[...], vs_scr[...])
    o_slc = acc[:, 0:hd] / acc[:, hd:2 * hd]

    w0 = pl.multiple_of(t0, QB)
    sw = _dot_nt(qb, kw_scr[pl.ds(w0, wk), :])
    tiles = []
    for j in range(n_wt):
        s = sw[:, j * QB:(j + 1) * QB]
        if j == 0 or j >= n_wt - 2:
            s = s + wb_ref[:, :, j * QB:(j + 1) * QB].reshape(rq, QB)
        if j < n_wt - 1:
            s = s + jnp.where(c + j >= n_wt - 1, 0.0, NEG)
        tiles.append(s)
    mw = tiles[0]
    for s in tiles[1:]:
        mw = jnp.maximum(mw, s)
    mw = jnp.max(mw, axis=-1, keepdims=True)
    pw = jnp.concatenate([jnp.exp(s - mw).astype(BF16) for s in tiles], axis=1)
    accw = _dot(pw, vw_scr[pl.ds(w0, wk), :])
    o_win = accw[:, 0:hd] / accw[:, hd:2 * hd]

    bg = _sigmoid(bg_ref[...] + gb_ref[0])
    ug = ug_ref[...]
    for r in range(r_):
        rows = slice(r * QB, (r + 1) * QB)
        o = (bg[:, 3 * r:3 * r + 1] * o_cmp[rows] + bg[:, 3 * r + 1:3 * r + 2] * o_slc[rows]
             + bg[:, 3 * r + 2:3 * r + 3] * o_win[rows])
        og_ref[:, r * hd:(r + 1) * hd] = (o * ug[:, r * hd:(r + 1) * hd]).astype(og_ref.dtype)


def _attn_prompt(u, bgs, gbias, kv, comp, wbias, tcfix, mimp_t, expand, k_norm,
                 batch, seq, n_kv, n_heads):
    m = u.shape[0]
    hd = k_norm.shape[1]
    r_ = n_heads // n_kv
    hq = n_heads * hd
    nq = seq // QB
    n_c = comp.shape[2]
    n_slc = seq // SLC_BLK
    rq = r_ * QB
    wk = WINDOW + QB
    n_sg = 2 * n_kv
    kv_spec = lambda col0: pl.BlockSpec((seq, hd), lambda b, g, c: (b, col0 + g))
    return pl.pallas_call(
        functools.partial(_attn_prompt_kernel, n_heads_grp=r_, n_slc=n_slc),
        grid=(batch, n_kv, nq),
        in_specs=[
            pl.BlockSpec((QB, r_ * hd), lambda b, g, c: (b * nq + c, g)),
            pl.BlockSpec((QB, r_ * hd), lambda b, g, c: (b * nq + c, n_kv + g)),
            pl.BlockSpec((QB, LANES), lambda b, g, c: (b * nq + c, g)),
            pl.BlockSpec((1, 1, LANES), lambda b, g, c: (g, 0, 0)),
            kv_spec(n_sg), kv_spec(n_sg + n_kv), kv_spec(2 * n_sg), kv_spec(2 * n_sg + n_kv),
            pl.BlockSpec((1, 1, n_c, hd), lambda b, g, c: (b, g, 0, 0)),
            pl.BlockSpec((1, 1, n_c, hd), lambda b, g, c: (b, n_kv + g, 0, 0)),
            pl.BlockSpec((r_, QB, wk), lambda b, g, c: (g, 0, 0)),
            pl.BlockSpec((r_, QB, n_c), lambda b, g, c: (g, 0, 0)),
            pl.BlockSpec((n_slc, n_c), lambda b, g, c: (0, 0)),
            pl.BlockSpec((n_slc, seq), lambda b, g, c: (0, 0)),
            pl.BlockSpec((3, hd), lambda b, g, c: (0, 0)),
        ],
        out_specs=pl.BlockSpec((QB, r_ * hd), lambda b, g, c: (b * nq + c, g)),
        out_shape=jax.ShapeDtypeStruct((m, hq), BF16),
        scratch_shapes=[
            pltpu.VMEM((seq, hd), BF16), pltpu.VMEM((seq, 2 * hd), BF16),
            pltpu.VMEM((seq + WINDOW, hd), BF16), pltpu.VMEM((seq + WINDOW, 2 * hd), BF16),
            pltpu.VMEM((seq // (2 * QB), QB, 2 * QB), F32),
            pltpu.VMEM((seq // (2 * QB), rq, 2 * QB), F32),
            pltpu.VMEM((rq, seq), BF16),
            pltpu.VMEM((rq, 2 * QB), F32),
        ],
        compiler_params=_cparams("arbitrary", "arbitrary", "arbitrary"),
        name="attn_prompt",
    )(u, u, bgs, gbias, kv, kv, kv, kv, comp, comp, wbias, tcfix, mimp_t, expand, k_norm)


def _attn_sample_cmp_kernel(q_ref, kc_ref, vc_ref, cb_ref, mi_ref, qn_ref, oc_ref, sel_ref,
                            *, n_slc):
    r_, hd = q_ref.shape[1], q_ref.shape[2]
    qb = (_rms(q_ref[0], qn_ref[...]) * hd ** -0.5).astype(BF16)
    lc = _dot_nt(qb, kc_ref[0, 0]) + cb_ref[0]
    e = jnp.exp(lc - jnp.max(lc, axis=-1, keepdims=True))
    pcb = (e / jnp.sum(e, axis=-1, keepdims=True)).astype(BF16)
    oc_ref[0] = _dot(pcb, vc_ref[0, 0])
    nsp = mi_ref.shape[1]
    imp = jnp.sum(_dot(pcb, mi_ref[...]), axis=0, keepdims=True)

    jl = lax.broadcasted_iota(jnp.int32, (1, nsp), 1)
    forced = (jl == 0) | (jl == n_slc - 1) | (jl == n_slc - 2)
    score = jnp.where(jl < n_slc, jnp.where(forced, BIG, imp), LOWEST)
    s_rows = jnp.broadcast_to(score, (nsp, nsp))
    kk = lax.broadcasted_iota(jnp.int32, (nsp, nsp), 0)
    jj = lax.broadcasted_iota(jnp.int32, (nsp, nsp), 1)
    s_col = jnp.sum(jnp.where(kk == jj, s_rows, 0.0), axis=1, keepdims=True)
    ahead = jnp.where(s_col > s_rows, 1.0, jnp.where((s_col == s_rows) & (kk < jj), 1.0, 0.0))
    rank = jnp.sum(ahead, axis=0, keepdims=True)
    lane = lax.broadcasted_iota(jnp.int32, (1, LANES), 1)
    out = jnp.zeros((1, LANES), F32)
    jf = jl.astype(F32)
    for k in range(min(TOP_N, n_slc)):
        idx_k = jnp.sum(jnp.where(rank == k, jf, 0.0), axis=1, keepdims=True)
        out = out + jnp.where(lane == k, idx_k, 0.0)
    sel_ref[0, 0] = jnp.broadcast_to(out, (8, LANES)).astype(jnp.int32)


def _attn_sample_cmp(q3, comp, cbias, mimp, q_norm, n_kv, n_slc):
    bs, n_heads, hd = q3.shape
    r_ = n_heads // n_kv
    nr = comp.shape[2]
    nsp = mimp.shape[1]
    return pl.pallas_call(
        functools.partial(_attn_sample_cmp_kernel, n_slc=n_slc),
        grid=(bs, n_kv),
        in_specs=[
            pl.BlockSpec((1, r_, hd), lambda b, g: (b, g, 0)),
            pl.BlockSpec((1, 1, nr, hd), lambda b, g: (b, g, 0, 0)),
            pl.BlockSpec((1, 1, nr, hd), lambda b, g: (b, n_kv + g, 0, 0)),
            pl.BlockSpec((1, r_, nr), lambda b, g: (g, 0, 0)),
            pl.BlockSpec((nr, nsp), lambda b, g: (0, 0)),
            pl.BlockSpec((1, hd), lambda b, g: (0, 0)),
        ],
        out_specs=[pl.BlockSpec((1, r_, hd), lambda b, g: (b, g, 0)),
                   pl.BlockSpec((1, 1, 8, LANES), lambda b, g: (b, g, 0, 0))],
        out_shape=[jax.ShapeDtypeStruct((bs, n_heads, hd), F32),
                   jax.ShapeDtypeStruct((bs, n_kv, 8, LANES), jnp.int32)],
        compiler_params=_cparams("parallel", "parallel"),
        name="attn_sample_cmp",
    )(q3, comp, comp, cbias, mimp, q_norm.reshape(1, hd))


def _attn_sample_kernel(sel_ref, pt_ref, q_ref, gate_ref, bg_ref, gb_ref, oc_ref, ksn_ref, vsn_ref,
                        kw_ref, vw_ref, kwn_ref, vwn_ref, tb_ref, wb_ref, qn_ref, kn_ref, *rest,
                        n_sel, n_slc, n_kv):
    del pt_ref
    k_refs, v_refs, o_ref = rest[:n_sel], rest[n_sel:2 * n_sel], rest[2 * n_sel]
    b, g = pl.program_id(0), pl.program_id(1)
    r_, hd = q_ref.shape[1], q_ref.shape[2]
    qb = (_rms(q_ref[0], qn_ref[...]) * hd ** -0.5).astype(BF16)

    def own_group(ref):
        rows = ref[0, :, 0, 0, :]
        for gi in range(1, n_kv):
            rows = jnp.where(g == gi, ref[0, :, 0, gi, :], rows)
        return rows

    k_new = ksn_ref[0, 0]
    v_new = vsn_ref[0, 0]
    logits, values = [], []
    for k in range(n_sel):
        idx = sel_ref[(b * n_kv + g) * TOP_N + k]
        is_new = idx == n_slc - 1
        kt = jnp.where(is_new, jnp.broadcast_to(k_new, (SLC_BLK, hd)), own_group(k_refs[k]))
        vt = jnp.where(is_new, jnp.broadcast_to(v_new, (SLC_BLK, hd)), own_group(v_refs[k]))
        logits.append(_dot_nt(qb, _rms(kt, kn_ref[1:2]).astype(BF16)) + tb_ref[0, idx])
        values.append(vt.astype(BF16))
    m = logits[0].max(axis=-1, keepdims=True)
    for s in logits[1:]:
        m = jnp.maximum(m, s.max(axis=-1, keepdims=True))
    es = [jnp.exp(s - m) for s in logits]
    den = es[0].sum(axis=-1, keepdims=True)
    for e in es[1:]:
        den = den + e.sum(axis=-1, keepdims=True)
    o_slc = jnp.zeros((r_, hd), F32)
    for e, v in zip(es, values):
        o_slc = o_slc + _dot((e / den).astype(BF16), v)

    n_w = kw_ref.shape[1]
    lw = _dot_nt(qb, _rms(own_group(kw_ref), kn_ref[2:3]).astype(BF16)) + wb_ref[0, :, 0:n_w]
    kwn = _rms(kwn_ref[0, 0], kn_ref[2:3]).astype(BF16).astype(F32)
    l_new = jnp.sum(qb.astype(F32) * kwn, axis=-1, keepdims=True) + wb_ref[0, :, n_w:n_w + 1]
    mw = jnp.maximum(lw.max(axis=-1, keepdims=True), l_new)
    ew, e_new = jnp.exp(lw - mw), jnp.exp(l_new - mw)
    dw = ew.sum(axis=-1, keepdims=True) + e_new
    o_win = (_dot((ew / dw).astype(BF16), own_group(vw_ref).astype(BF16))
             + (e_new / dw).astype(BF16).astype(F32) * vwn_ref[0, 0].astype(BF16).astype(F32))

    bg = _sigmoid(bg_ref[0] + gb_ref[...])
    o = bg[:, 0:1] * oc_ref[0] + bg[:, 1:2] * o_slc + bg[:, 2:3] * o_win
    o_ref[0] = o * _silu(gate_ref[0])


def _attn_sample(sel, page_table, q3, gate3, bg3, gbias2, o_cmp, cache_slc, slc_new, win_state,
                 win_new, tb, wb, q_norm, k_norm, n_kv, n_slc):
    bs, n_heads, hd = q3.shape
    r_ = n_heads // n_kv
    n_pages = page_table.shape[1]
    n_sel = min(TOP_N, n_slc)
    halves = PAGE_SIZE // SLC_BLK
    n_w = win_state.shape[1]

    def blk_map(k, kv_idx):
        def index_map(b, g, sel_ref, pt_ref):
            idx = jnp.minimum(sel_ref[(b * n_kv + g) * TOP_N + k], n_slc - 2)
            return (pt_ref[b * n_pages + idx // halves], idx % halves, kv_idx, 0, 0)
        return index_map

    head_spec = lambda: pl.BlockSpec((1, r_, hd), lambda b, g, s, p: (b, g, 0))
    in_specs = [
        head_spec(), head_spec(),
        pl.BlockSpec((1, r_, 3), lambda b, g, s, p: (b, g, 0)),
        pl.BlockSpec((r_, 3), lambda b, g, s, p: (g, 0)),
        head_spec(),
        pl.BlockSpec((1, 1, 1, hd), lambda b, g, s, p: (b, g, 0, 0)),
        pl.BlockSpec((1, 1, 1, hd), lambda b, g, s, p: (b, n_kv + g, 0, 0)),
        pl.BlockSpec((1, n_w, 1, n_kv, hd), lambda b, g, s, p: (b, 0, 0, 0, 0)),
        pl.BlockSpec((1, n_w, 1, n_kv, hd), lambda b, g, s, p: (b, 0, 1, 0, 0)),
        pl.BlockSpec((1, 1, 1, hd), lambda b, g, s, p: (b, g, 0, 0)),
        pl.BlockSpec((1, 1, 1, hd), lambda b, g, s, p: (b, n_kv + g, 0, 0)),
        pl.BlockSpec((1, n_slc, r_, SLC_BLK), lambda b, g, s, p: (g, 0, 0, 0)),
        pl.BlockSpec((1, r_, wb.shape[2]), lambda b, g, s, p: (g, 0, 0)),
        pl.BlockSpec((1, hd), lambda b, g, s, p: (0, 0)),
        pl.BlockSpec((3, hd), lambda b, g, s, p: (0, 0)),
    ]
    in_specs += [pl.BlockSpec((1, SLC_BLK, 1, n_kv, hd), blk_map(k, 0)) for k in range(n_sel)]
    in_specs += [pl.BlockSpec((1, SLC_BLK, 1, n_kv, hd), blk_map(k, 1)) for k in range(n_sel)]
    return pl.pallas_call(
        functools.partial(_attn_sample_kernel, n_sel=n_sel, n_slc=n_slc, n_kv=n_kv),
        grid_spec=pltpu.PrefetchScalarGridSpec(
            num_scalar_prefetch=2,
            grid=(bs, n_kv),
            in_specs=in_specs,
            out_specs=pl.BlockSpec((1, r_, hd), lambda b, g, s, p: (b, g, 0)),
        ),
        out_shape=jax.ShapeDtypeStruct((bs, n_heads, hd), F32),
        compiler_params=_cparams("arbitrary", "arbitrary"),
        name="attn_sample",
    )(sel, page_table.reshape(-1), q3, gate3, bg3, gbias2, o_cmp, slc_new, slc_new, win_state,
      win_state, win_new, win_new, tb, wb, q_norm.reshape(1, hd), k_norm,
      *([cache_slc] * (2 * n_sel)))


def _bucket_np(d):
    d = np.maximum(d, 0)
    n_exact = N_BUCKETS // 2
    df = np.maximum(d, 1).astype(np.float64)
    large = n_exact + (np.log(df / n_exact) / math.log(MAX_DIST / n_exact)
                       * (N_BUCKETS - n_exact)).astype(np.int64)
    return np.where(d < n_exact, d, np.minimum(large, N_BUCKETS - 1))


def _dist_bias(rel_table, d, valid, shift):
    fd = rel_table.astype(F32)[_bucket_np(np.arange(MAX_DIST + 1))]
    if shift:
        fd = fd - fd[MAX_DIST:MAX_DIST + 1]
    vals = jnp.moveaxis(fd[np.clip(d, 0, MAX_DIST)], -1, 0)
    return jnp.where(jnp.asarray(valid)[None], vals, NEG)


def _overlap_np(n_cmp, n_slc, rows, cols):
    cs = np.arange(n_cmp)[:, None] * CMP_STRIDE
    ss = np.arange(n_slc)[None, :] * SLC_BLK
    ov = np.minimum(cs + CMP_BLK, ss + SLC_BLK) - np.maximum(cs, ss)
    out = np.zeros((rows, cols), np.float32)
    out[:n_cmp, :n_slc] = np.maximum(ov, 0).astype(np.float32) / CMP_BLK
    return out


def _round_up(x, m):
    return (x + m - 1) // m * m


def kernel(x_prompt, x_sample, cache_cmp_kv, cache_slc_kv, state_win_kv, state_lru_h, state_conv,
           page_table, a_norm, a_w_in, a_conv_w, a_conv_b, a_w_rg, a_b_rg, a_w_ig, a_b_ig, a_lambda,
           a_w_out, kv_norm, w_kv, k_norm, cmp_pos, w_cmp1, w_cmp2, rel_table, b_norm, b_w_in,
           b_gate_bias, b_q_norm, b_w_out):
    batch, seq, d_model = x_prompt.shape
    bs = x_sample.shape[0]
    n_a, n_b = a_norm.shape[0], b_norm.shape[0]
    d_rnn = a_w_in.shape[2] // 2
    n_kv, hd = cache_cmp_kv.shape[3], cache_cmp_kv.shape[4]
    n_heads = rel_table.shape[1]
    r_ = n_heads // n_kv
    hq = n_heads * hd
    n_sg = 2 * n_kv
    cols = n_sg * hd
    n_pages = page_table.shape[1]
    past = n_pages * PAGE_SIZE
    m = batch * seq
    assert x_sample.shape[1] == 1 and seq % QB == 0 and seq >= WINDOW and hd == LANES
    assert seq // CMP_STRIDE == LANES and past >= WINDOW and state_win_kv.shape[1] == WINDOW

    xp = x_prompt.reshape(m, d_model)
    xs = jnp.pad(x_sample.reshape(bs, d_model), ((0, SAMPLE_ROWS - bs), (0, 0)))
    pad_rows = lambda a: jnp.pad(a, ((0, SAMPLE_ROWS - bs), (0, 0)))

    p_h, p_c, s_h, s_c = [], [], [], []
    h0_p = jnp.zeros((batch, d_rnn), F32)
    c0_p = jnp.zeros((batch, CONV_W - 1, d_rnn), F32)
    for l in range(n_a):
        wrg, wig = a_w_rg[l].astype(BF16), a_w_ig[l].astype(BF16)
        lru_w = (a_conv_w[l], a_conv_b[l], wrg, a_b_rg[l], wig, a_b_ig[l], a_lambda[l])
        u, us = _matmul(_norm_cast(xp, a_norm[l]), _norm_cast(xs, a_norm[l]), a_w_in, layer=l)
        gp, hl, ct = _lru_prompt(u, batch, seq, *lru_w, h0_p, c0_p)
        c0 = jnp.pad(jnp.swapaxes(state_conv[l], 0, 1), ((0, 0), (0, SAMPLE_ROWS - bs), (0, 0)))
        gs, hs = _lru_sample(us, *lru_w, pad_rows(state_lru_h[l]), c0)
        xp, xs = _matmul(gp, gs, a_w_out, layer=l, res=xp, res_s=xs)
        p_h.append(hl.reshape(batch, d_rnn))
        p_c.append(ct)
        s_h.append(hs[:bs])
        s_c.append(jnp.concatenate([state_conv[l][:, 1:], us[:bs, None, :d_rnn]], axis=1))

    kv, kvs, p_cmp_kv, p_slc_kv, p_win_rows = _matmul(
        _norm_cast(xp, kv_norm), _norm_cast(xs, kv_norm), w_kv[None], tm=512, tn=n_kv * hd,
        rows5=(batch, seq, n_kv, hd))
    kvs = kvs[:bs]
    rows5 = lambda a, n: a.reshape(n, -1, 2, n_kv, hd)
    p_win_kv = p_win_rows[:, seq - WINDOW:]
    s_cmp_kv, s_slc_kv = rows5(kvs[:, :cols], bs), rows5(kvs[:, cols:2 * cols], bs)
    s_win_rows = rows5(kvs[:, 2 * cols:], bs)
    s_win_kv = jnp.concatenate([state_win_kv, s_win_rows], axis=1)[:, 1:]

    half = CMP_BLK // 2
    hid = w_cmp1.shape[3]
    w1ab = jnp.concatenate([w_cmp1[:, :half].reshape(2, half * hd, hid),
                            w_cmp1[:, half:].reshape(2, half * hd, hid)], axis=2).astype(BF16)
    posab = jnp.stack([cmp_pos[:, :half].reshape(2, half * hd),
                       cmp_pos[:, half:].reshape(2, half * hd)], axis=1)
    w2b = w_cmp2.astype(BF16)
    own_pages = jnp.arange(m // PAGE_SIZE, dtype=jnp.int32).reshape(batch, seq // PAGE_SIZE)
    r16_p = _page_gather(p_cmp_kv.reshape(m // PAGE_SIZE, PAGE_SIZE, 2, n_kv, hd), own_pages)
    comp_p = _compress(r16_p, posab, w1ab, w2b, k_norm[0], n_kv)
    comp_s = _compress(_page_gather(cache_cmp_kv, page_table), posab, w1ab, w2b, k_norm[0], n_kv)

    n_c = seq // CMP_STRIDE
    n_cmp_p = (seq - CMP_BLK) // CMP_STRIDE + 1
    n_slc_p = seq // SLC_BLK
    wk = WINDOW + QB
    dv = WINDOW - np.arange(wk + 1)
    vrow = _dist_bias(rel_table, dv, dv >= 0, True)
    wbias = jnp.tile(vrow, (1, QB))[:, :QB * wk].reshape(n_heads, QB, wk)
    dc = (np.arange(QB)[:, None] - (np.arange(n_c)[None, :] - (n_c - QB // CMP_STRIDE)) * CMP_STRIDE
          - (CMP_BLK - 1))
    tcfix = jnp.where(jnp.asarray(dc >= 0)[None], _dist_bias(rel_table, dc, dc >= 0, True), 0.0)
    mimp_t = jnp.asarray(_overlap_np(n_cmp_p, n_slc_p, n_c, n_slc_p).T, BF16)
    expand = jnp.asarray((np.arange(n_slc_p)[:, None] == np.arange(seq)[None, :] // SLC_BLK), BF16)

    total = past + 1
    n_cmp_s = (total - CMP_BLK) // CMP_STRIDE + 1
    n_slc_s = -(-total // SLC_BLK)
    nr_s = past // CMP_STRIDE
    assert n_cmp_s == nr_s - 1 and n_slc_s == past // SLC_BLK + 1
    dcs = past - (np.arange(nr_s) * CMP_STRIDE + CMP_BLK - 1)
    cbias_s = _dist_bias(rel_table, dcs, (dcs >= 0) & (np.arange(nr_s) < n_cmp_s), False)
    cbias_s = cbias_s.reshape(n_kv, r_, nr_s)
    mimp_s = jnp.asarray(_overlap_np(n_cmp_s, n_slc_s, nr_s, _round_up(n_slc_s, LANES)), BF16)
    dss = past - np.arange(n_slc_s * SLC_BLK)
    tb_s = _dist_bias(rel_table, dss, dss >= 0, False)
    tb_s = tb_s.reshape(n_kv, r_, n_slc_s, SLC_BLK).transpose(0, 2, 1, 3)
    dws = WINDOW - np.arange(WINDOW + LANES)
    wb_s = _dist_bias(rel_table, dws, dws >= 0, False).reshape(n_kv, r_, WINDOW + LANES)

    slc_new = kvs[:, cols:2 * cols].reshape(bs, n_sg, 1, hd)
    win_new = kvs[:, 2 * cols:].reshape(bs, n_sg, 1, hd)
    slab_pad = ((0, 0), (0, 0), (0, LANES - 3 * r_))
    for l in range(n_b):
        w_qg = b_w_in[l][:, :2 * hq].astype(BF16)
        w_bg = jnp.pad(b_w_in[l][:, 2 * hq:], ((0, 0), (0, LANES - 3 * n_heads)))
        w_bg = jnp.pad(w_bg[:, :3 * n_heads].reshape(d_model, n_kv, 3 * r_), slab_pad)
        w_bg = w_bg.reshape(d_model, n_kv * LANES)
        gbias = b_gate_bias[l]
        gb_slab = jnp.pad(gbias.reshape(n_kv, 1, 3 * r_), slab_pad)

        xn, xns = _norm_cast(xp, b_norm[l]), _norm_cast(xs, b_norm[l])
        u, us = _matmul(xn, xns, w_qg[None], tn=min(hq, 1024),
                        q_gain=b_q_norm[l].reshape(1, hd) * hd ** -0.5)
        bgs, bgs_s = _matmul(xn, xns, w_bg[None])
        og = _attn_prompt(u, bgs, gb_slab, kv, comp_p, wbias, tcfix, mimp_t, expand,
                          k_norm, batch, seq, n_kv, n_heads)

        us = us[:bs]
        q3 = us[:, :hq].reshape(bs, n_heads, hd)
        gate3 = us[:, hq:].reshape(bs, n_heads, hd)
        bg3 = bgs_s[:bs].reshape(bs, n_kv, LANES)[:, :, :3 * r_].reshape(bs, n_heads, 3)
        o_cmp, sel = _attn_sample_cmp(q3, comp_s, cbias_s, mimp_s, b_q_norm[l], n_kv, n_slc_s)
        os_ = _attn_sample(sel[:, :, 0, :TOP_N].reshape(-1), page_table, q3, gate3, bg3,
                           gbias.reshape(n_heads, 3), o_cmp, cache_slc_kv, slc_new, state_win_kv,
                           win_new, tb_s, wb_s, b_q_norm[l], k_norm, n_kv, n_slc_s)
        xp, xs = _matmul(og, pad_rows(os_.reshape(bs, hq).astype(BF16)), b_w_out, layer=l,
                         res=xp, res_s=xs)

    return (xp.reshape(batch, seq, d_model), xs[:bs].reshape(bs, 1, d_model),
            p_cmp_kv, p_slc_kv, p_win_kv, jnp.stack(p_h), jnp.stack(p_c),
            s_cmp_kv, s_slc_kv, s_win_kv, jnp.stack(s_h), jnp.stack(s_c))
```

```python
import functools
import math

import numpy as np
import jax
import jax.numpy as jnp
from jax import lax
from jax.experimental import pallas as pl
from jax.experimental.pallas import tpu as pltpu

F32 = jnp.float32
BF16 = jnp.bfloat16

EPS = 1e-6
NEG = -1e30
BIG = 1e30
LOWEST = -3e38
LRU_C = 8.0
CONV_W = 4
CMP_BLK = 32
CMP_STRIDE = 16
SLC_BLK = 64
TOP_N = 16
WINDOW = 512
N_BUCKETS = 32
MAX_DIST = 128
PAGE_SIZE = 128
QB = 128
GATHER_PAGES = 8
LANES = 128
SUBLANES = 8
SEG_PAD = 8
SAMPLE_ROWS = 16
VMEM_LIMIT = 48 * 1024 * 1024


def _cparams(*sem, vmem_limit=VMEM_LIMIT):
    return pltpu.CompilerParams(dimension_semantics=sem, vmem_limit_bytes=vmem_limit)


def _dot(a, b):
    return jnp.dot(a, b, preferred_element_type=F32)


def _dot_nt(a, b):
    return lax.dot_general(a, b, (((1,), (1,)), ((), ())), preferred_element_type=F32)


def _rms(x, g):
    return x * lax.rsqrt(jnp.mean(x * x, axis=-1, keepdims=True) + EPS) * g


def _sigmoid(x):
    return jax.nn.sigmoid(x)


def _silu(x):
    return x * jax.nn.sigmoid(x)


def _norm_kernel(x_ref, g_ref, o_ref):
    o_ref[...] = _rms(x_ref[...], g_ref[...]).astype(o_ref.dtype)


def _norm_cast(x, g):
    m, d = x.shape
    tm = min(m, 512)
    return pl.pallas_call(
        _norm_kernel,
        grid=(m // tm,),
        in_specs=[pl.BlockSpec((tm, d), lambda i: (i, 0)),
                  pl.BlockSpec((1, d), lambda i: (0, 0))],
        out_specs=pl.BlockSpec((tm, d), lambda i: (i, 0)),
        out_shape=jax.ShapeDtypeStruct((m, d), BF16),
        compiler_params=_cparams("parallel"),
        name="norm_cast",
    )(x, g.reshape(1, d))


def _mm_kernel(a_ref, as_ref, w_ref, *rest, has_res, n_rows5, q_tiles):
    if q_tiles:
        qg_ref, rest = rest[0], rest[1:]
    if has_res:
        r_ref, rs_ref, o_ref, os_ref = rest[:4]
    else:
        o_ref, os_ref = rest[:2]
    wb_scr = rest[-1]
    rows5_refs = rest[-1 - n_rows5:-1]
    wb_ref = wb_scr if w_ref.dtype != BF16 else w_ref

    @pl.when(pl.program_id(1) == 0)
    def _():
        if w_ref.dtype != BF16:
            wb_scr[...] = w_ref[...].astype(BF16)
        acc_s = _dot(as_ref[...], wb_ref[...])
        os_ref[...] = rs_ref[...] + acc_s if has_res else acc_s

    acc = _dot(a_ref[...], wb_ref[...])
    if q_tiles:
        hd = qg_ref.shape[1]

        @pl.when(pl.program_id(0) < q_tiles)
        def _():
            for h in range(acc.shape[1] // hd):
                x = acc[:, h * hd:(h + 1) * hd]
                o_ref[:, h * hd:(h + 1) * hd] = (
                    x * lax.rsqrt(jnp.mean(x * x, axis=-1, keepdims=True) + EPS) * qg_ref[...])

        @pl.when(pl.program_id(0) >= q_tiles)
        def _():
            o_ref[...] = _silu(acc)
    else:
        o_ref[...] = r_ref[...] + acc if has_res else acc

    for br, r5_ref in enumerate(rows5_refs):
        @pl.when(pl.program_id(0) // 2 == br)
        def _(r5_ref=r5_ref):
            n_grp, hd = r5_ref.shape[3], r5_ref.shape[4]
            for g in range(n_grp):
                r5_ref[0, :, 0, g, :] = acc[:, g * hd:(g + 1) * hd]


def _mm_vmem_bytes(tm, tn, k, ms, w_dtype, has_res):
    w_bytes = jnp.dtype(w_dtype).itemsize
    blocks = (tm + ms) * k * 2 + k * tn * w_bytes + (tm + ms) * tn * 4 * (2 if has_res else 1)
    cast = k * tn * 2 if w_bytes != 2 else 0
    return 2 * blocks + cast + 3 * tm * tn * 4


def _matmul(a, a_s, w, layer=0, n=None, res=None, res_s=None, tm=1024, tn=512, rows5=None,
            q_gain=None):
    m, k = a.shape
    ms = a_s.shape[0]
    n = w.shape[2] if n is None else n
    tm = min(m, tm)
    tn = min(n, tn)
    has_res = res is not None
    in_specs = [pl.BlockSpec((tm, k), lambda j, i: (i, 0)),
                pl.BlockSpec((ms, k), lambda j, i: (0, 0)),
                pl.BlockSpec((None, k, tn), lambda j, i: (layer, 0, j))]
    args = [a, a_s, w]
    q_tiles = 0
    if q_gain is not None:
        assert not has_res and rows5 is None and (n // 2) % tn == 0 and tn % q_gain.shape[1] == 0
        q_tiles = n // 2 // tn
        in_specs.append(pl.BlockSpec(q_gain.shape, lambda j, i: (0, 0)))
        args.append(q_gain)
    if has_res:
        in_specs += [pl.BlockSpec((tm, tn), lambda j, i: (i, j)),
                     pl.BlockSpec((ms, tn), lambda j, i: (0, j))]
        args += [res, res_s]
    out_specs = [pl.BlockSpec((tm, tn), lambda j, i: (i, j)),
                 pl.BlockSpec((ms, tn), lambda j, i: (0, j))]
    out_shape = [jax.ShapeDtypeStruct((m, n), F32), jax.ShapeDtypeStruct((ms, n), F32)]
    n_rows5 = 0
    if rows5 is not None:
        batch, seq, n_kv, hd = rows5
        n_rows5 = n // (2 * tn)
        nt = seq // tm
        assert tn == n_kv * hd and seq % tm == 0 and n % (2 * tn) == 0

        def rows5_map(br):
            def index_map(j, i):
                active, before = j // 2 == br, j < 2 * br
                park = lambda first, last: jnp.where(before, first, last)
                return (jnp.where(active, i // nt, park(0, batch - 1)),
                        jnp.where(active, i % nt, park(0, nt - 1)),
                        jnp.where(active, j % 2, park(0, 1)), 0, 0)
            return index_map

        for br in range(n_rows5):
            out_specs.append(pl.BlockSpec((1, tm, 1, n_kv, hd), rows5_map(br)))
            out_shape.append(jax.ShapeDtypeStruct((batch, seq, 2, n_kv, hd), F32))
    return pl.pallas_call(
        functools.partial(_mm_kernel, has_res=has_res, n_rows5=n_rows5, q_tiles=q_tiles),
        grid=(n // tn, m // tm),
        in_specs=in_specs,
        out_specs=out_specs,
        out_shape=out_shape,
        scratch_shapes=[pltpu.VMEM((k, tn) if w.dtype != BF16 else (SUBLANES, LANES), BF16)],
        compiler_params=_cparams("arbitrary", "arbitrary",
                                 vmem_limit=max(VMEM_LIMIT, _mm_vmem_bytes(tm, tn, k, ms, w.dtype,
                                                                           has_res))),
        name="matmul",
    )(*args)


def _lru_coeffs(xc, wrg_ref, wig_ref, brg, big, lam, pos0_row):
    rows, cb = xc.shape
    bs = wrg_ref.shape[-1]
    xcb = xc.astype(BF16)
    r_parts, i_parts = [], []
    for n in range(cb // bs):
        xs = xcb[:, n * bs:(n + 1) * bs]
        r_parts.append(_dot(xs, wrg_ref[n]))
        i_parts.append(_dot(xs, wig_ref[n]))
    r = _sigmoid(jnp.concatenate(r_parts, axis=1) + brg)
    i = _sigmoid(jnp.concatenate(i_parts, axis=1) + big)
    nl = -lam
    softplus = jnp.maximum(nl, 0.0) + jnp.log1p(jnp.exp(-jnp.abs(nl)))
    log_a = -LRU_C * r * softplus
    a = jnp.exp(log_a)
    mult = jnp.sqrt(-jnp.tanh(log_a) * (a * a + 1.0))
    if pos0_row is not None:
        row = lax.broadcasted_iota(jnp.int32, (rows, cb), 0)
        mult = jnp.where(row == pos0_row, 1.0, mult)
    return a, mult * i * xc


def _lru_prompt_kernel(xb_ref, gate_ref, cw_ref, cb_ref, wrg_ref, brg_ref, wig_ref, big_ref,
                       lam_ref, h0_ref, c0_ref, g_ref, hl_ref, ct_ref, st_scr, un_scr, tail_scr, h_scr):
    tc = pl.program_id(2)
    tt, cb = xb_ref.shape
    nlb = cb // LANES
    seg = tt // SUBLANES
    tail = (CONV_W - 1) * SUBLANES

    @pl.when(tc == 0)
    def _():
        h_scr[...] = jnp.broadcast_to(h0_ref[0], (SUBLANES, cb))
        for i in range(CONV_W - 1):
            tail_scr[i * SUBLANES:(i + 1) * SUBLANES, :] = jnp.broadcast_to(c0_ref[0, i:i + 1, :],
                                                                             (SUBLANES, cb))

    pitch = seg + SEG_PAD

    def to_segments(src_ref, slot):
        cols = []
        for lb in range(nlb):
            for j in range(SUBLANES):
                st_scr[slot, lb, j * pitch:j * pitch + seg, :] = src_ref[j * seg:(j + 1) * seg,
                                                                         lb * LANES:(lb + 1) * LANES]
            cols.append(jnp.concatenate(
                [st_scr[slot, lb, pl.ds(k, SUBLANES, stride=pitch), :] for k in range(seg)], axis=0))
        return jnp.concatenate(cols, axis=1)

    x = to_segments(xb_ref, 0)

    sub = lax.broadcasted_iota(jnp.int32, (SUBLANES, cb), 0)
    heads = []
    for i in range(CONV_W - 1):
        cur = x[tt - tail + i * SUBLANES:tt - tail + (i + 1) * SUBLANES]
        prev = tail_scr[i * SUBLANES:(i + 1) * SUBLANES, :]
        heads.append(pltpu.roll(jnp.where(sub == SUBLANES - 1, prev, cur), 1, axis=0))
    tail_scr[...] = x[tt - tail:tt]
    w = cw_ref[...]
    xc = cb_ref[...]
    for k in range(CONV_W - 1):
        m = CONV_W - 1 - k
        xc = xc + w[k:k + 1] * jnp.concatenate(heads[CONV_W - 1 - m:] + [x[0:tt - m * SUBLANES]], axis=0)
    xc = xc + w[CONV_W - 1:CONV_W] * x

    a, b_in = _lru_coeffs(xc, wrg_ref, wig_ref, brg_ref[...], big_ref[...], lam_ref[...],
                           jnp.where(tc == 0, 0, -1))

    blk = lambda v, k: v[k * SUBLANES:(k + 1) * SUBLANES]
    e, p = blk(b_in, 0), blk(a, 0)
    for k in range(1, seg):
        e = blk(a, k) * e + blk(b_in, k)
        p = blk(a, k) * p
    carry = h_scr[0:1, :]
    carries = [carry]
    for j in range(SUBLANES - 1):
        carry = p[j:j + 1] * carry + e[j:j + 1]
        carries.append(carry)
    h = jnp.concatenate(carries, axis=0)
    hs = []
    for k in range(seg):
        h = blk(a, k) * h + blk(b_in, k)
        hs.append(h)
    h_scr[...] = jnp.broadcast_to(h[SUBLANES - 1:SUBLANES], (SUBLANES, cb))
    g = jnp.concatenate(hs, axis=0) * _silu(to_segments(gate_ref, 1))

    for lb in range(nlb):
        for k in range(seg):
            un_scr[lb, pl.ds(k, SUBLANES, stride=pitch), :] = g[k * SUBLANES:(k + 1) * SUBLANES,
                                                                lb * LANES:(lb + 1) * LANES]
        for j in range(SUBLANES):
            g_ref[j * seg:(j + 1) * seg, lb * LANES:(lb + 1) * LANES] = (
                un_scr[lb, j * pitch:j * pitch + seg, :].astype(g_ref.dtype))

    @pl.when(tc == pl.num_programs(2) - 1)
    def _():
        hl_ref[0] = h[SUBLANES - 1:SUBLANES]
        ct_ref[0] = xb_ref[tt - (CONV_W - 1):tt, :]


def _lru_prompt(u, batch, seq, cw, cb_, wrg, brg, wig, big, lam, h0, c0):
    m, c2 = u.shape
    c = c2 // 2
    cb = min(c, 1024)
    tt = min(seq, 512)
    nt = seq // tt
    ncb = c // cb
    bs = wrg.shape[-1]
    nb = cb // bs
    vec = lambda: pl.BlockSpec((1, cb), lambda b, j, t: (0, j))
    return pl.pallas_call(
        _lru_prompt_kernel,
        grid=(batch, ncb, nt),
        in_specs=[
            pl.BlockSpec((tt, cb), lambda b, j, t: (b * nt + t, j)),
            pl.BlockSpec((tt, cb), lambda b, j, t: (b * nt + t, ncb + j)),
            pl.BlockSpec((CONV_W, cb), lambda b, j, t: (0, j)),
            vec(),
            pl.BlockSpec((nb, bs, bs), lambda b, j, t: (j, 0, 0)),
            vec(),
            pl.BlockSpec((nb, bs, bs), lambda b, j, t: (j, 0, 0)),
            vec(),
            vec(),
            pl.BlockSpec((1, 1, cb), lambda b, j, t: (b, 0, j)),
            pl.BlockSpec((1, CONV_W - 1, cb), lambda b, j, t: (b, 0, j)),
        ],
        out_specs=[
            pl.BlockSpec((tt, cb), lambda b, j, t: (b * nt + t, j)),
            pl.BlockSpec((1, 1, cb), lambda b, j, t: (b, 0, j)),
            pl.BlockSpec((1, CONV_W - 1, cb), lambda b, j, t: (b, 0, j)),
        ],
        out_shape=[
            jax.ShapeDtypeStruct((m, c), BF16),
            jax.ShapeDtypeStruct((batch, 1, c), F32),
            jax.ShapeDtypeStruct((batch, CONV_W - 1, c), F32),
        ],
        scratch_shapes=[pltpu.VMEM((2, cb // LANES, tt + SUBLANES * SEG_PAD, LANES), F32),
                        pltpu.VMEM((cb // LANES, tt + SUBLANES * SEG_PAD, LANES), F32),
                        pltpu.VMEM(((CONV_W - 1) * SUBLANES, cb), F32),
                        pltpu.VMEM((SUBLANES, cb), F32)],
        compiler_params=_cparams("parallel", "parallel", "arbitrary"),
        name="lru_prompt",
    )(u, u, cw, cb_.reshape(1, c), wrg, brg.reshape(1, c), wig, big.reshape(1, c),
      lam.reshape(1, c), h0.reshape(batch, 1, c), c0)


def _lru_sample_kernel(xb_ref, gate_ref, cw_ref, cb_ref, wrg_ref, brg_ref, wig_ref, big_ref,
                       lam_ref, h0_ref, c0_ref, g_ref, h_ref):
    w = cw_ref[...]
    xc = cb_ref[...]
    for k in range(CONV_W - 1):
        xc = xc + w[k:k + 1] * c0_ref[k]
    xc = xc + w[CONV_W - 1:CONV_W] * xb_ref[...]
    a, b_in = _lru_coeffs(xc, wrg_ref, wig_ref, brg_ref[...], big_ref[...], lam_ref[...], None)
    h = a * h0_ref[...] + b_in
    h_ref[...] = h
    g_ref[...] = (h * _silu(gate_ref[...])).astype(g_ref.dtype)


def _lru_sample(u, cw, cb_, wrg, brg, wig, big, lam, h0, c0):
    rows, c2 = u.shape
    c = c2 // 2
    cb = min(c, 1024)
    ncb = c // cb
    bs = wrg.shape[-1]
    nb = cb // bs
    vec = lambda: pl.BlockSpec((1, cb), lambda j: (0, j))
    return pl.pallas_call(
        _lru_sample_kernel,
        grid=(ncb,),
        in_specs=[
            pl.BlockSpec((rows, cb), lambda j: (0, j)),
            pl.BlockSpec((rows, cb), lambda j: (0, ncb + j)),
            pl.BlockSpec((CONV_W, cb), lambda j: (0, j)),
            vec(),
            pl.BlockSpec((nb, bs, bs), lambda j: (j, 0, 0)),
            vec(),
            pl.BlockSpec((nb, bs, bs), lambda j: (j, 0, 0)),
            vec(),
            vec(),
            pl.BlockSpec((rows, cb), lambda j: (0, j)),
            pl.BlockSpec((CONV_W - 1, rows, cb), lambda j: (0, 0, j)),
        ],
        out_specs=[pl.BlockSpec((rows, cb), lambda j: (0, j)),
                   pl.BlockSpec((rows, cb), lambda j: (0, j))],
        out_shape=[jax.ShapeDtypeStruct((rows, c), BF16),
                   jax.ShapeDtypeStruct((rows, c), F32)],
        compiler_params=_cparams("parallel"),
        name="lru_sample",
    )(u, u, cw, cb_.reshape(1, c), wrg, brg.reshape(1, c), wig, big.reshape(1, c),
      lam.reshape(1, c), h0, c0)


def _page_gather_kernel(pt_ref, *refs):
    del pt_ref
    c_refs, o_ref, slab_scr = refs[:GATHER_PAGES], refs[GATHER_PAGES], refs[GATHER_PAGES + 1]
    n_kv, hd = c_refs[0].shape[3], c_refs[0].shape[4]
    rows = PAGE_SIZE // CMP_STRIDE
    for q, c_ref in enumerate(c_refs):
        for sg in range(2 * n_kv):
            slab_scr[q, sg] = c_ref[0, :, sg // n_kv, sg % n_kv, :]
            for l in range(CMP_STRIDE):
                o_ref[0, sg, q * rows:(q + 1) * rows, l * hd:(l + 1) * hd] = (
                    slab_scr[q, sg, pl.ds(l, rows, stride=CMP_STRIDE), :])


def _page_gather(pages, page_table):
    bs, n_pages = page_table.shape
    n_kv, hd = pages.shape[3], pages.shape[4]
    n_sg = 2 * n_kv
    rows = PAGE_SIZE // CMP_STRIDE
    assert n_pages % GATHER_PAGES == 0

    def page_map(q):
        return lambda b, j, pt: (pt[b * n_pages + j * GATHER_PAGES + q], 0, 0, 0, 0)

    return pl.pallas_call(
        _page_gather_kernel,
        grid_spec=pltpu.PrefetchScalarGridSpec(
            num_scalar_prefetch=1,
            grid=(bs, n_pages // GATHER_PAGES),
            in_specs=[pl.BlockSpec((1, PAGE_SIZE, 2, n_kv, hd), page_map(q))
                      for q in range(GATHER_PAGES)],
            out_specs=pl.BlockSpec((1, n_sg, GATHER_PAGES * rows, CMP_STRIDE * hd),
                                   lambda b, j, pt: (b, 0, j, 0)),
            scratch_shapes=[pltpu.VMEM((GATHER_PAGES, n_sg, PAGE_SIZE, hd), F32)],
        ),
        out_shape=jax.ShapeDtypeStruct((bs, n_sg, n_pages * rows, CMP_STRIDE * hd), F32),
        compiler_params=_cparams("parallel", "arbitrary"),
        name="page_gather",
    )(page_table.reshape(-1), *([pages] * GATHER_PAGES))


def _compress_kernel(r_ref, pos_ref, w1_ref, w2_ref, kn_ref, o_ref, q_scr, *, n_kv):
    sg = pl.program_id(1)
    rows = r_ref[0, 0]
    nr = rows.shape[0]
    hid = w1_ref.shape[2] // 2
    xa = (rows + pos_ref[0, 0:1]).astype(BF16)
    xb = (rows + pos_ref[0, 1:2]).astype(BF16)
    w1 = w1_ref[0]
    p = _dot(xa, w1[:, :hid])
    q_scr[0:nr, :] = _dot(xb, w1[:, hid:])
    q_scr[nr:nr + 8, :] = jnp.zeros((8, hid), F32)
    pre = p + q_scr[1:nr + 1, :]
    comp = _dot(_silu(pre).astype(BF16), w2_ref[0])
    normed = _rms(comp, kn_ref[...])
    o_ref[0, 0] = jnp.where(sg < n_kv, normed, comp).astype(o_ref.dtype)


def _compress(r16, posab, w1ab, w2, k_norm0, n_kv):
    b, n_sg, nr, kk = r16.shape
    hid2 = w1ab.shape[2]
    hd = w2.shape[2]
    return pl.pallas_call(
        functools.partial(_compress_kernel, n_kv=n_kv),
        grid=(b, n_sg),
        in_specs=[
            pl.BlockSpec((1, 1, nr, kk), lambda i, s: (i, s, 0, 0)),
            pl.BlockSpec((1, 2, kk), lambda i, s: (s // n_kv, 0, 0)),
            pl.BlockSpec((1, kk, hid2), lambda i, s: (s // n_kv, 0, 0)),
            pl.BlockSpec((1, hid2 // 2, hd), lambda i, s: (s // n_kv, 0, 0)),
            pl.BlockSpec((1, hd), lambda i, s: (0, 0)),
        ],
        out_specs=pl.BlockSpec((1, 1, nr, hd), lambda i, s: (i, s, 0, 0)),
        out_shape=jax.ShapeDtypeStruct((b, n_sg, nr, hd), BF16),
        scratch_shapes=[pltpu.VMEM((nr + 8, hid2 // 2), F32)],
        compiler_params=_cparams("parallel", "arbitrary"),
        name="compress",
    )(r16, posab, w1ab, w2, k_norm0.reshape(1, hd))


def _attn_prompt_kernel(uq_ref, ug_ref, bg_ref, gb_ref, ks_ref, vs_ref, kw_ref, vw_ref,
                        kc_ref, vc_ref, wb_ref, tc_ref, mit_ref, ex_ref, kn_ref, og_ref,
                        ks_scr, vs_scr, kw_scr, vw_scr, madd_scr, lg_scr, p_scr, m_scr,
                        *, n_heads_grp, n_slc):
    r_ = n_heads_grp
    c = pl.program_id(2)
    t0 = c * QB
    seq, hd = ks_ref.shape
    rq = r_ * QB
    wk = WINDOW + QB
    n_wt = wk // QB

    @pl.when(c == 0)
    def _():
        ks_scr[...] = _rms(ks_ref[...], kn_ref[1:2]).astype(BF16)
        vs_scr[:, 0:hd] = vs_ref[...].astype(BF16)
        vs_scr[:, hd:2 * hd] = jnp.ones((seq, hd), BF16)
        kw_scr[0:WINDOW, :] = jnp.zeros((WINDOW, hd), BF16)
        vw_scr[0:WINDOW, :] = jnp.zeros((WINDOW, 2 * hd), BF16)
        kw_scr[WINDOW:WINDOW + seq, :] = _rms(kw_ref[...], kn_ref[2:3]).astype(BF16)
        vw_scr[WINDOW:WINDOW + seq, 0:hd] = vw_ref[...].astype(BF16)
        vw_scr[WINDOW:WINDOW + seq, hd:2 * hd] = jnp.ones((seq, hd), BF16)
        lg_scr[...] = jnp.zeros(lg_scr.shape, F32)

    uq = uq_ref[...]
    qb = jnp.concatenate([uq[:, r * hd:(r + 1) * hd] for r in range(r_)],
                         axis=0).astype(BF16)

    n_c = kc_ref.shape[2]
    bias_c = pltpu.roll(tc_ref[...], (c * (QB // CMP_STRIDE) + QB // CMP_STRIDE) % n_c, axis=2)
    lc = _dot_nt(qb, kc_ref[0, 0]).reshape(r_, QB, n_c) + bias_c
    tq = t0 + lax.broadcasted_iota(jnp.int32, (QB, n_c), 0)
    nn = lax.broadcasted_iota(jnp.int32, (QB, n_c), 1)
    c_ok = (tq - (nn * CMP_STRIDE + (CMP_BLK - 1)) >= 0)[None]
    lc = jnp.where(c_ok, lc, NEG)
    e = jnp.exp(lc - jnp.max(lc, axis=-1, keepdims=True))
    pc = jnp.where(c_ok, e / jnp.sum(e, axis=-1, keepdims=True), 0.0)
    pcb = pc.reshape(rq, n_c).astype(BF16)
    o_cmp = _dot(pcb, vc_ref[0, 0])
    imp_all = _dot_nt(mit_ref[...], pcb)
    imp = imp_all[:, 0:QB]
    for r in range(1, r_):
        imp = imp + imp_all[:, r * QB:(r + 1) * QB]

    jj = lax.broadcasted_iota(jnp.int32, (n_slc, QB), 0)
    cur = (t0 + lax.broadcasted_iota(jnp.int32, (n_slc, QB), 1)) // SLC_BLK
    forced = (jj == 0) | (jj == cur) | (jj == cur - 1)
    allowed = jj <= cur
    score = jnp.where(forced, BIG, jnp.where(allowed, imp, NEG))
    rank = jnp.zeros((n_slc, QB), F32)
    for k in range(n_slc):
        sk = score[k:k + 1, :]
        rank = rank + jnp.where(sk > score, 1.0, jnp.where((sk == score) & (k < jj), 1.0, 0.0))
    sel = jnp.where((rank < min(TOP_N, n_slc)) & allowed, 1.0, 0.0).astype(BF16)
    selx = lax.dot_general(sel, ex_ref[...], (((0,), (0,)), ((), ())),
                           preferred_element_type=F32)
    tk = 2 * QB
    for kp in range(seq // tk):
        madd_scr[kp] = (selx[:, kp * tk:(kp + 1) * tk] - 1.0) * BIG

    m_scr[...] = jnp.full((rq, tk), NEG, F32)
    near1 = wb_ref[:, :, WINDOW - QB:WINDOW]
    near0 = wb_ref[:, :, WINDOW:WINDOW + QB]

    def logits_pair(kp, near):
        k0 = pl.multiple_of(kp * tk, tk)
        s3 = _dot_nt(qb, ks_scr[pl.ds(k0, tk), :]).reshape(r_, QB, tk) + madd_scr[kp][None]
        if near:
            halves = []
            for h in range(2):
                dist = c - (2 * kp + h)
                bias = jnp.where(dist == 0, near0, jnp.where(dist == 1, near1, 0.0))
                halves.append(s3[:, :, h * QB:(h + 1) * QB] + bias)
            s3 = jnp.concatenate(halves, axis=2)
        s = s3.reshape(rq, tk)
        lg_scr[kp] = s
        m_scr[...] = jnp.maximum(m_scr[...], s)

    def far_body(kp, carry):
        logits_pair(kp, False)
        return carry

    last = c // 2
    prev = jnp.maximum(last - 1, 0)
    logits_pair(last, True)
    logits_pair(prev, True)
    lax.fori_loop(0, prev, far_body, 0)

    m_scr[...] = jnp.broadcast_to(jnp.max(m_scr[...], axis=-1, keepdims=True), (rq, tk))
    for kp in range(seq // tk):
        p = jnp.where(kp <= last, jnp.exp(lg_scr[kp] - m_scr[...]), 0.0)
        p_scr[:, kp * tk:(kp + 1) * tk] = p.astype(BF16)
    acc = _dot(p_scr[...], vs_scr[...])
    o_slc = acc[:, 0:hd] / acc[:, hd:2 * hd]

    w0 = pl.multiple_of(t0, QB)
    sw = _dot_nt(qb, kw_scr[pl.ds(w0, wk), :])
    tiles = []
    for j in range(n_wt):
        s = sw[:, j * QB:(j + 1) * QB]
        if j == 0 or j >= n_wt - 2:
            s = s + wb_ref[:, :, j * QB:(j + 1) * QB].reshape(rq, QB)
        if j < n_wt - 1:
            s = s + jnp.where(c + j >= n_wt - 1, 0.0, NEG)
        tiles.append(s)
    mw = tiles[0]
    for s in tiles[1:]:
        mw = jnp.maximum(mw, s)
    mw = jnp.max(mw, axis=-1, keepdims=True)
    pw = jnp.concatenate([jnp.exp(s - mw).astype(BF16) for s in tiles], axis=1)
    accw = _dot(pw, vw_scr[pl.ds(w0, wk), :])
    o_win = accw[:, 0:hd] / accw[:, hd:2 * hd]

    bg = _sigmoid(bg_ref[...] + gb_ref[0])
    ug = ug_ref[...]
    for r in range(r_):
        rows = slice(r * QB, (r + 1) * QB)
        o = (bg[:, 3 * r:3 * r + 1] * o_cmp[rows] + bg[:, 3 * r + 1:3 * r + 2] * o_slc[rows]
             + bg[:, 3 * r + 2:3 * r + 3] * o_win[rows])
        og_ref[:, r * hd:(r + 1) * hd] = (o * ug[:, r * hd:(r + 1) * hd]).astype(og_ref.dtype)


def _attn_prompt(u, bgs, gbias, kv, comp, wbias, tcfix, mimp_t, expand, k_norm,
                 batch, seq, n_kv, n_heads):
    m = u.shape[0]
    hd = k_norm.shape[1]
    r_ = n_heads // n_kv
    hq = n_heads * hd
    nq = seq // QB
    n_c = comp.shape[2]
    n_slc = seq // SLC_BLK
    rq = r_ * QB
    wk = WINDOW + QB
    n_sg = 2 * n_kv
    kv_spec = lambda col0: pl.BlockSpec((seq, hd), lambda b, g, c: (b, col0 + g))
    return pl.pallas_call(
        functools.partial(_attn_prompt_kernel, n_heads_grp=r_, n_slc=n_slc),
        grid=(batch, n_kv, nq),
        in_specs=[
            pl.BlockSpec((QB, r_ * hd), lambda b, g, c: (b * nq + c, g)),
            pl.BlockSpec((QB, r_ * hd), lambda b, g, c: (b * nq + c, n_kv + g)),
            pl.BlockSpec((QB, LANES), lambda b, g, c: (b * nq + c, g)),
            pl.BlockSpec((1, 1, LANES), lambda b, g, c: (g, 0, 0)),
            kv_spec(n_sg), kv_spec(n_sg + n_kv), kv_spec(2 * n_sg), kv_spec(2 * n_sg + n_kv),
            pl.BlockSpec((1, 1, n_c, hd), lambda b, g, c: (b, g, 0, 0)),
            pl.BlockSpec((1, 1, n_c, hd), lambda b, g, c: (b, n_kv + g, 0, 0)),
            pl.BlockSpec((r_, QB, wk), lambda b, g, c: (g, 0, 0)),
            pl.BlockSpec((r_, QB, n_c), lambda b, g, c: (g, 0, 0)),
            pl.BlockSpec((n_slc, n_c), lambda b, g, c: (0, 0)),
            pl.BlockSpec((n_slc, seq), lambda b, g, c: (0, 0)),
            pl.BlockSpec((3, hd), lambda b, g, c: (0, 0)),
        ],
        out_specs=pl.BlockSpec((QB, r_ * hd), lambda b, g, c: (b * nq + c, g)),
        out_shape=jax.ShapeDtypeStruct((m, hq), BF16),
        scratch_shapes=[
            pltpu.VMEM((seq, hd), BF16), pltpu.VMEM((seq, 2 * hd), BF16),
            pltpu.VMEM((seq + WINDOW, hd), BF16), pltpu.VMEM((seq + WINDOW, 2 * hd), BF16),
            pltpu.VMEM((seq // (2 * QB), QB, 2 * QB), F32),
            pltpu.VMEM((seq // (2 * QB), rq, 2 * QB), F32),
            pltpu.VMEM((rq, seq), BF16),
            pltpu.VMEM((rq, 2 * QB), F32),
        ],
        compiler_params=_cparams("arbitrary", "arbitrary", "arbitrary"),
        name="attn_prompt",
    )(u, u, bgs, gbias, kv, kv, kv, kv, comp, comp, wbias, tcfix, mimp_t, expand, k_norm)


def _attn_sample_cmp_kernel(q_ref, kc_ref, vc_ref, cb_ref, mi_ref, qn_ref, oc_ref, sel_ref,
                            *, n_slc):
    r_, hd = q_ref.shape[1], q_ref.shape[2]
    qb = (_rms(q_ref[0], qn_ref[...]) * hd ** -0.5).astype(BF16)
    lc = _dot_nt(qb, kc_ref[0, 0]) + cb_ref[0]
    e = jnp.exp(lc - jnp.max(lc, axis=-1, keepdims=True))
    pcb = (e / jnp.sum(e, axis=-1, keepdims=True)).astype(BF16)
    oc_ref[0] = _dot(pcb, vc_ref[0, 0])
    nsp = mi_ref.shape[1]
    imp = jnp.sum(_dot(pcb, mi_ref[...]), axis=0, keepdims=True)

    jl = lax.broadcasted_iota(jnp.int32, (1, nsp), 1)
    forced = (jl == 0) | (jl == n_slc - 1) | (jl == n_slc - 2)
    score = jnp.where(jl < n_slc, jnp.where(forced, BIG, imp), LOWEST)
    s_rows = jnp.broadcast_to(score, (nsp, nsp))
    kk = lax.broadcasted_iota(jnp.int32, (nsp, nsp), 0)
    jj = lax.broadcasted_iota(jnp.int32, (nsp, nsp), 1)
    s_col = jnp.sum(jnp.where(kk == jj, s_rows, 0.0), axis=1, keepdims=True)
    ahead = jnp.where(s_col > s_rows, 1.0, jnp.where((s_col == s_rows) & (kk < jj), 1.0, 0.0))
    rank = jnp.sum(ahead, axis=0, keepdims=True)
    lane = lax.broadcasted_iota(jnp.int32, (1, LANES), 1)
    out = jnp.zeros((1, LANES), F32)
    jf = jl.astype(F32)
    for k in range(min(TOP_N, n_slc)):
        idx_k = jnp.sum(jnp.where(rank == k, jf, 0.0), axis=1, keepdims=True)
        out = out + jnp.where(lane == k, idx_k, 0.0)
    sel_ref[0, 0] = jnp.broadcast_to(out, (8, LANES)).astype(jnp.int32)


def _attn_sample_cmp(q3, comp, cbias, mimp, q_norm, n_kv, n_slc):
    bs, n_heads, hd = q3.shape
    r_ = n_heads // n_kv
    nr = comp.shape[2]
    nsp = mimp.shape[1]
    return pl.pallas_call(
        functools.partial(_attn_sample_cmp_kernel, n_slc=n_slc),
        grid=(bs, n_kv),
        in_specs=[
            pl.BlockSpec((1, r_, hd), lambda b, g: (b, g, 0)),
            pl.BlockSpec((1, 1, nr, hd), lambda b, g: (b, g, 0, 0)),
            pl.BlockSpec((1, 1, nr, hd), lambda b, g: (b, n_kv + g, 0, 0)),
            pl.BlockSpec((1, r_, nr), lambda b, g: (g, 0, 0)),
            pl.BlockSpec((nr, nsp), lambda b, g: (0, 0)),
            pl.BlockSpec((1, hd), lambda b, g: (0, 0)),
        ],
        out_specs=[pl.BlockSpec((1, r_, hd), lambda b, g: (b, g, 0)),
                   pl.BlockSpec((1, 1, 8, LANES), lambda b, g: (b, g, 0, 0))],
        out_shape=[jax.ShapeDtypeStruct((bs, n_heads, hd), F32),
                   jax.ShapeDtypeStruct((bs, n_kv, 8, LANES), jnp.int32)],
        compiler_params=_cparams("parallel", "parallel"),
        name="attn_sample_cmp",
    )(q3, comp, comp, cbias, mimp, q_norm.reshape(1, hd))


def _attn_sample_kernel(sel_ref, pt_ref, q_ref, gate_ref, bg_ref, gb_ref, oc_ref, ksn_ref, vsn_ref,
                        kw_ref, vw_ref, kwn_ref, vwn_ref, tb_ref, wb_ref, qn_ref, kn_ref, *rest,
                        n_sel, n_slc, n_kv):
    del pt_ref
    k_refs, v_refs, o_ref = rest[:n_sel], rest[n_sel:2 * n_sel], rest[2 * n_sel]
    b, g = pl.program_id(0), pl.program_id(1)
    r_, hd = q_ref.shape[1], q_ref.shape[2]
    qb = (_rms(q_ref[0], qn_ref[...]) * hd ** -0.5).astype(BF16)

    def own_group(ref):
        rows = ref[0, :, 0, 0, :]
        for gi in range(1, n_kv):
            rows = jnp.where(g == gi, ref[0, :, 0, gi, :], rows)
        return rows

    k_new = ksn_ref[0, 0]
    v_new = vsn_ref[0, 0]
    logits, values = [], []
    for k in range(n_sel):
        idx = sel_ref[(b * n_kv + g) * TOP_N + k]
        is_new = idx == n_slc - 1
        kt = jnp.where(is_new, jnp.broadcast_to(k_new, (SLC_BLK, hd)), own_group(k_refs[k]))
        vt = jnp.where(is_new, jnp.broadcast_to(v_new, (SLC_BLK, hd)), own_group(v_refs[k]))
        logits.append(_dot_nt(qb, _rms(kt, kn_ref[1:2]).astype(BF16)) + tb_ref[0, idx])
        values.append(vt.astype(BF16))
    m = logits[0].max(axis=-1, keepdims=True)
    for s in logits[1:]:
        m = jnp.maximum(m, s.max(axis=-1, keepdims=True))
    es = [jnp.exp(s - m) for s in logits]
    den = es[0].sum(axis=-1, keepdims=True)
    for e in es[1:]:
        den = den + e.sum(axis=-1, keepdims=True)
    o_slc = jnp.zeros((r_, hd), F32)
    for e, v in zip(es, values):
        o_slc = o_slc + _dot((e / den).astype(BF16), v)

    n_w = kw_ref.shape[1]
    lw = _dot_nt(qb, _rms(own_group(kw_ref), kn_ref[2:3]).astype(BF16)) + wb_ref[0, :, 0:n_w]
    kwn = _rms(kwn_ref[0, 0], kn_ref[2:3]).astype(BF16).astype(F32)
    l_new = jnp.sum(qb.astype(F32) * kwn, axis=-1, keepdims=True) + wb_ref[0, :, n_w:n_w + 1]
    mw = jnp.maximum(lw.max(axis=-1, keepdims=True), l_new)
    ew, e_new = jnp.exp(lw - mw), jnp.exp(l_new - mw)
    dw = ew.sum(axis=-1, keepdims=True) + e_new
    o_win = (_dot((ew / dw).astype(BF16), own_group(vw_ref).astype(BF16))
             + (e_new / dw).astype(BF16).astype(F32) * vwn_ref[0, 0].astype(BF16).astype(F32))

    bg = _sigmoid(bg_ref[0] + gb_ref[...])
    o = bg[:, 0:1] * oc_ref[0] + bg[:, 1:2] * o_slc + bg[:, 2:3] * o_win
    o_ref[0] = o * _silu(gate_ref[0])


def _attn_sample(sel, page_table, q3, gate3, bg3, gbias2, o_cmp, cache_slc, slc_new, win_state,
                 win_new, tb, wb, q_norm, k_norm, n_kv, n_slc):
    bs, n_heads, hd = q3.shape
    r_ = n_heads // n_kv
    n_pages = page_table.shape[1]
    n_sel = min(TOP_N, n_slc)
    halves = PAGE_SIZE // SLC_BLK
    n_w = win_state.shape[1]

    def blk_map(k, kv_idx):
        def index_map(b, g, sel_ref, pt_ref):
            idx = jnp.minimum(sel_ref[(b * n_kv + g) * TOP_N + k], n_slc - 2)
            return (pt_ref[b * n_pages + idx // halves], idx % halves, kv_idx, 0, 0)
        return index_map

    head_spec = lambda: pl.BlockSpec((1, r_, hd), lambda b, g, s, p: (b, g, 0))
    in_specs = [
        head_spec(), head_spec(),
        pl.BlockSpec((1, r_, 3), lambda b, g, s, p: (b, g, 0)),
        pl.BlockSpec((r_, 3), lambda b, g, s, p: (g, 0)),
        head_spec(),
        pl.BlockSpec((1, 1, 1, hd), lambda b, g, s, p: (b, g, 0, 0)),
        pl.BlockSpec((1, 1, 1, hd), lambda b, g, s, p: (b, n_kv + g, 0, 0)),
        pl.BlockSpec((1, n_w, 1, n_kv, hd), lambda b, g, s, p: (b, 0, 0, 0, 0)),
        pl.BlockSpec((1, n_w, 1, n_kv, hd), lambda b, g, s, p: (b, 0, 1, 0, 0)),
        pl.BlockSpec((1, 1, 1, hd), lambda b, g, s, p: (b, g, 0, 0)),
        pl.BlockSpec((1, 1, 1, hd), lambda b, g, s, p: (b, n_kv + g, 0, 0)),
        pl.BlockSpec((1, n_slc, r_, SLC_BLK), lambda b, g, s, p: (g, 0, 0, 0)),
        pl.BlockSpec((1, r_, wb.shape[2]), lambda b, g, s, p: (g, 0, 0)),
        pl.BlockSpec((1, hd), lambda b, g, s, p: (0, 0)),
        pl.BlockSpec((3, hd), lambda b, g, s, p: (0, 0)),
    ]
    in_specs += [pl.BlockSpec((1, SLC_BLK, 1, n_kv, hd), blk_map(k, 0)) for k in range(n_sel)]
    in_specs += [pl.BlockSpec((1, SLC_BLK, 1, n_kv, hd), blk_map(k, 1)) for k in range(n_sel)]
    return pl.pallas_call(
        functools.partial(_attn_sample_kernel, n_sel=n_sel, n_slc=n_slc, n_kv=n_kv),
        grid_spec=pltpu.PrefetchScalarGridSpec(
            num_scalar_prefetch=2,
            grid=(bs, n_kv),
            in_specs=in_specs,
            out_specs=pl.BlockSpec((1, r_, hd), lambda b, g, s, p: (b, g, 0)),
        ),
        out_shape=jax.ShapeDtypeStruct((bs, n_heads, hd), F32),
        compiler_params=_cparams("arbitrary", "arbitrary"),
        name="attn_sample",
    )(sel, page_table.reshape(-1), q3, gate3, bg3, gbias2, o_cmp, slc_new, slc_new, win_state,
      win_state, win_new, win_new, tb, wb, q_norm.reshape(1, hd), k_norm,
      *([cache_slc] * (2 * n_sel)))


def _bucket_np(d):
    d = np.maximum(d, 0)
    n_exact = N_BUCKETS // 2
    df = np.maximum(d, 1).astype(np.float64)
    large = n_exact + (np.log(df / n_exact) / math.log(MAX_DIST / n_exact)
                       * (N_BUCKETS - n_exact)).astype(np.int64)
    return np.where(d < n_exact, d, np.minimum(large, N_BUCKETS - 1))


def _dist_bias(rel_table, d, valid, shift):
    fd = rel_table.astype(F32)[_bucket_np(np.arange(MAX_DIST + 1))]
    if shift:
        fd = fd - fd[MAX_DIST:MAX_DIST + 1]
    vals = jnp.moveaxis(fd[np.clip(d, 0, MAX_DIST)], -1, 0)
    return jnp.where(jnp.asarray(valid)[None], vals, NEG)


def _overlap_np(n_cmp, n_slc, rows, cols):
    cs = np.arange(n_cmp)[:, None] * CMP_STRIDE
    ss = np.arange(n_slc)[None, :] * SLC_BLK
    ov = np.minimum(cs + CMP_BLK, ss + SLC_BLK) - np.maximum(cs, ss)
    out = np.zeros((rows, cols), np.float32)
    out[:n_cmp, :n_slc] = np.maximum(ov, 0).astype(np.float32) / CMP_BLK
    return out


def _round_up(x, m):
    return (x + m - 1) // m * m


def kernel(x_prompt, x_sample, cache_cmp_kv, cache_slc_kv, state_win_kv, state_lru_h, state_conv,
           page_table, a_norm, a_w_in, a_conv_w, a_conv_b, a_w_rg, a_b_rg, a_w_ig, a_b_ig, a_lambda,
           a_w_out, kv_norm, w_kv, k_norm, cmp_pos, w_cmp1, w_cmp2, rel_table, b_norm, b_w_in,
           b_gate_bias, b_q_norm, b_w_out):
    batch, seq, d_model = x_prompt.shape
    bs = x_sample.shape[0]
    n_a, n_b = a_norm.shape[0], b_norm.shape[0]
    d_rnn = a_w_in.shape[2] // 2
    n_kv, hd = cache_cmp_kv.shape[3], cache_cmp_kv.shape[4]
    n_heads = rel_table.shape[1]
    r_ = n_heads // n_kv
    hq = n_heads * hd
    n_sg = 2 * n_kv
    cols = n_sg * hd
    n_pages = page_table.shape[1]
    past = n_pages * PAGE_SIZE
    m = batch * seq
    assert x_sample.shape[1] == 1 and seq % QB == 0 and seq >= WINDOW and hd == LANES
    assert seq // CMP_STRIDE == LANES and past >= WINDOW and state_win_kv.shape[1] == WINDOW

    xp = x_prompt.reshape(m, d_model)
    xs = jnp.pad(x_sample.reshape(bs, d_model), ((0, SAMPLE_ROWS - bs), (0, 0)))
    pad_rows = lambda a: jnp.pad(a, ((0, SAMPLE_ROWS - bs), (0, 0)))

    p_h, p_c, s_h, s_c = [], [], [], []
    h0_p = jnp.zeros((batch, d_rnn), F32)
    c0_p = jnp.zeros((batch, CONV_W - 1, d_rnn), F32)
    for l in range(n_a):
        wrg, wig = a_w_rg[l].astype(BF16), a_w_ig[l].astype(BF16)
        lru_w = (a_conv_w[l], a_conv_b[l], wrg, a_b_rg[l], wig, a_b_ig[l], a_lambda[l])
        u, us = _matmul(_norm_cast(xp, a_norm[l]), _norm_cast(xs, a_norm[l]), a_w_in, layer=l)
        gp, hl, ct = _lru_prompt(u, batch, seq, *lru_w, h0_p, c0_p)
        c0 = jnp.pad(jnp.swapaxes(state_conv[l], 0, 1), ((0, 0), (0, SAMPLE_ROWS - bs), (0, 0)))
        gs, hs = _lru_sample(us, *lru_w, pad_rows(state_lru_h[l]), c0)
        xp, xs = _matmul(gp, gs, a_w_out, layer=l, res=xp, res_s=xs)
        p_h.append(hl.reshape(batch, d_rnn))
        p_c.append(ct)
        s_h.append(hs[:bs])
        s_c.append(jnp.concatenate([state_conv[l][:, 1:], us[:bs, None, :d_rnn]], axis=1))

    kv, kvs, p_cmp_kv, p_slc_kv, p_win_rows = _matmul(
        _norm_cast(xp, kv_norm), _norm_cast(xs, kv_norm), w_kv[None], tm=512, tn=n_kv * hd,
        rows5=(batch, seq, n_kv, hd))
    kvs = kvs[:bs]
    rows5 = lambda a, n: a.reshape(n, -1, 2, n_kv, hd)
    p_win_kv = p_win_rows[:, seq - WINDOW:]
    s_cmp_kv, s_slc_kv = rows5(kvs[:, :cols], bs), rows5(kvs[:, cols:2 * cols], bs)
    s_win_rows = rows5(kvs[:, 2 * cols:], bs)
    s_win_kv = jnp.concatenate([state_win_kv, s_win_rows], axis=1)[:, 1:]

    half = CMP_BLK // 2
    hid = w_cmp1.shape[3]
    w1ab = jnp.concatenate([w_cmp1[:, :half].reshape(2, half * hd, hid),
                            w_cmp1[:, half:].reshape(2, half * hd, hid)], axis=2).astype(BF16)
    posab = jnp.stack([cmp_pos[:, :half].reshape(2, half * hd),
                       cmp_pos[:, half:].reshape(2, half * hd)], axis=1)
    w2b = w_cmp2.astype(BF16)
    own_pages = jnp.arange(m // PAGE_SIZE, dtype=jnp.int32).reshape(batch, seq // PAGE_SIZE)
    r16_p = _page_gather(p_cmp_kv.reshape(m // PAGE_SIZE, PAGE_SIZE, 2, n_kv, hd), own_pages)
    comp_p = _compress(r16_p, posab, w1ab, w2b, k_norm[0], n_kv)
    comp_s = _compress(_page_gather(cache_cmp_kv, page_table), posab, w1ab, w2b, k_norm[0], n_kv)

    n_c = seq // CMP_STRIDE
    n_cmp_p = (seq - CMP_BLK) // CMP_STRIDE + 1
    n_slc_p = seq // SLC_BLK
    wk = WINDOW + QB
    dv = WINDOW - np.arange(wk + 1)
    vrow = _dist_bias(rel_table, dv, dv >= 0, True)
    wbias = jnp.tile(vrow, (1, QB))[:, :QB * wk].reshape(n_heads, QB, wk)
    dc = (np.arange(QB)[:, None] - (np.arange(n_c)[None, :] - (n_c - QB // CMP_STRIDE)) * CMP_STRIDE
          - (CMP_BLK - 1))
    tcfix = jnp.where(jnp.asarray(dc >= 0)[None], _dist_bias(rel_table, dc, dc >= 0, True), 0.0)
    mimp_t = jnp.asarray(_overlap_np(n_cmp_p, n_slc_p, n_c, n_slc_p).T, BF16)
    expand = jnp.asarray((np.arange(n_slc_p)[:, None] == np.arange(seq)[None, :] // SLC_BLK), BF16)

    total = past + 1
    n_cmp_s = (total - CMP_BLK) // CMP_STRIDE + 1
    n_slc_s = -(-total // SLC_BLK)
    nr_s = past // CMP_STRIDE
    assert n_cmp_s == nr_s - 1 and n_slc_s == past // SLC_BLK + 1
    dcs = past - (np.arange(nr_s) * CMP_STRIDE + CMP_BLK - 1)
    cbias_s = _dist_bias(rel_table, dcs, (dcs >= 0) & (np.arange(nr_s) < n_cmp_s), False)
    cbias_s = cbias_s.reshape(n_kv, r_, nr_s)
    mimp_s = jnp.asarray(_overlap_np(n_cmp_s, n_slc_s, nr_s, _round_up(n_slc_s, LANES)), BF16)
    dss = past - np.arange(n_slc_s * SLC_BLK)
    tb_s = _dist_bias(rel_table, dss, dss >= 0, False)
    tb_s = tb_s.reshape(n_kv, r_, n_slc_s, SLC_BLK).transpose(0, 2, 1, 3)
    dws = WINDOW - np.arange(WINDOW + LANES)
    wb_s = _dist_bias(rel_table, dws, dws >= 0, False).reshape(n_kv, r_, WINDOW + LANES)

    slc_new = kvs[:, cols:2 * cols].reshape(bs, n_sg, 1, hd)
    win_new = kvs[:, 2 * cols:].reshape(bs, n_sg, 1, hd)
    slab_pad = ((0, 0), (0, 0), (0, LANES - 3 * r_))
    for l in range(n_b):
        w_qg = b_w_in[l][:, :2 * hq].astype(BF16)
        w_bg = jnp.pad(b_w_in[l][:, 2 * hq:], ((0, 0), (0, LANES - 3 * n_heads)))
        w_bg = jnp.pad(w_bg[:, :3 * n_heads].reshape(d_model, n_kv, 3 * r_), slab_pad)
        w_bg = w_bg.reshape(d_model, n_kv * LANES)
        gbias = b_gate_bias[l]
        gb_slab = jnp.pad(gbias.reshape(n_kv, 1, 3 * r_), slab_pad)

        xn, xns = _norm_cast(xp, b_norm[l]), _norm_cast(xs, b_norm[l])
        u, us = _matmul(xn, xns, w_qg[None], tn=min(hq, 1024),
                        q_gain=b_q_norm[l].reshape(1, hd) * hd ** -0.5)
        bgs, bgs_s = _matmul(xn, xns, w_bg[None])
        og = _attn_prompt(u, bgs, gb_slab, kv, comp_p, wbias, tcfix, mimp_t, expand,
                          k_norm, batch, seq, n_kv, n_heads)

        us = us[:bs]
        q3 = us[:, :hq].reshape(bs, n_heads, hd)
        gate3 = us[:, hq:].reshape(bs, n_heads, hd)
        bg3 = bgs_s[:bs].reshape(bs, n_kv, LANES)[:, :, :3 * r_].reshape(bs, n_heads, 3)
        o_cmp, sel = _attn_sample_cmp(q3, comp_s, cbias_s, mimp_s, b_q_norm[l], n_kv, n_slc_s)
        os_ = _attn_sample(sel[:, :, 0, :TOP_N].reshape(-1), page_table, q3, gate3, bg3,
                           gbias.reshape(n_heads, 3), o_cmp, cache_slc_kv, slc_new, state_win_kv,
                           win_new, tb_s, wb_s, b_q_norm[l], k_norm, n_kv, n_slc_s)
        xp, xs = _matmul(og, pad_rows(os_.reshape(bs, hq).astype(BF16)), b_w_out, layer=l,
                         res=xp, res_s=xs)

    return (xp.reshape(batch, seq, d_model), xs[:bs].reshape(bs, 1, d_model),
            p_cmp_kv, p_slc_kv, p_win_kv, jnp.stack(p_h), jnp.stack(p_c),
            s_cmp_kv, s_slc_kv, s_win_kv, jnp.stack(s_h), jnp.stack(s_c))
```
